```python
import math
import jax, jax.numpy as jnp
from jax import lax
import numpy as np

D_MODEL = 2048
BATCH = 4
SEQ = 2048
DEPTH = 1
DEC_BATCH = 128
DEC_SEQ = 8
PAST_LEN = 16384
PAGE_SIZE = 128

N_META = 16
MIX_WIDTH = D_MODEL
GLA_WIDTH = MIX_WIDTH // 2
GLA_HEADS = 4
GLA_DV = GLA_WIDTH // GLA_HEADS
GLA_DK = GLA_DV // 2
GLA_KEY_WIDTH = GLA_HEADS * GLA_DK
GLA_RANK = 16
GLA_GATE_NORM = 16.0
GLA_CHUNK = 64
S5_WIDTH = MIX_WIDTH - GLA_WIDTH
S5_GROUP = 16
S5_GROUPS = S5_WIDTH // S5_GROUP
S5_STATE = 64
N_GROUPS = 4
EXPERTS_PER_GROUP = 8
N_EXPERTS = N_GROUPS * EXPERTS_PER_GROUP
EXPERT_HIDDEN = D_MODEL // 4
TOP_K = 2
EXPERT_BLOCK = 128
EPS = 1e-6
IN_COLS = 2 * GLA_KEY_WIDTH + 2 * GLA_WIDTH + GLA_RANK + S5_WIDTH

kernel_name = 'hymba_gla_s5_hmoe_step'


def rmsnorm(x, g):
    xf = x.astype(jnp.float32)
    y = xf * lax.rsqrt(jnp.mean(xf * xf, axis=-1, keepdims=True) + EPS) * g.astype(jnp.float32)
    return y.astype(x.dtype)


def gla_chunk(S, q, k, v, g):
    C = q.shape[1]
    b = jnp.cumsum(g, axis=1)
    o_inter = jnp.einsum('bchk,bhkv->bchv', q * jnp.exp(b), S)
    mask = jnp.tril(jnp.ones((C, C), bool))[None, :, :, None, None]
    diff = b[:, :, None] - b[:, None, :]
    decay = jnp.where(mask, jnp.exp(jnp.where(mask, diff, 0.0)), 0.0)
    scores = jnp.einsum('bihk,bjhk,bijhk->bhij', q, k, decay)
    o_intra = jnp.einsum('bhij,bjhv->bihv', scores, v)
    b_last = b[:, -1]
    k_dec = k * jnp.exp(b_last[:, None] - b)
    S_new = jnp.exp(b_last)[..., None] * S + jnp.einsum('bchk,bchv->bhkv', k_dec, v)
    return o_inter + o_intra, S_new


def gla_sequence(S0, q, k, v, g, n_lead, chunk):
    outs = []
    S = S0
    if n_lead > 0:
        o, S = gla_chunk(S, q[:, :n_lead], k[:, :n_lead], v[:, :n_lead], g[:, :n_lead])
        outs.append(o)
        q, k, v, g = q[:, n_lead:], k[:, n_lead:], v[:, n_lead:], g[:, n_lead:]
    Bsz, T = q.shape[:2]
    nc = T // chunk

    def to_chunks(t):
        return jnp.moveaxis(t.reshape((Bsz, nc, chunk) + t.shape[2:]), 1, 0)

    def step(S, inp):
        o, S = gla_chunk(S, *inp)
        return S, o

    S, o = lax.scan(step, S, (to_chunks(q), to_chunks(k), to_chunks(v), to_chunks(g)))
    outs.append(jnp.moveaxis(o, 0, 1).reshape(Bsz, T, GLA_HEADS, GLA_DV))
    return jnp.concatenate(outs, axis=1), S


def _complex_affine_combine(e1, e2):
    a1r, a1i, b1r, b1i = e1
    a2r, a2i, b2r, b2i = e2
    ar = a2r * a1r - a2i * a1i
    ai = a2r * a1i + a2i * a1r
    br = a2r * b1r - a2i * b1i + b2r
    bi = a2r * b1i + a2i * b1r + b2i
    return (ar, ai, br, bi)


def s5_mix(u, h0_re, h0_im, lam_re, lam_im, log_dt, b_re, b_im, c_re, c_im, d_skip):
    f32 = jnp.float32
    Bsz, T = u.shape[:2]
    u = u.astype(f32).reshape(Bsz, T, S5_GROUPS, S5_GROUP)
    dt = jnp.exp(log_dt.astype(f32))[:, None]
    lr, li = lam_re.astype(f32), lam_im.astype(f32)
    mag = jnp.exp(lr * dt)
    ab_re, ab_im = mag * jnp.cos(li * dt), mag * jnp.sin(li * dt)
    den = lr * lr + li * li
    f_re = ((ab_re - 1.0) * lr + ab_im * li) / den
    f_im = (ab_im * lr - (ab_re - 1.0) * li) / den
    br, bi = b_re.astype(f32), b_im.astype(f32)
    bb_re = f_re[..., None] * br - f_im[..., None] * bi
    bb_im = f_re[..., None] * bi + f_im[..., None] * br
    bu_re = jnp.einsum('gpc,btgc->btgp', bb_re, u)
    bu_im = jnp.einsum('gpc,btgc->btgp', bb_im, u)
    h0r, h0i = h0_re.astype(f32), h0_im.astype(f32)
    bu_re = bu_re.at[:, 0].add(ab_re * h0r - ab_im * h0i)
    bu_im = bu_im.at[:, 0].add(ab_re * h0i + ab_im * h0r)
    a_re = jnp.broadcast_to(ab_re, bu_re.shape)
    a_im = jnp.broadcast_to(ab_im, bu_im.shape)
    _, _, hr, hi = lax.associative_scan(_complex_affine_combine, (a_re, a_im, bu_re, bu_im), axis=1)
    y = (jnp.einsum('gcp,btgp->btgc', c_re.astype(f32), hr)
         - jnp.einsum('gcp,btgp->btgc', c_im.astype(f32), hi)
         + d_skip.astype(f32) * u)
    return y.reshape(Bsz, T, S5_WIDTH), hr[:, -1], hi[:, -1]


def hier_moe(xt, w_rg, b_rg, w_re, b_re, w_gate, w_up, w_down):
    f32 = jnp.float32
    T = xt.shape[0]
    xf = xt.astype(f32)
    lg = xf @ w_rg.astype(f32) + b_rg.astype(f32)
    pg = jax.nn.softmax(lg, axis=-1)
    gsel = jnp.argmax(lg, axis=-1)
    pg_sel = jnp.take_along_axis(pg, gsel[:, None], axis=-1)[:, 0]
    le = jnp.einsum('td,gde->tge', xf, w_re.astype(f32)) + b_re.astype(f32)
    le_sel = jnp.take_along_axis(le, gsel[:, None, None], axis=1)[:, 0]
    vals, idx = lax.top_k(le_sel, TOP_K)
    pe = jax.nn.softmax(vals, axis=-1)
    eid = (gsel[:, None] * EXPERTS_PER_GROUP + idx).reshape(-1).astype(jnp.int32)
    wts = (pg_sel[:, None] * pe).reshape(-1)
    tok = jnp.repeat(jnp.arange(T, dtype=jnp.int32), TOP_K)
    A = T * TOP_K
    order = jnp.argsort(eid)
    eid_s, tok_s, w_s = eid[order], tok[order], wts[order]
    counts = jax.ops.segment_sum(jnp.ones((A,), jnp.int32), eid, num_segments=N_EXPERTS)
    starts = jnp.cumsum(counts) - counts
    padded = (counts + EXPERT_BLOCK - 1) // EXPERT_BLOCK * EXPERT_BLOCK
    pends = jnp.cumsum(padded)
    pstarts = pends - padded
    dest = pstarts[eid_s] + jnp.arange(A, dtype=jnp.int32) - starts[eid_s]
    nb = -(-A // EXPERT_BLOCK) + N_EXPERTS
    slot_tok = jnp.full((nb * EXPERT_BLOCK,), T, jnp.int32).at[dest].set(tok_s)
    slot_w = jnp.zeros((nb * EXPERT_BLOCK,), f32).at[dest].set(w_s)
    block_exp = jnp.minimum(jnp.searchsorted(pends, jnp.arange(nb, dtype=jnp.int32) * EXPERT_BLOCK, side='right'),
                            N_EXPERTS - 1)
    xpad = jnp.concatenate([xt, jnp.zeros((1, xt.shape[1]), xt.dtype)], axis=0)

    def run_block(args):
        idx_b, e = args
        xb = xpad[idx_b]
        hb = jax.nn.silu(xb @ w_gate[e]) * (xb @ w_up[e])
        return hb @ w_down[e]

    out = lax.map(run_block, (slot_tok.reshape(nb, EXPERT_BLOCK), block_exp))
    y = jnp.zeros((T + 1, xt.shape[1]), f32).at[slot_tok].add(
        out.reshape(-1, xt.shape[1]).astype(f32) * slot_w[:, None])
    return y[:T].astype(xt.dtype)


def layer(h, S_gla, s_re, s_im, n_lead, chunk, p):
    f32 = jnp.float32
    Bsz, T, _ = h.shape
    xn = rmsnorm(h, p['g_mix'])
    proj = xn @ p['w_in']
    sizes = [GLA_KEY_WIDTH, GLA_KEY_WIDTH, GLA_WIDTH, GLA_WIDTH, GLA_RANK, S5_WIDTH]
    q, k, v, r, a_low, u = jnp.split(proj, np.cumsum(sizes)[:-1].tolist(), axis=-1)
    q = q.astype(f32).reshape(Bsz, T, GLA_HEADS, GLA_DK) * (GLA_DK ** -0.5)
    k = k.astype(f32).reshape(Bsz, T, GLA_HEADS, GLA_DK)
    v = v.astype(f32).reshape(Bsz, T, GLA_HEADS, GLA_DV)
    gk = jax.nn.log_sigmoid((a_low @ p['w_gk2'] + p['b_gk']).astype(f32)) / GLA_GATE_NORM
    gk = gk.reshape(Bsz, T, GLA_HEADS, GLA_DK)
    o, S_new = gla_sequence(S_gla.astype(f32), q, k, v, gk, n_lead, chunk)
    o = o * lax.rsqrt(jnp.mean(o * o, axis=-1, keepdims=True) + EPS) * p['g_gla'].astype(f32)
    o_gla = o.reshape(Bsz, T, GLA_WIDTH) * jax.nn.silu(r.astype(f32))
    y5, hr, hi = s5_mix(u, s_re, s_im, p['lam_re'], p['lam_im'], p['log_dt'], p['s5_b_re'], p['s5_b_im'],
                        p['s5_c_re'], p['s5_c_im'], p['d_skip'])
    z = jax.nn.gelu(y5)
    o_s5 = z * jax.nn.sigmoid(z @ p['w_glu'].astype(f32) + p['b_glu'].astype(f32))
    mixed = jnp.concatenate([o_gla, o_s5], axis=-1).astype(h.dtype)
    h = h + mixed @ p['w_out']
    hn = rmsnorm(h, p['g_ffn'])
    h = h + hier_moe(hn.reshape(-1, D_MODEL), p['w_rg'], p['b_rg'], p['w_re'], p['b_re'],
                     p['w_gate'], p['w_up'], p['w_down']).reshape(h.shape)
    return h, S_new, hr, hi


def setup_inputs(seed: int = 0) -> dict:
    key = jax.random.key(seed)
    ks = jax.random.split(key, 40)
    f32 = jnp.float32
    nrm = lambda i, shape, s: jax.random.normal(ks[i], shape, f32) * s
    L, D, G, P = DEPTH, D_MODEL, S5_GROUPS, S5_STATE
    lam_im_base = math.pi * jnp.arange(P, dtype=f32)
    return {
        'x_prompt': nrm(0, (BATCH, SEQ, D), 1.0),
        'x_sample': nrm(1, (DEC_BATCH, DEC_SEQ, D), 1.0),
        'state_gla': nrm(2, (L, DEC_BATCH, GLA_HEADS, GLA_DK, GLA_DV), 0.1),
        'state_s5_re': nrm(3, (L, DEC_BATCH, G, P), 0.5),
        'state_s5_im': nrm(4, (L, DEC_BATCH, G, P), 0.5),
        'meta': nrm(5, (N_META, D), 1.0),
        'g_mix': 1.0 + nrm(6, (L, D), 0.01),
        'w_in': nrm(7, (L, D, IN_COLS), D ** -0.5),
        'w_gk2': nrm(8, (L, GLA_RANK, GLA_KEY_WIDTH), GLA_RANK ** -0.5),
        'b_gk': nrm(9, (L, GLA_KEY_WIDTH), 0.1),
        'g_gla': 1.0 + nrm(10, (L, GLA_HEADS, GLA_DV), 0.01),
        'lam_re': -0.5 + nrm(11, (L, G, P), 0.01),
        'lam_im': lam_im_base + nrm(12, (L, G, P), 0.01),
        'log_dt': jax.random.uniform(ks[13], (L, G), f32, math.log(1e-3), math.log(1e-1)),
        's5_b_re': nrm(14, (L, G, P, S5_GROUP), (2 * S5_GROUP) ** -0.5),
        's5_b_im': nrm(15, (L, G, P, S5_GROUP), (2 * S5_GROUP) ** -0.5),
        's5_c_re': nrm(16, (L, G, S5_GROUP, P), (2 * P) ** -0.5),
        's5_c_im': nrm(17, (L, G, S5_GROUP, P), (2 * P) ** -0.5),
        'd_skip': nrm(18, (L, G, S5_GROUP), 1.0),
        'w_glu': nrm(19, (L, S5_WIDTH, S5_WIDTH), S5_WIDTH ** -0.5),
        'b_glu': nrm(20, (L, S5_WIDTH), 0.01),
        'w_out': nrm(21, (L, MIX_WIDTH, D), MIX_WIDTH ** -0.5),
        'g_ffn': 1.0 + nrm(22, (L, D), 0.01),
        'w_rg': nrm(23, (L, D, N_GROUPS), D ** -0.5),
        'b_rg': nrm(24, (L, N_GROUPS), 0.01),
        'w_re': nrm(25, (L, N_GROUPS, D, EXPERTS_PER_GROUP), D ** -0.5),
        'b_re': nrm(26, (L, N_GROUPS, EXPERTS_PER_GROUP), 0.01),
        'w_gate': nrm(27, (L, N_EXPERTS, D, EXPERT_HIDDEN), D ** -0.5),
        'w_up': nrm(28, (L, N_EXPERTS, D, EXPERT_HIDDEN), D ** -0.5),
        'w_down': nrm(29, (L, N_EXPERTS, EXPERT_HIDDEN, D), EXPERT_HIDDEN ** -0.5),
        'g_final': 1.0 + nrm(30, (D,), 0.01),
    }


def reference(x_prompt, x_sample, state_gla, state_s5_re, state_s5_im, meta, g_mix, w_in, w_gk2, b_gk,
              g_gla, lam_re, lam_im, log_dt, s5_b_re, s5_b_im, s5_c_re, s5_c_im, d_skip, w_glu, b_glu,
              w_out, g_ffn, w_rg, b_rg, w_re, b_re, w_gate, w_up, w_down, g_final):
    f32 = jnp.float32
    h_p = jnp.concatenate([jnp.broadcast_to(meta.astype(x_prompt.dtype), (x_prompt.shape[0], N_META, D_MODEL)),
                           x_prompt], axis=1)
    h_s = x_sample
    Bp, Bs = x_prompt.shape[0], x_sample.shape[0]
    chunk_s = GLA_CHUNK if x_sample.shape[1] % GLA_CHUNK == 0 else x_sample.shape[1]
    gla_p, re_p, im_p, gla_s, re_s, im_s = [], [], [], [], [], []
    for l in range(DEPTH):
        p = {'g_mix': g_mix[l], 'w_in': w_in[l], 'w_gk2': w_gk2[l], 'b_gk': b_gk[l], 'g_gla': g_gla[l],
             'lam_re': lam_re[l], 'lam_im': lam_im[l], 'log_dt': log_dt[l], 's5_b_re': s5_b_re[l],
             's5_b_im': s5_b_im[l], 's5_c_re': s5_c_re[l], 's5_c_im': s5_c_im[l], 'd_skip': d_skip[l],
             'w_glu': w_glu[l], 'b_glu': b_glu[l], 'w_out': w_out[l], 'g_ffn': g_ffn[l], 'w_rg': w_rg[l],
             'b_rg': b_rg[l], 'w_re': w_re[l], 'b_re': b_re[l], 'w_gate': w_gate[l], 'w_up': w_up[l],
             'w_down': w_down[l]}
        h_p, Sg, sr, si = layer(h_p, jnp.zeros((Bp, GLA_HEADS, GLA_DK, GLA_DV), f32),
                                jnp.zeros((Bp, S5_GROUPS, S5_STATE), f32),
                                jnp.zeros((Bp, S5_GROUPS, S5_STATE), f32), N_META, GLA_CHUNK, p)
        gla_p.append(Sg); re_p.append(sr); im_p.append(si)
        h_s, Sg, sr, si = layer(h_s, state_gla[l], state_s5_re[l], state_s5_im[l], 0, chunk_s, p)
        gla_s.append(Sg); re_s.append(sr); im_s.append(si)
    y_prompt = rmsnorm(h_p, g_final)[:, N_META:]
    y_sample = rmsnorm(h_s, g_final)
    new_gla_prompt = jnp.stack(gla_p)
    new_s5_re_prompt = jnp.stack(re_p)
    new_s5_im_prompt = jnp.stack(im_p)
    new_gla_sample = jnp.stack(gla_s)
    new_s5_re_sample = jnp.stack(re_s)
    new_s5_im_sample = jnp.stack(im_s)
    return (y_prompt, y_sample, new_gla_prompt, new_s5_re_prompt, new_s5_im_prompt,
            new_gla_sample, new_s5_re_sample, new_s5_im_sample)
```

```python
import functools
import math

import jax
import jax.numpy as jnp
from jax import lax
from jax.experimental import pallas as pl
from jax.experimental.pallas import tpu as pltpu

F32 = jnp.float32
BF16 = jnp.bfloat16

D_MODEL = 2048
N_META = 16
GLA_HEADS = 4
GLA_DK = 128
GLA_DV = 256
GLA_KEY_WIDTH = GLA_HEADS * GLA_DK
GLA_WIDTH = GLA_HEADS * GLA_DV
GLA_RANK = 16
GLA_GATE_NORM = 16.0
GLA_CHUNK = 64
GLA_SUB = 16
S5_WIDTH = 1024
S5_GROUP = 16
S5_GROUPS = 64
S5_STATE = 64
S5_GB = 8
S5_NGB = S5_GROUPS // S5_GB
S5_UL = S5_GB * S5_GROUP
S5_SL = S5_GB * S5_STATE
S5_LT = 2 * S5_SL // 128
N_GROUPS = 4
EXPERTS_PER_GROUP = 8
N_EXPERTS = N_GROUPS * EXPERTS_PER_GROUP
EXPERT_HIDDEN = 512
TOP_K = 2
EPS = 1e-6
ROUTER_LANES = 128
MOE_TM = 256
VMEM_LIMIT = 56 * 1024 * 1024

_dot = functools.partial(jnp.dot, preferred_element_type=F32)


def _split2(x):
    hi = x.astype(BF16)
    lo = (x - hi.astype(F32)).astype(BF16)
    return hi, lo


def _rms(x, g):
    return x * lax.rsqrt(jnp.mean(x * x, axis=-1, keepdims=True) + EPS) * g


def _params(sem):
    return pltpu.CompilerParams(dimension_semantics=sem, vmem_limit_bytes=VMEM_LIMIT)


def _in_proj_kernel(x_ref, g_ref, w_ref, wgk_hi_ref, wgk_lo_ref, bgk_ref,
                    q_ref, k_ref, v_ref, r_ref, gk_ref, u_ref):
    xb = _rms(x_ref[...], g_ref[...]).astype(BF16)
    kw = GLA_KEY_WIDTH
    q_ref[...] = _dot(xb, w_ref[:, 0:kw]) * (GLA_DK ** -0.5)
    k_ref[...] = _dot(xb, w_ref[:, kw:2 * kw])
    v_ref[...] = _dot(xb, w_ref[:, 2 * kw:2 * kw + GLA_WIDTH])
    r_ref[...] = _dot(xb, w_ref[:, 2 * kw + GLA_WIDTH:2 * kw + 2 * GLA_WIDTH])
    c0 = 2 * kw + 2 * GLA_WIDTH
    u_ref[...] = _dot(xb, w_ref[:, c0:c0 + S5_WIDTH])
    a_low = _dot(xb, w_ref[:, c0 + S5_WIDTH:c0 + S5_WIDTH + 128])
    a_hi, a_lo = _split2(a_low)
    z = (_dot(a_hi, wgk_hi_ref[...]) + _dot(a_hi, wgk_lo_ref[...]) + _dot(a_lo, wgk_hi_ref[...])
         + bgk_ref[...])
    gk_ref[...] = (jnp.minimum(z, 0.0) - jnp.log1p(jnp.exp(-jnp.abs(z)))) * (1.0 / GLA_GATE_NORM)


def _in_proj(x, g_mix, w_cat, wgk_hi, wgk_lo, b_gk, tm):
    n = x.shape[0]
    wcols = w_cat.shape[1]
    row = lambda w: pl.BlockSpec((tm, w), lambda i: (i, 0))
    full = lambda a: pl.BlockSpec(a.shape, lambda i: (0,) * a.ndim)
    widths = [GLA_KEY_WIDTH, GLA_KEY_WIDTH, GLA_WIDTH, GLA_WIDTH, GLA_KEY_WIDTH, S5_WIDTH]
    return pl.pallas_call(
        _in_proj_kernel,
        grid=(n // tm,),
        in_specs=[row(D_MODEL), full(g_mix),
                  pl.BlockSpec((D_MODEL, wcols), lambda i: (0, 0), pipeline_mode=pl.Buffered(1)),
                  full(wgk_hi), full(wgk_lo), full(b_gk)],
        out_specs=[row(w) for w in widths],
        out_shape=[jax.ShapeDtypeStruct((n, w), F32) for w in widths],
        compiler_params=_params(("parallel",)),
        name="in_proj",
    )(x, g_mix, w_cat, wgk_hi, wgk_lo, b_gk)


def _gla_kernel(q_ref, k_ref, v_ref, g_ref, s0_ref, o_ref, sout_ref, s_scr, *, chunk, n_inner):
    C = chunk
    sub = min(C, GLA_SUB)
    nsub = C // sub
    j = pl.program_id(1)

    @pl.when(j == 0)
    def _init():
        s_scr[...] = s0_ref[0]

    ri = lax.broadcasted_iota(jnp.int32, (C, C), 0)
    ci = lax.broadcasted_iota(jnp.int32, (C, C), 1)
    tril = jnp.where(ri >= ci, 1.0, 0.0).astype(BF16)
    sub_rows = lax.broadcasted_iota(jnp.int32, (sub, GLA_DK), 0)
    out_rows = lax.broadcasted_iota(jnp.int32, (sub, GLA_DV), 0)

    def chunk_body(c, carry):
        r0 = pl.multiple_of(c * C, C)
        g = g_ref[pl.ds(r0, C), :]
        g1 = g.astype(BF16)
        rem = g - g1.astype(F32)
        g2 = rem.astype(BF16)
        g3 = (rem - g2.astype(F32)).astype(BF16)
        b_all = _dot(tril, g1) + _dot(tril, g2) + _dot(tril, g3)
        for h in range(GLA_HEADS):
            ks = slice(h * GLA_DK, (h + 1) * GLA_DK)
            vs = slice(h * GLA_DV, (h + 1) * GLA_DV)
            q = q_ref[pl.ds(r0, C), ks]
            k = k_ref[pl.ds(r0, C), ks]
            v = v_ref[pl.ds(r0, C), vs]
            b = b_all[:, ks]
            s_prev = s_scr[h]
            vb = v.astype(BF16)
            o_inter = _dot((q * jnp.exp(b)).astype(BF16), s_prev.astype(BF16))
            blocks = []
            for s in range(nsub):
                lo = s * sub
                bs, qs, ksub, vsub = b[lo:lo + sub], q[lo:lo + sub], k[lo:lo + sub], v[lo:lo + sub]
                acc = o_inter[lo:lo + sub]
                if s > 0:
                    anchor = b[lo - 1:lo]
                    qd = (qs * jnp.exp(bs - anchor)).astype(BF16)
                    kd = (k[:lo] * jnp.exp(anchor - b[:lo])).astype(BF16)
                    sc = lax.dot_general(qd, kd, (((1,), (1,)), ((), ())), preferred_element_type=F32)
                    acc = acc + _dot(sc.astype(BF16), vb[:lo])
                for i in range(sub):
                    diff = bs[i:i + 1] - bs
                    causal = sub_rows <= i
                    dec = jnp.where(causal, jnp.exp(jnp.where(causal, diff, 0.0)), 0.0)
                    t = (qs[i:i + 1] * ksub) * dec
                    sc_i = jnp.sum(t, axis=-1, keepdims=True)
                    o_i = jnp.sum(sc_i * vsub, axis=0, keepdims=True)
                    acc = jnp.where(out_rows == i, acc + o_i, acc)
                blocks.append(acc)
            o_ref[pl.ds(r0, C), vs] = jnp.concatenate(blocks, axis=0) if nsub > 1 else blocks[0]
            b_last = b[C - 1:C]
            kdec = (k * jnp.exp(b_last - b)).astype(BF16)
            upd = lax.dot_general(kdec, vb, (((0,), (0,)), ((), ())), preferred_element_type=F32)
            dcol = jnp.broadcast_to(jnp.exp(b_last), (GLA_DK, GLA_DK)).T
            s_scr[h] = s_prev * jnp.concatenate([dcol, dcol], axis=1) + upd
        return carry

    lax.fori_loop(0, n_inner, chunk_body, 0)

    @pl.when(j == pl.num_programs(1) - 1)
    def _fin():
        sout_ref[0] = s_scr[...]


def _gla(q, k, v, g, s0, *, seq_len, chunk):
    nseq = s0.shape[0]
    rb = min(seq_len, 4 * chunk)
    nblk = seq_len // rb
    rows = lambda w: pl.BlockSpec((rb, w), lambda s, j: (s * nblk + j, 0))
    st = pl.BlockSpec((1, GLA_HEADS, GLA_DK, GLA_DV), lambda s, j: (s, 0, 0, 0))
    n = q.shape[0]
    return pl.pallas_call(
        functools.partial(_gla_kernel, chunk=chunk, n_inner=rb // chunk),
        grid=(nseq, nblk),
        in_specs=[rows(GLA_KEY_WIDTH), rows(GLA_KEY_WIDTH), rows(GLA_WIDTH), rows(GLA_KEY_WIDTH), st],
        out_specs=[rows(GLA_WIDTH), st],
        out_shape=[jax.ShapeDtypeStruct((n, GLA_WIDTH), F32),
                   jax.ShapeDtypeStruct((nseq, GLA_HEADS, GLA_DK, GLA_DV), F32)],
        scratch_shapes=[pltpu.VMEM((GLA_HEADS, GLA_DK, GLA_DV), F32)],
        compiler_params=_params(("parallel", "arbitrary")),
        name="gla",
    )(q, k, v, g, s0)


def _cmul(ar, ai, br, bi):
    return ar * br - ai * bi, ar * bi + ai * br


def _cpow(ar, ai, n):
    res = None
    while n:
        if n & 1:
            res = (ar, ai) if res is None else _cmul(res[0], res[1], ar, ai)
        n >>= 1
        if n:
            ar, ai = _cmul(ar, ai, ar, ai)
    return res


def _s5_segment_len(seq_len):
    s = -(-seq_len // 8)
    s = -(-s // 4) * 4
    return s if (s // 4) % 2 == 1 else s + 4


def _st_store(st_scr, rows, val):
    for t in range(S5_LT):
        st_scr[t, rows, :] = val[:, t * 128:(t + 1) * 128]


def _st_load(st_scr, rows):
    return jnp.concatenate([st_scr[t, rows, :] for t in range(S5_LT)], axis=1)


def _s5_long_kernel(u_ref, bb_ref, cc_ref, dsk_ref, are_ref, aim_ref, h0re_ref, h0im_ref,
                    y_ref, hre_ref, him_ref, st_scr, *, seq_len, seg):
    T, S, SL = seq_len, seg, S5_SL
    rc = min(T, 512)
    for c in range(T // rc):
        sl = slice(c * rc, (c + 1) * rc)
        _st_store(st_scr, sl, _dot(u_ref[sl, :].astype(BF16), bb_ref[0]))
    if 8 * S > T:
        _st_store(st_scr, slice(T, 8 * S), jnp.zeros((8 * S - T, 2 * SL), F32))
    a_re, a_im = are_ref[0], aim_ref[0]
    ar = jnp.broadcast_to(a_re, (8, SL))
    ai = jnp.broadcast_to(a_im, (8, SL))

    def local_scan(i, h):
        strand = pl.ds(i, 8, stride=S)
        x = _st_load(st_scr, strand)
        mr, mi = _cmul(ar, ai, h[0], h[1])
        nr, ni = mr + x[:, :SL], mi + x[:, SL:]
        _st_store(st_scr, strand, jnp.concatenate([nr, ni], axis=1))
        return nr, ni

    zero = jnp.zeros((8, SL), F32)
    fr, fi = lax.fori_loop(0, S, local_scan, (zero, zero))

    as_re, as_im = _cpow(a_re, a_im, S)
    rows = lax.broadcasted_iota(jnp.int32, (8, SL), 0)
    cr, ci = h0re_ref[0, 0], h0im_ref[0, 0]
    car_r, car_i = zero, zero
    for r in range(8):
        car_r = jnp.where(rows == r, cr, car_r)
        car_i = jnp.where(rows == r, ci, car_i)
        if r < 7:
            mr, mi = _cmul(as_re, as_im, cr, ci)
            cr, ci = mr + fr[r:r + 1], mi + fi[r:r + 1]

    def add_carry(i, p):
        strand = pl.ds(i, 8, stride=S)
        x = _st_load(st_scr, strand)
        mr, mi = _cmul(p[0], p[1], car_r, car_i)
        _st_store(st_scr, strand, jnp.concatenate([x[:, :SL] + mr, x[:, SL:] + mi], axis=1))
        return _cmul(p[0], p[1], ar, ai)

    lax.fori_loop(0, S, add_carry, (ar, ai))

    last = _st_load(st_scr, slice(T - 1, T))
    hre_ref[0, 0] = last[:, :SL]
    him_ref[0, 0] = last[:, SL:]
    for c in range(T // rc):
        sl = slice(c * rc, (c + 1) * rc)
        y_ref[sl, :] = _dot(_st_load(st_scr, sl).astype(BF16), cc_ref[0]) + dsk_ref[0] * u_ref[sl, :]


def _s5_long(u, mats, h0_re, h0_im, *, seq_len):
    bb, cc, dsk, a_re, a_im = mats
    nseq = h0_re.shape[0]
    seg = _s5_segment_len(seq_len)
    gb3 = lambda shape: pl.BlockSpec((1,) + shape, lambda s, g: (g, 0, 0))
    st = pl.BlockSpec((1, 1, 1, S5_SL), lambda s, g: (s, g, 0, 0))
    urow = pl.BlockSpec((seq_len, S5_UL), lambda s, g: (s, g))
    st_shape = jax.ShapeDtypeStruct((nseq, S5_NGB, 1, S5_SL), F32)
    return pl.pallas_call(
        functools.partial(_s5_long_kernel, seq_len=seq_len, seg=seg),
        grid=(nseq, S5_NGB),
        in_specs=[urow, gb3((S5_UL, 2 * S5_SL)), gb3((2 * S5_SL, S5_UL)), gb3((1, S5_UL)),
                  gb3((1, S5_SL)), gb3((1, S5_SL)), st, st],
        out_specs=[urow, st, st],
        out_shape=[jax.ShapeDtypeStruct(u.shape, F32), st_shape, st_shape],
        scratch_shapes=[pltpu.VMEM((S5_LT, 8 * seg, 128), F32)],
        compiler_params=_params(("parallel", "parallel")),
        name="s5_long",
    )(u, bb, cc, dsk, a_re, a_im, h0_re, h0_im)


def _s5_short_kernel(u_ref, bb_ref, cc_ref, dsk_ref, are_ref, aim_ref, h0re_ref, h0im_ref,
                     y_ref, hre_ref, him_ref, st_scr, *, seq_len, nseq):
    SL = S5_SL
    _st_store(st_scr, slice(None), _dot(u_ref[...].astype(BF16), bb_ref[0]))
    ar = jnp.broadcast_to(are_ref[0], (nseq, SL))
    ai = jnp.broadcast_to(aim_ref[0], (nseq, SL))
    hr, hi = h0re_ref[...], h0im_ref[...]
    for t in range(seq_len):
        step = pl.ds(t, nseq, stride=seq_len)
        x = _st_load(st_scr, step)
        mr, mi = _cmul(ar, ai, hr, hi)
        hr, hi = mr + x[:, :SL], mi + x[:, SL:]
        _st_store(st_scr, step, jnp.concatenate([hr, hi], axis=1))
    hre_ref[...] = hr
    him_ref[...] = hi
    y_ref[...] = _dot(_st_load(st_scr, slice(None)).astype(BF16), cc_ref[0]) + dsk_ref[0] * u_ref[...]


def _s5_short(u, mats, h0_re, h0_im, *, seq_len):
    bb, cc, dsk, a_re, a_im = mats
    nseq = h0_re.shape[0]
    n = nseq * seq_len
    gb3 = lambda shape: pl.BlockSpec((1,) + shape, lambda g: (g, 0, 0))
    st = pl.BlockSpec((nseq, S5_SL), lambda g: (0, g))
    urow = pl.BlockSpec((n, S5_UL), lambda g: (0, g))
    st_shape = jax.ShapeDtypeStruct(h0_re.shape, F32)
    return pl.pallas_call(
        functools.partial(_s5_short_kernel, seq_len=seq_len, nseq=nseq),
        grid=(S5_NGB,),
        in_specs=[urow, gb3((S5_UL, 2 * S5_SL)), gb3((2 * S5_SL, S5_UL)), gb3((1, S5_UL)),
                  gb3((1, S5_SL)), gb3((1, S5_SL)), st, st],
        out_specs=[urow, st, st],
        out_shape=[jax.ShapeDtypeStruct(u.shape, F32), st_shape, st_shape],
        scratch_shapes=[pltpu.VMEM((S5_LT, n, 128), F32)],
        compiler_params=_params(("parallel",)),
        name="s5_short",
    )(u, bb, cc, dsk, a_re, a_im, h0_re, h0_im)


def _s5_matrices(lam_re, lam_im, log_dt, b_re, b_im, c_re, c_im, d_skip):
    dt = jnp.exp(log_dt)[:, None]
    mag = jnp.exp(lam_re * dt)
    ab_re, ab_im = mag * jnp.cos(lam_im * dt), mag * jnp.sin(lam_im * dt)
    den = lam_re * lam_re + lam_im * lam_im
    f_re = ((ab_re - 1.0) * lam_re + ab_im * lam_im) / den
    f_im = (ab_im * lam_re - (ab_re - 1.0) * lam_im) / den
    bb_re = f_re[..., None] * b_re - f_im[..., None] * b_im
    bb_im = f_re[..., None] * b_im + f_im[..., None] * b_re
    eye = jnp.eye(S5_GB, dtype=F32)

    def in_mat(m):
        m = m.reshape(S5_NGB, S5_GB, S5_STATE, S5_GROUP)
        return jnp.einsum('bgpc,gh->bgchp', m, eye).reshape(S5_NGB, S5_UL, S5_SL)

    def out_mat(m):
        m = m.reshape(S5_NGB, S5_GB, S5_GROUP, S5_STATE)
        return jnp.einsum('bgcp,gh->bgphc', m, eye).reshape(S5_NGB, S5_SL, S5_UL)

    bb = jnp.concatenate([in_mat(bb_re), in_mat(bb_im)], axis=2).astype(BF16)
    cc = jnp.concatenate([out_mat(c_re), out_mat(-c_im)], axis=1).astype(BF16)
    dsk = d_skip.reshape(S5_NGB, 1, S5_UL)
    return bb, cc, dsk, ab_re.reshape(S5_NGB, 1, S5_SL), ab_im.reshape(S5_NGB, 1, S5_SL)


def _post_mix_kernel(x_ref, o_ref, r_ref, y5_ref, ggla_ref, wglu_ref, bglu_ref, wout_ref, gffn_ref,
                     wr_hi_ref, wr_lo_ref, br_ref, h1_ref, hn_ref, lg_ref):
    o = o_ref[...]
    parts = []
    for h in range(GLA_HEADS):
        oh = o[:, h * GLA_DV:(h + 1) * GLA_DV]
        parts.append(oh * lax.rsqrt(jnp.mean(oh * oh, axis=-1, keepdims=True) + EPS))
    r = r_ref[...]
    o_gla = (jnp.concatenate(parts, axis=1) * ggla_ref[...]) * (r * jax.nn.sigmoid(r))
    y5 = y5_ref[...]
    z = y5 * (0.5 * (1.0 + jnp.tanh(math.sqrt(2.0 / math.pi) * (y5 + 0.044715 * (y5 * y5 * y5)))))
    o_s5 = z * jax.nn.sigmoid(_dot(z.astype(BF16), wglu_ref[...]) + bglu_ref[...])
    att = (_dot(o_gla.astype(BF16), wout_ref[0:GLA_WIDTH, :])
           + _dot(o_s5.astype(BF16), wout_ref[GLA_WIDTH:GLA_WIDTH + S5_WIDTH, :]))
    h1 = x_ref[...] + att
    h1_ref[...] = h1
    hn = _rms(h1, gffn_ref[...])
    hn_ref[...] = hn.astype(BF16)
    hn_hi, hn_lo = _split2(hn)
    lg_ref[...] = (_dot(hn_hi, wr_hi_ref[...]) + _dot(hn_hi, wr_lo_ref[...]) + _dot(hn_lo, wr_hi_ref[...])
                   + br_ref[...])


def _post_mix(x, o, r, y5, g_gla, w_glu, b_glu, w_out, g_ffn, wr_hi, wr_lo, b_r, tm):
    n = x.shape[0]
    row = lambda w: pl.BlockSpec((tm, w), lambda i: (i, 0))
    full = lambda a: pl.BlockSpec(a.shape, lambda i: (0,) * a.ndim)
    return pl.pallas_call(
        _post_mix_kernel,
        grid=(n // tm,),
        in_specs=[row(D_MODEL), row(GLA_WIDTH), row(GLA_WIDTH), row(S5_WIDTH), full(g_gla), full(w_glu),
                  full(b_glu), full(w_out), full(g_ffn), full(wr_hi), full(wr_lo), full(b_r)],
        out_specs=[row(D_MODEL), row(D_MODEL), row(ROUTER_LANES)],
        out_shape=[jax.ShapeDtypeStruct((n, D_MODEL), F32), jax.ShapeDtypeStruct((n, D_MODEL), BF16),
                   jax.ShapeDtypeStruct((n, ROUTER_LANES), F32)],
        compiler_params=_params(("parallel",)),
        name="post_mix",
    )(x, o, r, y5, g_gla, w_glu, b_glu, w_out, g_ffn, wr_hi, wr_lo, b_r)


def _moe_kernel(bexp_ref, nused_ref, x_ref, wg_ref, wu_ref, wd_ref, o_ref):
    b = pl.program_id(0)

    @pl.when(b < nused_ref[0])
    def _run():
        x = x_ref[...]
        gate = _dot(x, wg_ref[0].astype(BF16))
        up = _dot(x, wu_ref[0].astype(BF16))
        hid = (gate * jax.nn.sigmoid(gate)) * up
        o_ref[...] = _dot(hid.astype(BF16), wd_ref[0].astype(BF16))

    @pl.when(b >= nused_ref[0])
    def _skip():
        o_ref[...] = jnp.zeros(o_ref.shape, o_ref.dtype)


def _moe(block_exp, n_used, xs, w_gate, w_up, w_down):
    nblk = xs.shape[0] // MOE_TM
    xrow = pl.BlockSpec((MOE_TM, D_MODEL), lambda b, be, nu: (jnp.minimum(b, nu[0] - 1), 0))
    wspec = lambda shape: pl.BlockSpec((1,) + shape, lambda b, be, nu: (be[b], 0, 0))
    return pl.pallas_call(
        _moe_kernel,
        grid_spec=pltpu.PrefetchScalarGridSpec(
            num_scalar_prefetch=2,
            grid=(nblk,),
            in_specs=[xrow, wspec((D_MODEL, EXPERT_HIDDEN)), wspec((D_MODEL, EXPERT_HIDDEN)),
                      wspec((EXPERT_HIDDEN, D_MODEL))],
            out_specs=pl.BlockSpec((MOE_TM, D_MODEL), lambda b, be, nu: (b, 0)),
        ),
        out_shape=jax.ShapeDtypeStruct((nblk * MOE_TM, D_MODEL), F32),
        compiler_params=_params(("arbitrary",)),
        name="moe",
    )(block_exp, n_used, xs, w_gate, w_up, w_down)


def _final_kernel(h1_ref, ya_ref, yb_ref, wa_ref, wb_ref, g_ref, y_ref):
    moe = ya_ref[...] * wa_ref[...] + yb_ref[...] * wb_ref[...]
    y_ref[...] = _rms(h1_ref[...] + moe, g_ref[...])


def _final(h1, ya, yb, wa, wb, g_final, tm):
    n = h1.shape[0]
    row = lambda w: pl.BlockSpec((tm, w), lambda i: (i, 0))
    return pl.pallas_call(
        _final_kernel,
        grid=(n // tm,),
        in_specs=[row(D_MODEL), row(D_MODEL), row(D_MODEL), row(1), row(1),
                  pl.BlockSpec((1, D_MODEL), lambda i: (0, 0))],
        out_specs=row(D_MODEL),
        out_shape=jax.ShapeDtypeStruct((n, D_MODEL), F32),
        compiler_params=_params(("parallel",)),
        name="final",
    )(h1, ya, yb, wa, wb, g_final)


def _route(logits):
    lg = logits[:, :N_GROUPS]
    le = logits[:, N_GROUPS:N_GROUPS + N_EXPERTS].reshape(-1, N_GROUPS, EXPERTS_PER_GROUP)
    pg = jax.nn.softmax(lg, axis=-1)
    gsel = jnp.argmax(lg, axis=-1)
    pg_sel = jnp.take_along_axis(pg, gsel[:, None], axis=-1)[:, 0]
    le_sel = jnp.take_along_axis(le, gsel[:, None, None], axis=1)[:, 0]
    vals, idx = lax.top_k(le_sel, TOP_K)
    pe = jax.nn.softmax(vals, axis=-1)
    eid = (gsel[:, None] * EXPERTS_PER_GROUP + idx).astype(jnp.int32)
    return eid, pg_sel[:, None] * pe


def _dispatch(eid):
    t = eid.shape[0]
    a = t * TOP_K
    nblk = a // MOE_TM + N_EXPERTS
    flat = eid.reshape(-1)
    onehot = (flat[:, None] == jnp.arange(N_EXPERTS, dtype=jnp.int32)[None, :]).astype(jnp.int32)
    rank = jnp.take_along_axis(jnp.cumsum(onehot, axis=0) - onehot, flat[:, None], axis=1)[:, 0]
    counts = jnp.sum(onehot, axis=0)
    padded = (counts + MOE_TM - 1) // MOE_TM * MOE_TM
    pends = jnp.cumsum(padded)
    dest = (pends - padded)[flat] + rank
    tok = jnp.arange(a, dtype=jnp.int32) // TOP_K
    slot_tok = jnp.zeros((nblk * MOE_TM,), jnp.int32).at[dest].set(tok)
    n_used = (pends[-1] // MOE_TM).astype(jnp.int32)
    blk = jnp.minimum(jnp.arange(nblk, dtype=jnp.int32), n_used - 1)
    block_exp = jnp.minimum(jnp.searchsorted(pends, blk * MOE_TM, side='right'), N_EXPERTS - 1).astype(jnp.int32)
    return dest.reshape(t, TOP_K), slot_tok, block_exp, n_used.reshape(1)


def kernel(x_prompt, x_sample, state_gla, state_s5_re, state_s5_im, meta, g_mix, w_in, w_gk2, b_gk, g_gla,
           lam_re, lam_im, log_dt, s5_b_re, s5_b_im, s5_c_re, s5_c_im, d_skip, w_glu, b_glu, w_out, g_ffn,
           w_rg, b_rg, w_re, b_re, w_gate, w_up, w_down, g_final):
    bp, tp, _ = x_prompt.shape
    bs, ts, _ = x_sample.shape
    l = 0
    kw = GLA_KEY_WIDTH
    a0 = 2 * kw + 2 * GLA_WIDTH

    w = w_in[l]
    w_cat = jnp.concatenate(
        [w[:, :a0], w[:, a0 + GLA_RANK:], jnp.pad(w[:, a0:a0 + GLA_RANK], ((0, 0), (0, 128 - GLA_RANK)))],
        axis=1).astype(BF16)
    wgk_hi, wgk_lo = _split2(jnp.pad(w_gk2[l], ((0, 128 - GLA_RANK), (0, 0))))
    g_mix2, b_gk2 = g_mix[l][None], b_gk[l][None]
    mats = _s5_matrices(lam_re[l], lam_im[l], log_dt[l], s5_b_re[l], s5_b_im[l], s5_c_re[l], s5_c_im[l],
                        d_skip[l])
    w_router = jnp.concatenate([w_rg[l], jnp.moveaxis(w_re[l], 0, 1).reshape(D_MODEL, N_EXPERTS)], axis=1)
    w_router = jnp.pad(w_router, ((0, 0), (0, ROUTER_LANES - N_GROUPS - N_EXPERTS)))
    wr_hi, wr_lo = _split2(w_router)
    b_router = jnp.pad(jnp.concatenate([b_rg[l], b_re[l].reshape(-1)]),
                       (0, ROUTER_LANES - N_GROUPS - N_EXPERTS))[None]
    w_glu_b, w_out_b = w_glu[l].astype(BF16), w_out[l].astype(BF16)
    g_gla2, b_glu2, g_ffn2 = g_gla[l].reshape(1, GLA_WIDTH), b_glu[l][None], g_ffn[l][None]

    proj = functools.partial(_in_proj, g_mix=g_mix2, w_cat=w_cat, wgk_hi=wgk_hi, wgk_lo=wgk_lo, b_gk=b_gk2)

    qm, km, vm, _, gm, um = proj(meta, tm=N_META)
    zero_s = jnp.zeros((1, GLA_HEADS, GLA_DK, GLA_DV), F32)
    _, s_meta = _gla(qm, km, vm, gm, zero_s, seq_len=N_META, chunk=N_META)
    zero_h = jnp.zeros((1, S5_NGB, 1, S5_SL), F32)
    _, hm_re, hm_im = _s5_long(um, mats, zero_h, zero_h, seq_len=N_META)

    xp = x_prompt.reshape(bp * tp, D_MODEL)
    qp, kp, vp, rp, gp, up = proj(xp, tm=512)
    op, gla_p = _gla(qp, kp, vp, gp, jnp.broadcast_to(s_meta, (bp,) + s_meta.shape[1:]),
                     seq_len=tp, chunk=GLA_CHUNK)
    y5p, hp_re, hp_im = _s5_long(up, mats, jnp.broadcast_to(hm_re, (bp,) + hm_re.shape[1:]),
                                 jnp.broadcast_to(hm_im, (bp,) + hm_im.shape[1:]), seq_len=tp)

    xs = x_sample.reshape(bs * ts, D_MODEL)
    qs, ks, vs, rs, gs, us = proj(xs, tm=512)
    chunk_s = GLA_CHUNK if ts % GLA_CHUNK == 0 else ts
    os_, gla_s = _gla(qs, ks, vs, gs, state_gla[l], seq_len=ts, chunk=chunk_s)
    y5s, hs_re, hs_im = _s5_short(us, mats, state_s5_re[l].reshape(bs, -1), state_s5_im[l].reshape(bs, -1),
                                  seq_len=ts)

    post = functools.partial(_post_mix, g_gla=g_gla2, w_glu=w_glu_b, b_glu=b_glu2, w_out=w_out_b, g_ffn=g_ffn2,
                             wr_hi=wr_hi, wr_lo=wr_lo, b_r=b_router, tm=256)
    h1p, hnp, lgp = post(xp, op, rp, y5p)
    h1s, hns, lgs = post(xs, os_, rs, y5s)

    np_rows = bp * tp
    eid, wts = _route(jnp.concatenate([lgp, lgs], axis=0))
    dest, slot_tok, block_exp, n_used = _dispatch(eid)
    hn_all = jnp.concatenate([hnp, hns], axis=0)
    out_rows = _moe(block_exp, n_used, hn_all[slot_tok], w_gate[l], w_up[l], w_down[l])
    ya, yb = out_rows[dest[:, 0]], out_rows[dest[:, 1]]
    wa, wb = wts[:, 0:1], wts[:, 1:2]

    y_prompt = _final(h1p, ya[:np_rows], yb[:np_rows], wa[:np_rows], wb[:np_rows], g_final[None], tm=512)
    y_sample = _final(h1s, ya[np_rows:], yb[np_rows:], wa[np_rows:], wb[np_rows:], g_final[None], tm=512)

    return (y_prompt.reshape(bp, tp, D_MODEL), y_sample.reshape(bs, ts, D_MODEL),
            gla_p[None], hp_re.reshape(1, bp, S5_GROUPS, S5_STATE), hp_im.reshape(1, bp, S5_GROUPS, S5_STATE),
            gla_s[None], hs_re.reshape(1, bs, S5_GROUPS, S5_STATE), hs_im.reshape(1, bs, S5_GROUPS, S5_STATE))
```

```python
import functools
import math

import jax
import jax.numpy as jnp
from jax import lax
from jax.experimental import pallas as pl
from jax.experimental.pallas import tpu as pltpu

F32 = jnp.float32
BF16 = jnp.bfloat16

D_MODEL = 2048
N_META = 16
GLA_HEADS = 4
GLA_DK = 128
GLA_DV = 256
GLA_KEY_WIDTH = GLA_HEADS * GLA_DK
GLA_WIDTH = GLA_HEADS * GLA_DV
GLA_RANK = 16
GLA_GATE_NORM = 16.0
GLA_CHUNK = 64
GLA_SUB = 16
S5_WIDTH = 1024
S5_GROUP = 16
S5_GROUPS = 64
S5_STATE = 64
S5_GB = 8
S5_NGB = S5_GROUPS // S5_GB
S5_UL = S5_GB * S5_GROUP
S5_SL = S5_GB * S5_STATE
S5_LT = 2 * S5_SL // 128
N_GROUPS = 4
EXPERTS_PER_GROUP = 8
N_EXPERTS = N_GROUPS * EXPERTS_PER_GROUP
EXPERT_HIDDEN = 512
TOP_K = 2
EPS = 1e-6
ROUTER_LANES = 128
MOE_TM = 256
VMEM_LIMIT = 56 * 1024 * 1024

_dot = functools.partial(jnp.dot, preferred_element_type=F32)


def _split2(x):
    hi = x.astype(BF16)
    lo = (x - hi.astype(F32)).astype(BF16)
    return hi, lo


def _rms(x, g):
    return x * lax.rsqrt(jnp.mean(x * x, axis=-1, keepdims=True) + EPS) * g


def _params(sem):
    return pltpu.CompilerParams(dimension_semantics=sem, vmem_limit_bytes=VMEM_LIMIT)


def _in_proj_kernel(x_ref, g_ref, w_ref, wgk_hi_ref, wgk_lo_ref, bgk_ref,
                    q_ref, k_ref, v_ref, r_ref, gk_ref, u_ref):
    xb = _rms(x_ref[...], g_ref[...]).astype(BF16)
    kw = GLA_KEY_WIDTH
    q_ref[...] = _dot(xb, w_ref[:, 0:kw]) * (GLA_DK ** -0.5)
    k_ref[...] = _dot(xb, w_ref[:, kw:2 * kw])
    v_ref[...] = _dot(xb, w_ref[:, 2 * kw:2 * kw + GLA_WIDTH])
    r_ref[...] = _dot(xb, w_ref[:, 2 * kw + GLA_WIDTH:2 * kw + 2 * GLA_WIDTH])
    c0 = 2 * kw + 2 * GLA_WIDTH
    u_ref[...] = _dot(xb, w_ref[:, c0:c0 + S5_WIDTH])
    a_low = _dot(xb, w_ref[:, c0 + S5_WIDTH:c0 + S5_WIDTH + 128])
    a_hi, a_lo = _split2(a_low)
    z = (_dot(a_hi, wgk_hi_ref[...]) + _dot(a_hi, wgk_lo_ref[...]) + _dot(a_lo, wgk_hi_ref[...])
         + bgk_ref[...])
    gk_ref[...] = (jnp.minimum(z, 0.0) - jnp.log1p(jnp.exp(-jnp.abs(z)))) * (1.0 / GLA_GATE_NORM)


def _in_proj(x, g_mix, w_cat, wgk_hi, wgk_lo, b_gk, tm):
    n = x.shape[0]
    wcols = w_cat.shape[1]
    row = lambda w: pl.BlockSpec((tm, w), lambda i: (i, 0))
    full = lambda a: pl.BlockSpec(a.shape, lambda i: (0,) * a.ndim)
    widths = [GLA_KEY_WIDTH, GLA_KEY_WIDTH, GLA_WIDTH, GLA_WIDTH, GLA_KEY_WIDTH, S5_WIDTH]
    return pl.pallas_call(
        _in_proj_kernel,
        grid=(n // tm,),
        in_specs=[row(D_MODEL), full(g_mix),
                  pl.BlockSpec((D_MODEL, wcols), lambda i: (0, 0), pipeline_mode=pl.Buffered(1)),
                  full(wgk_hi), full(wgk_lo), full(b_gk)],
        out_specs=[row(w) for w in widths],
        out_shape=[jax.ShapeDtypeStruct((n, w), F32) for w in widths],
        compiler_params=_params(("parallel",)),
        name="in_proj",
    )(x, g_mix, w_cat, wgk_hi, wgk_lo, b_gk)


def _gla_kernel(q_ref, k_ref, v_ref, g_ref, s0_ref, o_ref, sout_ref, s_scr, *, chunk, n_inner):
    C = chunk
    sub = min(C, GLA_SUB)
    nsub = C // sub
    j = pl.program_id(1)

    @pl.when(j == 0)
    def _init():
        s_scr[...] = s0_ref[0]

    ri = lax.broadcasted_iota(jnp.int32, (C, C), 0)
    ci = lax.broadcasted_iota(jnp.int32, (C, C), 1)
    tril = jnp.where(ri >= ci, 1.0, 0.0).astype(BF16)
    sub_rows = lax.broadcasted_iota(jnp.int32, (sub, GLA_DK), 0)
    out_rows = lax.broadcasted_iota(jnp.int32, (sub, GLA_DV), 0)

    def chunk_body(c, carry):
        r0 = pl.multiple_of(c * C, C)
        g = g_ref[pl.ds(r0, C), :]
        g1 = g.astype(BF16)
        rem = g - g1.astype(F32)
        g2 = rem.astype(BF16)
        g3 = (rem - g2.astype(F32)).astype(BF16)
        b_all = _dot(tril, g1) + _dot(tril, g2) + _dot(tril, g3)
        for h in range(GLA_HEADS):
            ks = slice(h * GLA_DK, (h + 1) * GLA_DK)
            vs = slice(h * GLA_DV, (h + 1) * GLA_DV)
            q = q_ref[pl.ds(r0, C), ks]
            k = k_ref[pl.ds(r0, C), ks]
            v = v_ref[pl.ds(r0, C), vs]
            b = b_all[:, ks]
            s_prev = s_scr[h]
            vb = v.astype(BF16)
            o_inter = _dot((q * jnp.exp(b)).astype(BF16), s_prev.astype(BF16))
            blocks = []
            for s in range(nsub):
                lo = s * sub
                bs, qs, ksub, vsub = b[lo:lo + sub], q[lo:lo + sub], k[lo:lo + sub], v[lo:lo + sub]
                acc = o_inter[lo:lo + sub]
                if s > 0:
                    anchor = b[lo - 1:lo]
                    qd = (qs * jnp.exp(bs - anchor)).astype(BF16)
                    kd = (k[:lo] * jnp.exp(anchor - b[:lo])).astype(BF16)
                    sc = lax.dot_general(qd, kd, (((1,), (1,)), ((), ())), preferred_element_type=F32)
                    acc = acc + _dot(sc.astype(BF16), vb[:lo])
                for i in range(sub):
                    diff = bs[i:i + 1] - bs
                    causal = sub_rows <= i
                    dec = jnp.where(causal, jnp.exp(jnp.where(causal, diff, 0.0)), 0.0)
                    t = (qs[i:i + 1] * ksub) * dec
                    sc_i = jnp.sum(t, axis=-1, keepdims=True)
                    o_i = jnp.sum(sc_i * vsub, axis=0, keepdims=True)
                    acc = jnp.where(out_rows == i, acc + o_i, acc)
                blocks.append(acc)
            o_ref[pl.ds(r0, C), vs] = jnp.concatenate(blocks, axis=0) if nsub > 1 else blocks[0]
            b_last = b[C - 1:C]
            kdec = (k * jnp.exp(b_last - b)).astype(BF16)
            upd = lax.dot_general(kdec, vb, (((0,), (0,)), ((), ())), preferred_element_type=F32)
            dcol = jnp.broadcast_to(jnp.exp(b_last), (GLA_DK, GLA_DK)).T
            s_scr[h] = s_prev * jnp.concatenate([dcol, dcol], axis=1) + upd
        return carry

    lax.fori_loop(0, n_inner, chunk_body, 0)

    @pl.when(j == pl.num_programs(1) - 1)
    def _fin():
        sout_ref[0] = s_scr[...]


def _gla(q, k, v, g, s0, *, seq_len, chunk):
    nseq = s0.shape[0]
    rb = min(seq_len, 4 * chunk)
    nblk = seq_len // rb
    rows = lambda w: pl.BlockSpec((rb, w), lambda s, j: (s * nblk + j, 0))
    st = pl.BlockSpec((1, GLA_HEADS, GLA_DK, GLA_DV), lambda s, j: (s, 0, 0, 0))
    n = q.shape[0]
    return pl.pallas_call(
        functools.partial(_gla_kernel, chunk=chunk, n_inner=rb // chunk),
        grid=(nseq, nblk),
        in_specs=[rows(GLA_KEY_WIDTH), rows(GLA_KEY_WIDTH), rows(GLA_WIDTH), rows(GLA_KEY_WIDTH), st],
        out_specs=[rows(GLA_WIDTH), st],
        out_shape=[jax.ShapeDtypeStruct((n, GLA_WIDTH), F32),
                   jax.ShapeDtypeStruct((nseq, GLA_HEADS, GLA_DK, GLA_DV), F32)],
        scratch_shapes=[pltpu.VMEM((GLA_HEADS, GLA_DK, GLA_DV), F32)],
        compiler_params=_params(("parallel", "arbitrary")),
        name="gla",
    )(q, k, v, g, s0)


def _cmul(ar, ai, br, bi):
    return ar * br - ai * bi, ar * bi + ai * br


def _cpow(ar, ai, n):
    res = None
    while n:
        if n & 1:
            res = (ar, ai) if res is None else _cmul(res[0], res[1], ar, ai)
        n >>= 1
        if n:
            ar, ai = _cmul(ar, ai, ar, ai)
    return res


def _s5_segment_len(seq_len):
    s = -(-seq_len // 8)
    s = -(-s // 4) * 4
    return s if (s // 4) % 2 == 1 else s + 4


def _st_store(st_scr, rows, val):
    for t in range(S5_LT):
        st_scr[t, rows, :] = val[:, t * 128:(t + 1) * 128]


def _st_load(st_scr, rows):
    return jnp.concatenate([st_scr[t, rows, :] for t in range(S5_LT)], axis=1)


def _s5_long_kernel(u_ref, bb_ref, cc_ref, dsk_ref, are_ref, aim_ref, h0re_ref, h0im_ref,
                    y_ref, hre_ref, him_ref, st_scr, *, seq_len, seg):
    T, S, SL = seq_len, seg, S5_SL
    rc = min(T, 512)
    for c in range(T // rc):
        sl = slice(c * rc, (c + 1) * rc)
        _st_store(st_scr, sl, _dot(u_ref[sl, :].astype(BF16), bb_ref[0]))
    if 8 * S > T:
        _st_store(st_scr, slice(T, 8 * S), jnp.zeros((8 * S - T, 2 * SL), F32))
    a_re, a_im = are_ref[0], aim_ref[0]
    ar = jnp.broadcast_to(a_re, (8, SL))
    ai = jnp.broadcast_to(a_im, (8, SL))

    def local_scan(i, h):
        strand = pl.ds(i, 8, stride=S)
        x = _st_load(st_scr, strand)
        mr, mi = _cmul(ar, ai, h[0], h[1])
        nr, ni = mr + x[:, :SL], mi + x[:, SL:]
        _st_store(st_scr, strand, jnp.concatenate([nr, ni], axis=1))
        return nr, ni

    zero = jnp.zeros((8, SL), F32)
    fr, fi = lax.fori_loop(0, S, local_scan, (zero, zero))

    as_re, as_im = _cpow(a_re, a_im, S)
    rows = lax.broadcasted_iota(jnp.int32, (8, SL), 0)
    cr, ci = h0re_ref[0, 0], h0im_ref[0, 0]
    car_r, car_i = zero, zero
    for r in range(8):
        car_r = jnp.where(rows == r, cr, car_r)
        car_i = jnp.where(rows == r, ci, car_i)
        if r < 7:
            mr, mi = _cmul(as_re, as_im, cr, ci)
            cr, ci = mr + fr[r:r + 1], mi + fi[r:r + 1]

    def add_carry(i, p):
        strand = pl.ds(i, 8, stride=S)
        x = _st_load(st_scr, strand)
        mr, mi = _cmul(p[0], p[1], car_r, car_i)
        _st_store(st_scr, strand, jnp.concatenate([x[:, :SL] + mr, x[:, SL:] + mi], axis=1))
        return _cmul(p[0], p[1], ar, ai)

    lax.fori_loop(0, S, add_carry, (ar, ai))

    last = _st_load(st_scr, slice(T - 1, T))
    hre_ref[0, 0] = last[:, :SL]
    him_ref[0, 0] = last[:, SL:]
    for c in range(T // rc):
        sl = slice(c * rc, (c + 1) * rc)
        y_ref[sl, :] = _dot(_st_load(st_scr, sl).astype(BF16), cc_ref[0]) + dsk_ref[0] * u_ref[sl, :]


def _s5_long(u, mats, h0_re, h0_im, *, seq_len):
    bb, cc, dsk, a_re, a_im = mats
    nseq = h0_re.shape[0]
    seg = _s5_segment_len(seq_len)
    gb3 = lambda shape: pl.BlockSpec((1,) + shape, lambda s, g: (g, 0, 0))
    st = pl.BlockSpec((1, 1, 1, S5_SL), lambda s, g: (s, g, 0, 0))
    urow = pl.BlockSpec((seq_len, S5_UL), lambda s, g: (s, g))
    st_shape = jax.ShapeDtypeStruct((nseq, S5_NGB, 1, S5_SL), F32)
    return pl.pallas_call(
        functools.partial(_s5_long_kernel, seq_len=seq_len, seg=seg),
        grid=(nseq, S5_NGB),
        in_specs=[urow, gb3((S5_UL, 2 * S5_SL)), gb3((2 * S5_SL, S5_UL)), gb3((1, S5_UL)),
                  gb3((1, S5_SL)), gb3((1, S5_SL)), st, st],
        out_specs=[urow, st, st],
        out_shape=[jax.ShapeDtypeStruct(u.shape, F32), st_shape, st_shape],
        scratch_shapes=[pltpu.VMEM((S5_LT, 8 * seg, 128), F32)],
        compiler_params=_params(("parallel", "parallel")),
        name="s5_long",
    )(u, bb, cc, dsk, a_re, a_im, h0_re, h0_im)


def _s5_short_kernel(u_ref, bb_ref, cc_ref, dsk_ref, are_ref, aim_ref, h0re_ref, h0im_ref,
                     y_ref, hre_ref, him_ref, st_scr, *, seq_len, nseq):
    SL = S5_SL
    _st_store(st_scr, slice(None), _dot(u_ref[...].astype(BF16), bb_ref[0]))
    ar = jnp.broadcast_to(are_ref[0], (nseq, SL))
    ai = jnp.broadcast_to(aim_ref[0], (nseq, SL))
    hr, hi = h0re_ref[...], h0im_ref[...]
    for t in range(seq_len):
        step = pl.ds(t, nseq, stride=seq_len)
        x = _st_load(st_scr, step)
        mr, mi = _cmul(ar, ai, hr, hi)
        hr, hi = mr + x[:, :SL], mi + x[:, SL:]
        _st_store(st_scr, step, jnp.concatenate([hr, hi], axis=1))
    hre_ref[...] = hr
    him_ref[...] = hi
    y_ref[...] = _dot(_st_load(st_scr, slice(None)).astype(BF16), cc_ref[0]) + dsk_ref[0] * u_ref[...]


def _s5_short(u, mats, h0_re, h0_im, *, seq_len):
    bb, cc, dsk, a_re, a_im = mats
    nseq = h0_re.shape[0]
    n = nseq * seq_len
    gb3 = lambda shape: pl.BlockSpec((1,) + shape, lambda g: (g, 0, 0))
    st = pl.BlockSpec((nseq, S5_SL), lambda g: (0, g))
    urow = pl.BlockSpec((n, S5_UL), lambda g: (0, g))
    st_shape = jax.ShapeDtypeStruct(h0_re.shape, F32)
    return pl.pallas_call(
        functools.partial(_s5_short_kernel, seq_len=seq_len, nseq=nseq),
        grid=(S5_NGB,),
        in_specs=[urow, gb3((S5_UL, 2 * S5_SL)), gb3((2 * S5_SL, S5_UL)), gb3((1, S5_UL)),
                  gb3((1, S5_SL)), gb3((1, S5_SL)), st, st],
        out_specs=[urow, st, st],
        out_shape=[jax.ShapeDtypeStruct(u.shape, F32), st_shape, st_shape],
        scratch_shapes=[pltpu.VMEM((S5_LT, n, 128), F32)],
        compiler_params=_params(("parallel",)),
        name="s5_short",
    )(u, bb, cc, dsk, a_re, a_im, h0_re, h0_im)


def _s5_matrices(lam_re, lam_im, log_dt, b_re, b_im, c_re, c_im, d_skip):
    dt = jnp.exp(log_dt)[:, None]
    mag = jnp.exp(lam_re * dt)
    ab_re, ab_im = mag * jnp.cos(lam_im * dt), mag * jnp.sin(lam_im * dt)
    den = lam_re * lam_re + lam_im * lam_im
    f_re = ((ab_re - 1.0) * lam_re + ab_im * lam_im) / den
    f_im = (ab_im * lam_re - (ab_re - 1.0) * lam_im) / den
    bb_re = f_re[..., None] * b_re - f_im[..., None] * b_im
    bb_im = f_re[..., None] * b_im + f_im[..., None] * b_re
    eye = jnp.eye(S5_GB, dtype=F32)

    def in_mat(m):
        m = m.reshape(S5_NGB, S5_GB, S5_STATE, S5_GROUP)
        return jnp.einsum('bgpc,gh->bgchp', m, eye).reshape(S5_NGB, S5_UL, S5_SL)

    def out_mat(m):
        m = m.reshape(S5_NGB, S5_GB, S5_GROUP, S5_STATE)
        return jnp.einsum('bgcp,gh->bgphc', m, eye).reshape(S5_NGB, S5_SL, S5_UL)

    bb = jnp.concatenate([in_mat(bb_re), in_mat(bb_im)], axis=2).astype(BF16)
    cc = jnp.concatenate([out_mat(c_re), out_mat(-c_im)], axis=1).astype(BF16)
    dsk = d_skip.reshape(S5_NGB, 1, S5_UL)
    return bb, cc, dsk, ab_re.reshape(S5_NGB, 1, S5_SL), ab_im.reshape(S5_NGB, 1, S5_SL)


def _post_mix_kernel(xa_ref, oa_ref, ra_ref, ya_ref, xb_ref, ob_ref, rb_ref, yb_ref, *rest, n_first):
    i = pl.program_id(0)

    @pl.when(i < n_first)
    def _first():
        _post_mix_tile(xa_ref, oa_ref, ra_ref, ya_ref, *rest)

    @pl.when(i >= n_first)
    def _second():
        _post_mix_tile(xb_ref, ob_ref, rb_ref, yb_ref, *rest)


def _post_mix_tile(x_ref, o_ref, r_ref, y5_ref, ggla_ref, wglu_ref, bglu_ref, wout_ref, gffn_ref,
                   wr_hi_ref, wr_lo_ref, br_ref, h1_ref, hn_ref, lg_ref):
    o = o_ref[...]
    parts = []
    for h in range(GLA_HEADS):
        oh = o[:, h * GLA_DV:(h + 1) * GLA_DV]
        parts.append(oh * lax.rsqrt(jnp.mean(oh * oh, axis=-1, keepdims=True) + EPS))
    r = r_ref[...]
    o_gla = (jnp.concatenate(parts, axis=1) * ggla_ref[...]) * (r * jax.nn.sigmoid(r))
    y5 = y5_ref[...]
    z = y5 * (0.5 * (1.0 + jnp.tanh(math.sqrt(2.0 / math.pi) * (y5 + 0.044715 * (y5 * y5 * y5)))))
    o_s5 = z * jax.nn.sigmoid(_dot(z.astype(BF16), wglu_ref[...]) + bglu_ref[...])
    att = (_dot(o_gla.astype(BF16), wout_ref[0:GLA_WIDTH, :])
           + _dot(o_s5.astype(BF16), wout_ref[GLA_WIDTH:GLA_WIDTH + S5_WIDTH, :]))
    h1 = x_ref[...] + att
    h1_ref[...] = h1
    hn = _rms(h1, gffn_ref[...])
    hn_ref[...] = hn
    hn_hi, hn_lo = _split2(hn)
    lg_ref[...] = (_dot(hn_hi, wr_hi_ref[...]) + _dot(hn_hi, wr_lo_ref[...]) + _dot(hn_lo, wr_hi_ref[...])
                   + br_ref[...])


def _post_mix(rows_a, rows_b, g_gla, w_glu, b_glu, w_out, g_ffn, wr_hi, wr_lo, b_r, tm):
    na, nb = rows_a[0].shape[0], rows_b[0].shape[0]
    n_first = na // tm
    widths = [D_MODEL, GLA_WIDTH, GLA_WIDTH, S5_WIDTH]
    spec_a = [pl.BlockSpec((tm, w), lambda i: (jnp.minimum(i, n_first - 1), 0)) for w in widths]
    spec_b = [pl.BlockSpec((tm, w), lambda i: (jnp.maximum(i - n_first, 0), 0)) for w in widths]
    full = lambda a: pl.BlockSpec(a.shape, lambda i: (0,) * a.ndim)
    row = lambda w: pl.BlockSpec((tm, w), lambda i: (i, 0))
    weights = [g_gla, w_glu, b_glu, w_out, g_ffn, wr_hi, wr_lo, b_r]
    n = na + nb
    return pl.pallas_call(
        functools.partial(_post_mix_kernel, n_first=n_first),
        grid=(n // tm,),
        in_specs=spec_a + spec_b + [full(a) for a in weights],
        out_specs=[row(D_MODEL), row(D_MODEL), row(ROUTER_LANES)],
        out_shape=[jax.ShapeDtypeStruct((n, D_MODEL), F32), jax.ShapeDtypeStruct((n, D_MODEL), F32),
                   jax.ShapeDtypeStruct((n, ROUTER_LANES), F32)],
        compiler_params=_params(("parallel",)),
        name="post_mix",
    )(*rows_a, *rows_b, *weights)


def _gather_start(idx_ref, idx_base, idx_stride, src_hbm, dst, sem, n):
    def body(r, carry):
        row = idx_ref[idx_base + r * idx_stride]
        pltpu.make_async_copy(src_hbm.at[pl.ds(row, 1)], dst.at[pl.ds(r, 1)], sem).start()
        return carry
    lax.fori_loop(0, n, body, 0, unroll=8)


def _gather_wait(src_hbm, dst, sem, n):
    pltpu.make_async_copy(src_hbm.at[pl.ds(0, n)], dst.at[pl.ds(0, n)], sem).wait()


def _moe_kernel(bexp_ref, nused_ref, slot_ref, x_hbm, wg_ref, wu_ref, wd_ref, o_ref, xbuf, sem):
    b = pl.program_id(0)
    nu = nused_ref[0]

    def start(blk):
        slot = blk % 2
        _gather_start(slot_ref, blk * MOE_TM, 1, x_hbm, xbuf.at[slot], sem.at[slot], MOE_TM)

    @pl.when(b == 0)
    def _first():
        start(b)

    @pl.when(b + 1 < nu)
    def _next():
        start(b + 1)

    @pl.when(b < nu)
    def _run():
        slot = b % 2
        _gather_wait(x_hbm, xbuf.at[slot], sem.at[slot], MOE_TM)
        x = xbuf[slot].astype(BF16)
        gate = _dot(x, wg_ref[0].astype(BF16))
        up = _dot(x, wu_ref[0].astype(BF16))
        hid = (gate * jax.nn.sigmoid(gate)) * up
        o_ref[...] = _dot(hid.astype(BF16), wd_ref[0].astype(BF16))

    @pl.when(b >= nu)
    def _skip():
        o_ref[...] = jnp.zeros(o_ref.shape, o_ref.dtype)


def _moe(block_exp, n_used, slot_tok, hn, w_gate, w_up, w_down):
    nblk = slot_tok.shape[0] // MOE_TM
    wspec = lambda shape: pl.BlockSpec((1,) + shape, lambda b, be, nu, st: (be[b], 0, 0))
    return pl.pallas_call(
        _moe_kernel,
        grid_spec=pltpu.PrefetchScalarGridSpec(
            num_scalar_prefetch=3,
            grid=(nblk,),
            in_specs=[pl.BlockSpec(memory_space=pl.ANY), wspec((D_MODEL, EXPERT_HIDDEN)),
                      wspec((D_MODEL, EXPERT_HIDDEN)), wspec((EXPERT_HIDDEN, D_MODEL))],
            out_specs=pl.BlockSpec((MOE_TM, D_MODEL), lambda b, be, nu, st: (b, 0)),
            scratch_shapes=[pltpu.VMEM((2, MOE_TM, D_MODEL), F32), pltpu.SemaphoreType.DMA((2,))],
        ),
        out_shape=jax.ShapeDtypeStruct((nblk * MOE_TM, D_MODEL), F32),
        compiler_params=_params(("arbitrary",)),
        name="moe",
    )(block_exp, n_used, slot_tok, hn, w_gate, w_up, w_down)


def _final_kernel(dest_ref, h1_ref, wa_ref, wb_ref, g_ref, rows_hbm, y_ref, ybuf, sem, *, tm, tok_off):
    i = pl.program_id(0)

    def start(tile):
        slot = tile % 2
        base = (tok_off + tile * tm) * TOP_K
        for k in range(TOP_K):
            _gather_start(dest_ref, base + k, TOP_K, rows_hbm, ybuf.at[slot, k], sem.at[slot], tm)

    @pl.when(i == 0)
    def _first():
        start(i)

    @pl.when(i + 1 < pl.num_programs(0))
    def _next():
        start(i + 1)

    slot = i % 2
    for k in range(TOP_K):
        _gather_wait(rows_hbm, ybuf.at[slot, k], sem.at[slot], tm)
    moe = ybuf[slot, 0] * wa_ref[...] + ybuf[slot, 1] * wb_ref[...]
    y_ref[...] = _rms(h1_ref[...] + moe, g_ref[...])


def _final(dest, h1, wa, wb, g_final, rows, *, tm, n, row_off):
    off = row_off // tm
    row = lambda w: pl.BlockSpec((tm, w), lambda i, d: (i + off, 0))
    return pl.pallas_call(
        functools.partial(_final_kernel, tm=tm, tok_off=row_off),
        grid_spec=pltpu.PrefetchScalarGridSpec(
            num_scalar_prefetch=1,
            grid=(n // tm,),
            in_specs=[row(D_MODEL), row(1), row(1), pl.BlockSpec((1, D_MODEL), lambda i, d: (0, 0)),
                      pl.BlockSpec(memory_space=pl.ANY)],
            out_specs=pl.BlockSpec((tm, D_MODEL), lambda i, d: (i, 0)),
            scratch_shapes=[pltpu.VMEM((2, TOP_K, tm, D_MODEL), F32), pltpu.SemaphoreType.DMA((2,))],
        ),
        out_shape=jax.ShapeDtypeStruct((n, D_MODEL), F32),
        compiler_params=_params(("arbitrary",)),
        name="final",
    )(dest, h1, wa, wb, g_final, rows)


def _route(logits):
    lg = logits[:, :N_GROUPS]
    le = logits[:, N_GROUPS:N_GROUPS + N_EXPERTS].reshape(-1, N_GROUPS, EXPERTS_PER_GROUP)
    pg = jax.nn.softmax(lg, axis=-1)
    gsel = jnp.argmax(lg, axis=-1)
    pg_sel = jnp.take_along_axis(pg, gsel[:, None], axis=-1)[:, 0]
    le_sel = jnp.take_along_axis(le, gsel[:, None, None], axis=1)[:, 0]
    vals, idx = lax.top_k(le_sel, TOP_K)
    pe = jax.nn.softmax(vals, axis=-1)
    eid = (gsel[:, None] * EXPERTS_PER_GROUP + idx).astype(jnp.int32)
    return eid, pg_sel[:, None] * pe


def _dispatch(eid):
    t = eid.shape[0]
    a = t * TOP_K
    nblk = a // MOE_TM + N_EXPERTS
    flat = eid.reshape(-1)
    onehot = (flat[:, None] == jnp.arange(N_EXPERTS, dtype=jnp.int32)[None, :]).astype(jnp.int32)
    rank = jnp.take_along_axis(jnp.cumsum(onehot, axis=0) - onehot, flat[:, None], axis=1)[:, 0]
    counts = jnp.sum(onehot, axis=0)
    padded = (counts + MOE_TM - 1) // MOE_TM * MOE_TM
    pends = jnp.cumsum(padded)
    dest = (pends - padded)[flat] + rank
    tok = jnp.arange(a, dtype=jnp.int32) // TOP_K
    slot_tok = jnp.zeros((nblk * MOE_TM,), jnp.int32).at[dest].set(tok)
    n_used = (pends[-1] // MOE_TM).astype(jnp.int32)
    blk = jnp.minimum(jnp.arange(nblk, dtype=jnp.int32), n_used - 1)
    block_exp = jnp.minimum(jnp.searchsorted(pends, blk * MOE_TM, side='right'), N_EXPERTS - 1).astype(jnp.int32)
    return dest.reshape(t, TOP_K), slot_tok, block_exp, n_used.reshape(1)


def kernel(x_prompt, x_sample, state_gla, state_s5_re, state_s5_im, meta, g_mix, w_in, w_gk2, b_gk, g_gla,
           lam_re, lam_im, log_dt, s5_b_re, s5_b_im, s5_c_re, s5_c_im, d_skip, w_glu, b_glu, w_out, g_ffn,
           w_rg, b_rg, w_re, b_re, w_gate, w_up, w_down, g_final):
    bp, tp, _ = x_prompt.shape
    bs, ts, _ = x_sample.shape
    l = 0
    kw = GLA_KEY_WIDTH
    a0 = 2 * kw + 2 * GLA_WIDTH

    w = w_in[l]
    w_cat = jnp.concatenate(
        [w[:, :a0], w[:, a0 + GLA_RANK:], jnp.pad(w[:, a0:a0 + GLA_RANK], ((0, 0), (0, 128 - GLA_RANK)))],
        axis=1).astype(BF16)
    wgk_hi, wgk_lo = _split2(jnp.pad(w_gk2[l], ((0, 128 - GLA_RANK), (0, 0))))
    g_mix2, b_gk2 = g_mix[l][None], b_gk[l][None]
    mats = _s5_matrices(lam_re[l], lam_im[l], log_dt[l], s5_b_re[l], s5_b_im[l], s5_c_re[l], s5_c_im[l],
                        d_skip[l])
    w_router = jnp.concatenate([w_rg[l], jnp.moveaxis(w_re[l], 0, 1).reshape(D_MODEL, N_EXPERTS)], axis=1)
    w_router = jnp.pad(w_router, ((0, 0), (0, ROUTER_LANES - N_GROUPS - N_EXPERTS)))
    wr_hi, wr_lo = _split2(w_router)
    b_router = jnp.pad(jnp.concatenate([b_rg[l], b_re[l].reshape(-1)]),
                       (0, ROUTER_LANES - N_GROUPS - N_EXPERTS))[None]
    w_glu_b, w_out_b = w_glu[l].astype(BF16), w_out[l].astype(BF16)
    g_gla2, b_glu2, g_ffn2 = g_gla[l].reshape(1, GLA_WIDTH), b_glu[l][None], g_ffn[l][None]

    proj = functools.partial(_in_proj, g_mix=g_mix2, w_cat=w_cat, wgk_hi=wgk_hi, wgk_lo=wgk_lo, b_gk=b_gk2)

    qm, km, vm, _, gm, um = proj(meta, tm=N_META)
    zero_s = jnp.zeros((1, GLA_HEADS, GLA_DK, GLA_DV), F32)
    _, s_meta = _gla(qm, km, vm, gm, zero_s, seq_len=N_META, chunk=N_META)
    zero_h = jnp.zeros((1, S5_NGB, 1, S5_SL), F32)
    _, hm_re, hm_im = _s5_long(um, mats, zero_h, zero_h, seq_len=N_META)

    xp = x_prompt.reshape(bp * tp, D_MODEL)
    qp, kp, vp, rp, gp, up = proj(xp, tm=512)
    op, gla_p = _gla(qp, kp, vp, gp, jnp.broadcast_to(s_meta, (bp,) + s_meta.shape[1:]),
                     seq_len=tp, chunk=GLA_CHUNK)
    y5p, hp_re, hp_im = _s5_long(up, mats, jnp.broadcast_to(hm_re, (bp,) + hm_re.shape[1:]),
                                 jnp.broadcast_to(hm_im, (bp,) + hm_im.shape[1:]), seq_len=tp)

    xs = x_sample.reshape(bs * ts, D_MODEL)
    qs, ks, vs, rs, gs, us = proj(xs, tm=512)
    chunk_s = GLA_CHUNK if ts % GLA_CHUNK == 0 else ts
    os_, gla_s = _gla(qs, ks, vs, gs, state_gla[l], seq_len=ts, chunk=chunk_s)
    y5s, hs_re, hs_im = _s5_short(us, mats, state_s5_re[l].reshape(bs, -1), state_s5_im[l].reshape(bs, -1),
                                  seq_len=ts)

    np_rows, ns_rows = bp * tp, bs * ts
    h1, hn, logits = _post_mix((xp, op, rp, y5p), (xs, os_, rs, y5s), g_gla2, w_glu_b, b_glu2, w_out_b, g_ffn2,
                               wr_hi, wr_lo, b_router, tm=256)

    eid, wts = _route(logits)
    dest, slot_tok, block_exp, n_used = _dispatch(eid)
    out_rows = _moe(block_exp, n_used, slot_tok, hn, w_gate[l], w_up[l], w_down[l])
    fin = functools.partial(_final, dest.reshape(-1), h1, wts[:, 0:1], wts[:, 1:2], g_final[None], out_rows, tm=256)
    y_prompt = fin(n=np_rows, row_off=0)
    y_sample = fin(n=ns_rows, row_off=np_rows)

    return (y_prompt.reshape(bp, tp, D_MODEL), y_sample.reshape(bs, ts, D_MODEL),
            gla_p[None], hp_re.reshape(1, bp, S5_GROUPS, S5_STATE), hp_im.reshape(1, bp, S5_GROUPS, S5_STATE),
            gla_s[None], hs_re.reshape(1, bs, S5_GROUPS, S5_STATE), hs_im.reshape(1, bs, S5_GROUPS, S5_STATE))
```

```python
import functools
import math

import jax
import jax.numpy as jnp
from jax import lax
from jax.experimental import pallas as pl
from jax.experimental.pallas import tpu as pltpu

F32 = jnp.float32
BF16 = jnp.bfloat16

D_MODEL = 2048
N_META = 16
GLA_HEADS = 4
GLA_DK = 128
GLA_DV = 256
GLA_KEY_WIDTH = GLA_HEADS * GLA_DK
GLA_WIDTH = GLA_HEADS * GLA_DV
GLA_RANK = 16
GLA_GATE_NORM = 16.0
GLA_CHUNK = 64
GLA_SUB = 16
GLA_SEQS_PER_STEP = 8
MASKED_EXPONENT = -1e30
S5_WIDTH = 1024
S5_GROUP = 16
S5_GROUPS = 64
S5_STATE = 64
S5_GB = 8
S5_NGB = S5_GROUPS // S5_GB
S5_UL = S5_GB * S5_GROUP
S5_SL = S5_GB * S5_STATE
S5_LT = 2 * S5_SL // 128
S5_UNROLL = 4
N_GROUPS = 4
EXPERTS_PER_GROUP = 8
N_EXPERTS = N_GROUPS * EXPERTS_PER_GROUP
EXPERT_HIDDEN = 512
TOP_K = 2
EPS = 1e-6
ROUTER_LANES = 128
MOE_TM = 256
VMEM_LIMIT = 56 * 1024 * 1024

_dot = functools.partial(jnp.dot, preferred_element_type=F32)


def _split2(x):
    hi = x.astype(BF16)
    lo = (x - hi.astype(F32)).astype(BF16)
    return hi, lo


def _rms(x, g):
    return x * lax.rsqrt(jnp.mean(x * x, axis=-1, keepdims=True) + EPS) * g


def _params(sem):
    return pltpu.CompilerParams(dimension_semantics=sem, vmem_limit_bytes=VMEM_LIMIT)


def _in_proj_kernel(x_ref, g_ref, w_ref, wgk_hi_ref, wgk_lo_ref, bgk_ref,
                    q_ref, k_ref, v_ref, r_ref, gk_ref, u_ref):
    xb = _rms(x_ref[...], g_ref[...]).astype(BF16)
    kw = GLA_KEY_WIDTH
    q_ref[...] = _dot(xb, w_ref[:, 0:kw]) * (GLA_DK ** -0.5)
    k_ref[...] = _dot(xb, w_ref[:, kw:2 * kw])
    v_ref[...] = _dot(xb, w_ref[:, 2 * kw:2 * kw + GLA_WIDTH])
    r_ref[...] = _dot(xb, w_ref[:, 2 * kw + GLA_WIDTH:2 * kw + 2 * GLA_WIDTH])
    c0 = 2 * kw + 2 * GLA_WIDTH
    u_ref[...] = _dot(xb, w_ref[:, c0:c0 + S5_WIDTH])
    a_low = _dot(xb, w_ref[:, c0 + S5_WIDTH:c0 + S5_WIDTH + 128])
    a_hi, a_lo = _split2(a_low)
    z = (_dot(a_hi, wgk_hi_ref[...]) + _dot(a_hi, wgk_lo_ref[...]) + _dot(a_lo, wgk_hi_ref[...])
         + bgk_ref[...])
    gk_ref[...] = (jnp.minimum(z, 0.0) - jnp.log1p(jnp.exp(-jnp.abs(z)))) * (1.0 / GLA_GATE_NORM)


def _in_proj(x, g_mix, w_cat, wgk_hi, wgk_lo, b_gk, tm):
    n = x.shape[0]
    wcols = w_cat.shape[1]
    row = lambda w: pl.BlockSpec((tm, w), lambda i: (i, 0))
    full = lambda a: pl.BlockSpec(a.shape, lambda i: (0,) * a.ndim)
    widths = [GLA_KEY_WIDTH, GLA_KEY_WIDTH, GLA_WIDTH, GLA_WIDTH, GLA_KEY_WIDTH, S5_WIDTH]
    return pl.pallas_call(
        _in_proj_kernel,
        grid=(n // tm,),
        in_specs=[row(D_MODEL), full(g_mix),
                  pl.BlockSpec((D_MODEL, wcols), lambda i: (0, 0), pipeline_mode=pl.Buffered(1)),
                  full(wgk_hi), full(wgk_lo), full(b_gk)],
        out_specs=[row(w) for w in widths],
        out_shape=[jax.ShapeDtypeStruct((n, w), F32) for w in widths],
        compiler_params=_params(("parallel",)),
        name="in_proj",
    )(x, g_mix, w_cat, wgk_hi, wgk_lo, b_gk)


def _gla_chunk(q_ref, k_ref, v_ref, g_ref, o_ref, r0, C, consts, read_state, write_state):
    tril, sub_rows, score_lanes = consts
    sub = min(C, GLA_SUB)
    nsub = C // sub
    g = g_ref[pl.ds(r0, C), :]
    g1 = g.astype(BF16)
    rem = g - g1.astype(F32)
    g2 = rem.astype(BF16)
    g3 = (rem - g2.astype(F32)).astype(BF16)
    b_all = _dot(tril, g1) + _dot(tril, g2) + _dot(tril, g3)
    for h in range(GLA_HEADS):
        ks = slice(h * GLA_DK, (h + 1) * GLA_DK)
        vs = slice(h * GLA_DV, (h + 1) * GLA_DV)
        q = q_ref[pl.ds(r0, C), ks]
        k = k_ref[pl.ds(r0, C), ks]
        b = b_all[:, ks]
        s_prev = read_state(h)
        vb = v_ref[pl.ds(r0, C), vs].astype(BF16)
        o_inter = _dot((q * jnp.exp(b)).astype(BF16), s_prev.astype(BF16))
        blocks = []
        for s in range(nsub):
            lo = s * sub
            bs, qs, ksub = b[lo:lo + sub], q[lo:lo + sub], k[lo:lo + sub]
            acc = o_inter[lo:lo + sub]
            if s > 0:
                anchor = b[lo - 1:lo]
                qd = (qs * jnp.exp(bs - anchor)).astype(BF16)
                kd = (k[:lo] * jnp.exp(anchor - b[:lo])).astype(BF16)
                sc = lax.dot_general(qd, kd, (((1,), (1,)), ((), ())), preferred_element_type=F32)
                acc = acc + _dot(sc.astype(BF16), vb[:lo])
            scores = jnp.zeros((sub, 128), F32)
            for jj in range(sub):
                causal = sub_rows >= jj
                dec = jnp.exp(jnp.where(causal, bs - bs[jj:jj + 1], MASKED_EXPONENT))
                col = jnp.sum((qs * ksub[jj:jj + 1]) * dec, axis=-1, keepdims=True)
                scores = jnp.where(score_lanes == jj, col, scores)
            acc = acc + _dot(scores[:, :sub].astype(BF16), vb[lo:lo + sub])
            blocks.append(acc)
        o_ref[pl.ds(r0, C), vs] = jnp.concatenate(blocks, axis=0) if nsub > 1 else blocks[0]
        b_last = b[C - 1:C]
        kdec = (k * jnp.exp(b_last - b)).astype(BF16)
        upd = lax.dot_general(kdec, vb, (((0,), (0,)), ((), ())), preferred_element_type=F32)
        dcol = jnp.broadcast_to(jnp.exp(b_last), (GLA_DK, GLA_DK)).T
        write_state(h, s_prev * jnp.concatenate([dcol, dcol], axis=1) + upd)


def _gla_consts(C):
    sub = min(C, GLA_SUB)
    ri = lax.broadcasted_iota(jnp.int32, (C, C), 0)
    ci = lax.broadcasted_iota(jnp.int32, (C, C), 1)
    tril = jnp.where(ri >= ci, 1.0, 0.0).astype(BF16)
    sub_rows = lax.broadcasted_iota(jnp.int32, (sub, GLA_DK), 0)
    score_lanes = lax.broadcasted_iota(jnp.int32, (sub, 128), 1)
    return tril, sub_rows, score_lanes


def _gla_long_kernel(q_ref, k_ref, v_ref, g_ref, s0_ref, o_ref, sout_ref, s_scr, *, chunk, n_inner):
    j = pl.program_id(1)
    consts = _gla_consts(chunk)

    @pl.when(j == 0)
    def _init():
        s_scr[...] = s0_ref[0]

    def write_state(h, s):
        s_scr[h] = s

    def chunk_body(c, carry):
        _gla_chunk(q_ref, k_ref, v_ref, g_ref, o_ref, pl.multiple_of(c * chunk, chunk), chunk, consts,
                   lambda h: s_scr[h], write_state)
        return carry

    lax.fori_loop(0, n_inner, chunk_body, 0)

    @pl.when(j == pl.num_programs(1) - 1)
    def _fin():
        sout_ref[0] = s_scr[...]


def _gla_short_kernel(q_ref, k_ref, v_ref, g_ref, s0_ref, o_ref, sout_ref, *, chunk, nb):
    consts = _gla_consts(chunk)

    def seq_body(n, carry):
        def write_state(h, s):
            sout_ref[n, h] = s

        _gla_chunk(q_ref, k_ref, v_ref, g_ref, o_ref, pl.multiple_of(n * chunk, chunk), chunk, consts,
                   lambda h: s0_ref[n, h], write_state)
        return carry

    lax.fori_loop(0, nb, seq_body, 0, unroll=2 if nb % 2 == 0 else 1)


def _gla(q, k, v, g, s0, *, seq_len, chunk):
    nseq = s0.shape[0]
    n = q.shape[0]
    out_shape = [jax.ShapeDtypeStruct((n, GLA_WIDTH), F32),
                 jax.ShapeDtypeStruct((nseq, GLA_HEADS, GLA_DK, GLA_DV), F32)]
    if seq_len == chunk:
        nb = min(nseq, GLA_SEQS_PER_STEP)
        rows = lambda w: pl.BlockSpec((nb * chunk, w), lambda s: (s, 0))
        st = pl.BlockSpec((nb, GLA_HEADS, GLA_DK, GLA_DV), lambda s: (s, 0, 0, 0))
        return pl.pallas_call(
            functools.partial(_gla_short_kernel, chunk=chunk, nb=nb),
            grid=(nseq // nb,),
            in_specs=[rows(GLA_KEY_WIDTH), rows(GLA_KEY_WIDTH), rows(GLA_WIDTH), rows(GLA_KEY_WIDTH), st],
            out_specs=[rows(GLA_WIDTH), st],
            out_shape=out_shape,
            compiler_params=_params(("parallel",)),
            name="gla_short",
        )(q, k, v, g, s0)
    rb = min(seq_len, 4 * chunk)
    nblk = seq_len // rb
    rows = lambda w: pl.BlockSpec((rb, w), lambda s, j: (s * nblk + j, 0))
    st = pl.BlockSpec((1, GLA_HEADS, GLA_DK, GLA_DV), lambda s, j: (s, 0, 0, 0))
    return pl.pallas_call(
        functools.partial(_gla_long_kernel, chunk=chunk, n_inner=rb // chunk),
        grid=(nseq, nblk),
        in_specs=[rows(GLA_KEY_WIDTH), rows(GLA_KEY_WIDTH), rows(GLA_WIDTH), rows(GLA_KEY_WIDTH), st],
        out_specs=[rows(GLA_WIDTH), st],
        out_shape=out_shape,
        scratch_shapes=[pltpu.VMEM((GLA_HEADS, GLA_DK, GLA_DV), F32)],
        compiler_params=_params(("parallel", "arbitrary")),
        name="gla_long",
    )(q, k, v, g, s0)


def _cmul(ar, ai, br, bi):
    return ar * br - ai * bi, ar * bi + ai * br


def _cpow(ar, ai, n):
    res = None
    while n:
        if n & 1:
            res = (ar, ai) if res is None else _cmul(res[0], res[1], ar, ai)
        n >>= 1
        if n:
            ar, ai = _cmul(ar, ai, ar, ai)
    return res


def _s5_segment_len(seq_len):
    s = -(-seq_len // 8)
    s = -(-s // 4) * 4
    return s if (s // 4) % 2 == 1 else s + 4


def _st_store(st_scr, rows, val):
    for t in range(S5_LT):
        st_scr[t, rows, :] = val[:, t * 128:(t + 1) * 128]


def _st_load(st_scr, rows):
    return jnp.concatenate([st_scr[t, rows, :] for t in range(S5_LT)], axis=1)


def _s5_long_kernel(u_ref, bb_ref, cc_ref, dsk_ref, are_ref, aim_ref, h0re_ref, h0im_ref,
                    y_ref, hre_ref, him_ref, st_scr, *, seq_len, seg):
    T, S, SL = seq_len, seg, S5_SL
    rc = min(T, 512)
    for c in range(T // rc):
        sl = slice(c * rc, (c + 1) * rc)
        _st_store(st_scr, sl, _dot(u_ref[sl, :].astype(BF16), bb_ref[0]))
    if 8 * S > T:
        _st_store(st_scr, slice(T, 8 * S), jnp.zeros((8 * S - T, 2 * SL), F32))
    a_re, a_im = are_ref[0], aim_ref[0]
    ar = jnp.broadcast_to(a_re, (8, SL))
    ai = jnp.broadcast_to(a_im, (8, SL))

    def step(i, h, store):
        strand = pl.ds(i, 8, stride=S)
        x = _st_load(st_scr, strand)
        mr, mi = _cmul(ar, ai, h[0], h[1])
        nr, ni = mr + x[:, :SL], mi + x[:, SL:]
        if store:
            _st_store(st_scr, strand, jnp.concatenate([nr, ni], axis=1))
        return nr, ni

    zero = jnp.zeros((8, SL), F32)
    fr, fi = lax.fori_loop(0, S, functools.partial(step, store=False), (zero, zero), unroll=S5_UNROLL)

    as_re, as_im = _cpow(a_re, a_im, S)
    rows = lax.broadcasted_iota(jnp.int32, (8, SL), 0)
    cr, ci = h0re_ref[0, 0], h0im_ref[0, 0]
    car_r, car_i = zero, zero
    for r in range(8):
        car_r = jnp.where(rows == r, cr, car_r)
        car_i = jnp.where(rows == r, ci, car_i)
        if r < 7:
            mr, mi = _cmul(as_re, as_im, cr, ci)
            cr, ci = mr + fr[r:r + 1], mi + fi[r:r + 1]

    lax.fori_loop(0, S, functools.partial(step, store=True), (car_r, car_i), unroll=S5_UNROLL)

    last = _st_load(st_scr, slice(T - 1, T))
    hre_ref[0, 0] = last[:, :SL]
    him_ref[0, 0] = last[:, SL:]
    for c in range(T // rc):
        sl = slice(c * rc, (c + 1) * rc)
        y_ref[sl, :] = _dot(_st_load(st_scr, sl).astype(BF16), cc_ref[0]) + dsk_ref[0] * u_ref[sl, :]


def _s5_long(u, mats, h0_re, h0_im, *, seq_len):
    bb, cc, dsk, a_re, a_im = mats
    nseq = h0_re.shape[0]
    seg = _s5_segment_len(seq_len)
    gb3 = lambda shape: pl.BlockSpec((1,) + shape, lambda s, g: (g, 0, 0))
    st = pl.BlockSpec((1, 1, 1, S5_SL), lambda s, g: (s, g, 0, 0))
    urow = pl.BlockSpec((seq_len, S5_UL), lambda s, g: (s, g))
    st_shape = jax.ShapeDtypeStruct((nseq, S5_NGB, 1, S5_SL), F32)
    return pl.pallas_call(
        functools.partial(_s5_long_kernel, seq_len=seq_len, seg=seg),
        grid=(nseq, S5_NGB),
        in_specs=[urow, gb3((S5_UL, 2 * S5_SL)), gb3((2 * S5_SL, S5_UL)), gb3((1, S5_UL)),
                  gb3((1, S5_SL)), gb3((1, S5_SL)), st, st],
        out_specs=[urow, st, st],
        out_shape=[jax.ShapeDtypeStruct(u.shape, F32), st_shape, st_shape],
        scratch_shapes=[pltpu.VMEM((S5_LT, 8 * seg, 128), F32)],
        compiler_params=_params(("parallel", "parallel")),
        name="s5_long",
    )(u, bb, cc, dsk, a_re, a_im, h0_re, h0_im)


def _s5_short_kernel(u_ref, bb_ref, cc_ref, dsk_ref, are_ref, aim_ref, h0re_ref, h0im_ref,
                     y_ref, hre_ref, him_ref, st_scr, *, seq_len, nseq):
    SL = S5_SL
    _st_store(st_scr, slice(None), _dot(u_ref[...].astype(BF16), bb_ref[0]))
    ar = jnp.broadcast_to(are_ref[0], (nseq, SL))
    ai = jnp.broadcast_to(aim_ref[0], (nseq, SL))
    hr, hi = h0re_ref[...], h0im_ref[...]
    for t in range(seq_len):
        step = pl.ds(t, nseq, stride=seq_len)
        x = _st_load(st_scr, step)
        mr, mi = _cmul(ar, ai, hr, hi)
        hr, hi = mr + x[:, :SL], mi + x[:, SL:]
        _st_store(st_scr, step, jnp.concatenate([hr, hi], axis=1))
    hre_ref[...] = hr
    him_ref[...] = hi
    y_ref[...] = _dot(_st_load(st_scr, slice(None)).astype(BF16), cc_ref[0]) + dsk_ref[0] * u_ref[...]


def _s5_short(u, mats, h0_re, h0_im, *, seq_len):
    bb, cc, dsk, a_re, a_im = mats
    nseq = h0_re.shape[0]
    n = nseq * seq_len
    gb3 = lambda shape: pl.BlockSpec((1,) + shape, lambda g: (g, 0, 0))
    st = pl.BlockSpec((nseq, S5_SL), lambda g: (0, g))
    urow = pl.BlockSpec((n, S5_UL), lambda g: (0, g))
    st_shape = jax.ShapeDtypeStruct(h0_re.shape, F32)
    return pl.pallas_call(
        functools.partial(_s5_short_kernel, seq_len=seq_len, nseq=nseq),
        grid=(S5_NGB,),
        in_specs=[urow, gb3((S5_UL, 2 * S5_SL)), gb3((2 * S5_SL, S5_UL)), gb3((1, S5_UL)),
                  gb3((1, S5_SL)), gb3((1, S5_SL)), st, st],
        out_specs=[urow, st, st],
        out_shape=[jax.ShapeDtypeStruct(u.shape, F32), st_shape, st_shape],
        scratch_shapes=[pltpu.VMEM((S5_LT, n, 128), F32)],
        compiler_params=_params(("parallel",)),
        name="s5_short",
    )(u, bb, cc, dsk, a_re, a_im, h0_re, h0_im)


def _s5_matrices(lam_re, lam_im, log_dt, b_re, b_im, c_re, c_im, d_skip):
    dt = jnp.exp(log_dt)[:, None]
    mag = jnp.exp(lam_re * dt)
    ab_re, ab_im = mag * jnp.cos(lam_im * dt), mag * jnp.sin(lam_im * dt)
    den = lam_re * lam_re + lam_im * lam_im
    f_re = ((ab_re - 1.0) * lam_re + ab_im * lam_im) / den
    f_im = (ab_im * lam_re - (ab_re - 1.0) * lam_im) / den
    bb_re = f_re[..., None] * b_re - f_im[..., None] * b_im
    bb_im = f_re[..., None] * b_im + f_im[..., None] * b_re
    eye = jnp.eye(S5_GB, dtype=F32)

    def in_mat(m):
        m = m.reshape(S5_NGB, S5_GB, S5_STATE, S5_GROUP)
        return jnp.einsum('bgpc,gh->bgchp', m, eye).reshape(S5_NGB, S5_UL, S5_SL)

    def out_mat(m):
        m = m.reshape(S5_NGB, S5_GB, S5_GROUP, S5_STATE)
        return jnp.einsum('bgcp,gh->bgphc', m, eye).reshape(S5_NGB, S5_SL, S5_UL)

    bb = jnp.concatenate([in_mat(bb_re), in_mat(bb_im)], axis=2).astype(BF16)
    cc = jnp.concatenate([out_mat(c_re), out_mat(-c_im)], axis=1).astype(BF16)
    dsk = d_skip.reshape(S5_NGB, 1, S5_UL)
    return bb, cc, dsk, ab_re.reshape(S5_NGB, 1, S5_SL), ab_im.reshape(S5_NGB, 1, S5_SL)


def _post_mix_kernel(xa_ref, oa_ref, ra_ref, ya_ref, xb_ref, ob_ref, rb_ref, yb_ref, *rest, n_first):
    i = pl.program_id(0)

    @pl.when(i < n_first)
    def _first():
        _post_mix_tile(xa_ref, oa_ref, ra_ref, ya_ref, *rest)

    @pl.when(i >= n_first)
    def _second():
        _post_mix_tile(xb_ref, ob_ref, rb_ref, yb_ref, *rest)


def _post_mix_tile(x_ref, o_ref, r_ref, y5_ref, ggla_ref, wglu_ref, bglu_ref, wout_ref, gffn_ref,
                   wr_hi_ref, wr_lo_ref, br_ref, h1_ref, hn_ref, lg_ref):
    o = o_ref[...]
    parts = []
    for h in range(GLA_HEADS):
        oh = o[:, h * GLA_DV:(h + 1) * GLA_DV]
        parts.append(oh * lax.rsqrt(jnp.mean(oh * oh, axis=-1, keepdims=True) + EPS))
    r = r_ref[...]
    o_gla = (jnp.concatenate(parts, axis=1) * ggla_ref[...]) * (r * jax.nn.sigmoid(r))
    y5 = y5_ref[...]
    z = y5 * (0.5 * (1.0 + jnp.tanh(math.sqrt(2.0 / math.pi) * (y5 + 0.044715 * (y5 * y5 * y5)))))
    o_s5 = z * jax.nn.sigmoid(_dot(z.astype(BF16), wglu_ref[...]) + bglu_ref[...])
    att = (_dot(o_gla.astype(BF16), wout_ref[0:GLA_WIDTH, :])
           + _dot(o_s5.astype(BF16), wout_ref[GLA_WIDTH:GLA_WIDTH + S5_WIDTH, :]))
    h1 = x_ref[...] + att
    h1_ref[...] = h1
    hn = _rms(h1, gffn_ref[...])
    hn_ref[...] = hn
    hn_hi, hn_lo = _split2(hn)
    lg_ref[...] = (_dot(hn_hi, wr_hi_ref[...]) + _dot(hn_hi, wr_lo_ref[...]) + _dot(hn_lo, wr_hi_ref[...])
                   + br_ref[...])


def _post_mix(rows_a, rows_b, g_gla, w_glu, b_glu, w_out, g_ffn, wr_hi, wr_lo, b_r, tm):
    na, nb = rows_a[0].shape[0], rows_b[0].shape[0]
    n_first = na // tm
    widths = [D_MODEL, GLA_WIDTH, GLA_WIDTH, S5_WIDTH]
    spec_a = [pl.BlockSpec((tm, w), lambda i: (jnp.minimum(i, n_first - 1), 0)) for w in widths]
    spec_b = [pl.BlockSpec((tm, w), lambda i: (jnp.maximum(i - n_first, 0), 0)) for w in widths]
    full = lambda a: pl.BlockSpec(a.shape, lambda i: (0,) * a.ndim)
    row = lambda w: pl.BlockSpec((tm, w), lambda i: (i, 0))
    weights = [g_gla, w_glu, b_glu, w_out, g_ffn, wr_hi, wr_lo, b_r]
    n = na + nb
    return pl.pallas_call(
        functools.partial(_post_mix_kernel, n_first=n_first),
        grid=(n // tm,),
        in_specs=spec_a + spec_b + [full(a) for a in weights],
        out_specs=[row(D_MODEL), row(D_MODEL), row(ROUTER_LANES)],
        out_shape=[jax.ShapeDtypeStruct((n, D_MODEL), F32), jax.ShapeDtypeStruct((n, D_MODEL), F32),
                   jax.ShapeDtypeStruct((n, ROUTER_LANES), F32)],
        compiler_params=_params(("parallel",)),
        name="post_mix",
    )(*rows_a, *rows_b, *weights)


def _gather_start(idx_ref, idx_base, idx_stride, src_hbm, dst, sem, n):
    def body(r, carry):
        row = idx_ref[idx_base + r * idx_stride]
        pltpu.make_async_copy(src_hbm.at[pl.ds(row, 1)], dst.at[pl.ds(r, 1)], sem).start()
        return carry
    lax.fori_loop(0, n, body, 0, unroll=8)


def _gather_wait(src_hbm, dst, sem, n):
    pltpu.make_async_copy(src_hbm.at[pl.ds(0, n)], dst.at[pl.ds(0, n)], sem).wait()


def _moe_kernel(bexp_ref, nused_ref, slot_ref, x_hbm, wg_ref, wu_ref, wd_ref, o_ref, xbuf, sem,
                wg_bf, wu_bf, wd_bf):
    b = pl.program_id(0)
    nu = nused_ref[0]
    new_expert = jnp.logical_or(b == 0, bexp_ref[b] != bexp_ref[jnp.maximum(b - 1, 0)])

    @pl.when(jnp.logical_and(b < nu, new_expert))
    def _cast_weights():
        wg_bf[...] = wg_ref[0].astype(BF16)
        wu_bf[...] = wu_ref[0].astype(BF16)
        wd_bf[...] = wd_ref[0].astype(BF16)

    def start(blk):
        slot = blk % 2
        _gather_start(slot_ref, blk * MOE_TM, 1, x_hbm, xbuf.at[slot], sem.at[slot], MOE_TM)

    @pl.when(b == 0)
    def _first():
        start(b)

    @pl.when(b + 1 < nu)
    def _next():
        start(b + 1)

    @pl.when(b < nu)
    def _run():
        slot = b % 2
        _gather_wait(x_hbm, xbuf.at[slot], sem.at[slot], MOE_TM)
        x = xbuf[slot].astype(BF16)
        gate = _dot(x, wg_bf[...])
        up = _dot(x, wu_bf[...])
        hid = (gate * jax.nn.sigmoid(gate)) * up
        o_ref[...] = _dot(hid.astype(BF16), wd_bf[...])

    @pl.when(b >= nu)
    def _skip():
        o_ref[...] = jnp.zeros(o_ref.shape, o_ref.dtype)


def _moe(block_exp, n_used, slot_tok, hn, w_gate, w_up, w_down):
    nblk = slot_tok.shape[0] // MOE_TM
    wspec = lambda shape: pl.BlockSpec((1,) + shape, lambda b, be, nu, st: (be[b], 0, 0))
    return pl.pallas_call(
        _moe_kernel,
        grid_spec=pltpu.PrefetchScalarGridSpec(
            num_scalar_prefetch=3,
            grid=(nblk,),
            in_specs=[pl.BlockSpec(memory_space=pl.ANY), wspec((D_MODEL, EXPERT_HIDDEN)),
                      wspec((D_MODEL, EXPERT_HIDDEN)), wspec((EXPERT_HIDDEN, D_MODEL))],
            out_specs=pl.BlockSpec((MOE_TM, D_MODEL), lambda b, be, nu, st: (b, 0)),
            scratch_shapes=[pltpu.VMEM((2, MOE_TM, D_MODEL), F32), pltpu.SemaphoreType.DMA((2,)),
                            pltpu.VMEM((D_MODEL, EXPERT_HIDDEN), BF16), pltpu.VMEM((D_MODEL, EXPERT_HIDDEN), BF16),
                            pltpu.VMEM((EXPERT_HIDDEN, D_MODEL), BF16)],
        ),
        out_shape=jax.ShapeDtypeStruct((nblk * MOE_TM, D_MODEL), F32),
        compiler_params=_params(("arbitrary",)),
        name="moe",
    )(block_exp, n_used, slot_tok, hn, w_gate, w_up, w_down)


def _final_kernel(dest_ref, h1_ref, wa_ref, wb_ref, g_ref, rows_hbm, y_ref, ybuf, sem, *, tm, tok_off):
    i = pl.program_id(0)

    def start(tile):
        slot = tile % 2
        base = (tok_off + tile * tm) * TOP_K
        for k in range(TOP_K):
            _gather_start(dest_ref, base + k, TOP_K, rows_hbm, ybuf.at[slot, k], sem.at[slot], tm)

    @pl.when(i == 0)
    def _first():
        start(i)

    @pl.when(i + 1 < pl.num_programs(0))
    def _next():
        start(i + 1)

    slot = i % 2
    for k in range(TOP_K):
        _gather_wait(rows_hbm, ybuf.at[slot, k], sem.at[slot], tm)
    moe = ybuf[slot, 0] * wa_ref[...] + ybuf[slot, 1] * wb_ref[...]
    y_ref[...] = _rms(h1_ref[...] + moe, g_ref[...])


def _final(dest, h1, wa, wb, g_final, rows, *, tm, n, row_off):
    off = row_off // tm
    row = lambda w: pl.BlockSpec((tm, w), lambda i, d: (i + off, 0))
    return pl.pallas_call(
        functools.partial(_final_kernel, tm=tm, tok_off=row_off),
        grid_spec=pltpu.PrefetchScalarGridSpec(
            num_scalar_prefetch=1,
            grid=(n // tm,),
            in_specs=[row(D_MODEL), row(1), row(1), pl.BlockSpec((1, D_MODEL), lambda i, d: (0, 0)),
                      pl.BlockSpec(memory_space=pl.ANY)],
            out_specs=pl.BlockSpec((tm, D_MODEL), lambda i, d: (i, 0)),
            scratch_shapes=[pltpu.VMEM((2, TOP_K, tm, D_MODEL), F32), pltpu.SemaphoreType.DMA((2,))],
        ),
        out_shape=jax.ShapeDtypeStruct((n, D_MODEL), F32),
        compiler_params=_params(("arbitrary",)),
        name="final",
    )(dest, h1, wa, wb, g_final, rows)


def _route(logits):
    lg = logits[:, :N_GROUPS]
    le = logits[:, N_GROUPS:N_GROUPS + N_EXPERTS].reshape(-1, N_GROUPS, EXPERTS_PER_GROUP)
    pg = jax.nn.softmax(lg, axis=-1)
    gsel = jnp.argmax(lg, axis=-1)
    pg_sel = jnp.take_along_axis(pg, gsel[:, None], axis=-1)[:, 0]
    le_sel = jnp.take_along_axis(le, gsel[:, None, None], axis=1)[:, 0]
    vals, idx = lax.top_k(le_sel, TOP_K)
    pe = jax.nn.softmax(vals, axis=-1)
    eid = (gsel[:, None] * EXPERTS_PER_GROUP + idx).astype(jnp.int32)
    return eid, pg_sel[:, None] * pe


def _dispatch(eid):
    t = eid.shape[0]
    a = t * TOP_K
    nblk = a // MOE_TM + N_EXPERTS
    flat = eid.reshape(-1)
    onehot = (flat[:, None] == jnp.arange(N_EXPERTS, dtype=jnp.int32)[None, :]).astype(jnp.int32)
    rank = jnp.take_along_axis(jnp.cumsum(onehot, axis=0) - onehot, flat[:, None], axis=1)[:, 0]
    counts = jnp.sum(onehot, axis=0)
    padded = (counts + MOE_TM - 1) // MOE_TM * MOE_TM
    pends = jnp.cumsum(padded)
    dest = (pends - padded)[flat] + rank
    tok = jnp.arange(a, dtype=jnp.int32) // TOP_K
    slot_tok = jnp.zeros((nblk * MOE_TM,), jnp.int32).at[dest].set(tok)
    n_used = (pends[-1] // MOE_TM).astype(jnp.int32)
    blk = jnp.minimum(jnp.arange(nblk, dtype=jnp.int32), n_used - 1)
    block_exp = jnp.minimum(jnp.searchsorted(pends, blk * MOE_TM, side='right'), N_EXPERTS - 1).astype(jnp.int32)
    return dest.reshape(t, TOP_K), slot_tok, block_exp, n_used.reshape(1)


def kernel(x_prompt, x_sample, state_gla, state_s5_re, state_s5_im, meta, g_mix, w_in, w_gk2, b_gk, g_gla,
           lam_re, lam_im, log_dt, s5_b_re, s5_b_im, s5_c_re, s5_c_im, d_skip, w_glu, b_glu, w_out, g_ffn,
           w_rg, b_rg, w_re, b_re, w_gate, w_up, w_down, g_final):
    bp, tp, _ = x_prompt.shape
    bs, ts, _ = x_sample.shape
    l = 0
    kw = GLA_KEY_WIDTH
    a0 = 2 * kw + 2 * GLA_WIDTH

    w = w_in[l]
    w_cat = jnp.concatenate(
        [w[:, :a0], w[:, a0 + GLA_RANK:], jnp.pad(w[:, a0:a0 + GLA_RANK], ((0, 0), (0, 128 - GLA_RANK)))],
        axis=1).astype(BF16)
    wgk_hi, wgk_lo = _split2(jnp.pad(w_gk2[l], ((0, 128 - GLA_RANK), (0, 0))))
    g_mix2, b_gk2 = g_mix[l][None], b_gk[l][None]
    mats = _s5_matrices(lam_re[l], lam_im[l], log_dt[l], s5_b_re[l], s5_b_im[l], s5_c_re[l], s5_c_im[l],
                        d_skip[l])
    w_router = jnp.concatenate([w_rg[l], jnp.moveaxis(w_re[l], 0, 1).reshape(D_MODEL, N_EXPERTS)], axis=1)
    w_router = jnp.pad(w_router, ((0, 0), (0, ROUTER_LANES - N_GROUPS - N_EXPERTS)))
    wr_hi, wr_lo = _split2(w_router)
    b_router = jnp.pad(jnp.concatenate([b_rg[l], b_re[l].reshape(-1)]),
                       (0, ROUTER_LANES - N_GROUPS - N_EXPERTS))[None]
    w_glu_b, w_out_b = w_glu[l].astype(BF16), w_out[l].astype(BF16)
    g_gla2, b_glu2, g_ffn2 = g_gla[l].reshape(1, GLA_WIDTH), b_glu[l][None], g_ffn[l][None]

    proj = functools.partial(_in_proj, g_mix=g_mix2, w_cat=w_cat, wgk_hi=wgk_hi, wgk_lo=wgk_lo, b_gk=b_gk2)

    qm, km, vm, _, gm, um = proj(meta, tm=N_META)
    zero_s = jnp.zeros((1, GLA_HEADS, GLA_DK, GLA_DV), F32)
    _, s_meta = _gla(qm, km, vm, gm, zero_s, seq_len=N_META, chunk=N_META)
    zero_h = jnp.zeros((1, S5_NGB, 1, S5_SL), F32)
    _, hm_re, hm_im = _s5_long(um, mats, zero_h, zero_h, seq_len=N_META)

    xp = x_prompt.reshape(bp * tp, D_MODEL)
    qp, kp, vp, rp, gp, up = proj(xp, tm=512)
    op, gla_p = _gla(qp, kp, vp, gp, jnp.broadcast_to(s_meta, (bp,) + s_meta.shape[1:]),
                     seq_len=tp, chunk=GLA_CHUNK)
    y5p, hp_re, hp_im = _s5_long(up, mats, jnp.broadcast_to(hm_re, (bp,) + hm_re.shape[1:]),
                                 jnp.broadcast_to(hm_im, (bp,) + hm_im.shape[1:]), seq_len=tp)

    xs = x_sample.reshape(bs * ts, D_MODEL)
    qs, ks, vs, rs, gs, us = proj(xs, tm=512)
    chunk_s = GLA_CHUNK if ts % GLA_CHUNK == 0 else ts
    os_, gla_s = _gla(qs, ks, vs, gs, state_gla[l], seq_len=ts, chunk=chunk_s)
    y5s, hs_re, hs_im = _s5_short(us, mats, state_s5_re[l].reshape(bs, -1), state_s5_im[l].reshape(bs, -1),
                                  seq_len=ts)

    np_rows, ns_rows = bp * tp, bs * ts
    h1, hn, logits = _post_mix((xp, op, rp, y5p), (xs, os_, rs, y5s), g_gla2, w_glu_b, b_glu2, w_out_b, g_ffn2,
                               wr_hi, wr_lo, b_router, tm=256)

    eid, wts = _route(logits)
    dest, slot_tok, block_exp, n_used = _dispatch(eid)
    out_rows = _moe(block_exp, n_used, slot_tok, hn, w_gate[l], w_up[l], w_down[l])
    fin = functools.partial(_final, dest.reshape(-1), h1, wts[:, 0:1], wts[:, 1:2], g_final[None], out_rows, tm=256)
    y_prompt = fin(n=np_rows, row_off=0)
    y_sample = fin(n=ns_rows, row_off=np_rows)

    return (y_prompt.reshape(bp, tp, D_MODEL), y_sample.reshape(bs, ts, D_MODEL),
            gla_p[None], hp_re.reshape(1, bp, S5_GROUPS, S5_STATE), hp_im.reshape(1, bp, S5_GROUPS, S5_STATE),
            gla_s[None], hs_re.reshape(1, bs, S5_GROUPS, S5_STATE), hs_im.reshape(1, bs, S5_GROUPS, S5_STATE))
```

```python
import functools
import math

import jax
import jax.numpy as jnp
from jax import lax
from jax.experimental import pallas as pl
from jax.experimental.pallas import tpu as pltpu

F32 = jnp.float32
BF16 = jnp.bfloat16

D_MODEL = 2048
N_META = 16
GLA_HEADS = 4
GLA_DK = 128
GLA_DV = 256
GLA_KEY_WIDTH = GLA_HEADS * GLA_DK
GLA_WIDTH = GLA_HEADS * GLA_DV
GLA_RANK = 16
GLA_GATE_NORM = 16.0
GLA_CHUNK = 64
GLA_SUB = 16
GLA_SEQS_PER_STEP = 8
MASKED_EXPONENT = -1e30
S5_WIDTH = 1024
S5_GROUP = 16
S5_GROUPS = 64
S5_STATE = 64
S5_GB = 8
S5_NGB = S5_GROUPS // S5_GB
S5_UL = S5_GB * S5_GROUP
S5_SL = S5_GB * S5_STATE
S5_LT = 2 * S5_SL // 128
S5_UNROLL = 4
N_GROUPS = 4
EXPERTS_PER_GROUP = 8
N_EXPERTS = N_GROUPS * EXPERTS_PER_GROUP
EXPERT_HIDDEN = 512
TOP_K = 2
EPS = 1e-6
ROUTER_LANES = 128
MOE_TM = 256
MXU_COLS = 256
WEIGHT_DMA_PRIORITY = 1
VMEM_LIMIT = 56 * 1024 * 1024

_dot = functools.partial(jnp.dot, preferred_element_type=F32)


def _split2(x):
    hi = x.astype(BF16)
    lo = (x - hi.astype(F32)).astype(BF16)
    return hi, lo


def _rms(x, g):
    return x * lax.rsqrt(jnp.mean(x * x, axis=-1, keepdims=True) + EPS) * g


def _params(sem):
    return pltpu.CompilerParams(dimension_semantics=sem, vmem_limit_bytes=VMEM_LIMIT)


def _in_proj_kernel(x_ref, g_ref, w_ref, wgk_hi_ref, wgk_lo_ref, bgk_ref,
                    q_ref, k_ref, v_ref, r_ref, gk_ref, u_ref):
    xb = _rms(x_ref[...], g_ref[...]).astype(BF16)
    kw = GLA_KEY_WIDTH
    q_ref[...] = _dot(xb, w_ref[:, 0:kw]) * (GLA_DK ** -0.5)
    k_ref[...] = _dot(xb, w_ref[:, kw:2 * kw])
    v_ref[...] = _dot(xb, w_ref[:, 2 * kw:2 * kw + GLA_WIDTH])
    r_ref[...] = _dot(xb, w_ref[:, 2 * kw + GLA_WIDTH:2 * kw + 2 * GLA_WIDTH])
    c0 = 2 * kw + 2 * GLA_WIDTH
    u_ref[...] = _dot(xb, w_ref[:, c0:c0 + S5_WIDTH])
    a_low = _dot(xb, w_ref[:, c0 + S5_WIDTH:c0 + S5_WIDTH + 128])
    a_hi, a_lo = _split2(a_low)
    z = (_dot(a_hi, wgk_hi_ref[...]) + _dot(a_hi, wgk_lo_ref[...]) + _dot(a_lo, wgk_hi_ref[...])
         + bgk_ref[...])
    gk_ref[...] = (jnp.minimum(z, 0.0) - jnp.log1p(jnp.exp(-jnp.abs(z)))) * (1.0 / GLA_GATE_NORM)


def _in_proj(x, g_mix, w_cat, wgk_hi, wgk_lo, b_gk, tm):
    n = x.shape[0]
    wcols = w_cat.shape[1]
    row = lambda w: pl.BlockSpec((tm, w), lambda i: (i, 0))
    full = lambda a: pl.BlockSpec(a.shape, lambda i: (0,) * a.ndim)
    widths = [GLA_KEY_WIDTH, GLA_KEY_WIDTH, GLA_WIDTH, GLA_WIDTH, GLA_KEY_WIDTH, S5_WIDTH]
    return pl.pallas_call(
        _in_proj_kernel,
        grid=(n // tm,),
        in_specs=[row(D_MODEL), full(g_mix),
                  pl.BlockSpec((D_MODEL, wcols), lambda i: (0, 0), pipeline_mode=pl.Buffered(1)),
                  full(wgk_hi), full(wgk_lo), full(b_gk)],
        out_specs=[row(w) for w in widths],
        out_shape=[jax.ShapeDtypeStruct((n, w), F32) for w in widths],
        compiler_params=_params(("parallel",)),
        name="in_proj",
    )(x, g_mix, w_cat, wgk_hi, wgk_lo, b_gk)


def _gla_chunk(q_ref, k_ref, v_ref, g_ref, o_ref, r0, C, consts, read_state, write_state):
    tril, sub_rows, score_lanes = consts
    sub = min(C, GLA_SUB)
    nsub = C // sub
    g = g_ref[pl.ds(r0, C), :]
    g1 = g.astype(BF16)
    rem = g - g1.astype(F32)
    g2 = rem.astype(BF16)
    g3 = (rem - g2.astype(F32)).astype(BF16)
    b_all = _dot(tril, g1) + _dot(tril, g2) + _dot(tril, g3)
    for h in range(GLA_HEADS):
        ks = slice(h * GLA_DK, (h + 1) * GLA_DK)
        vs = slice(h * GLA_DV, (h + 1) * GLA_DV)
        q = q_ref[pl.ds(r0, C), ks]
        k = k_ref[pl.ds(r0, C), ks]
        b = b_all[:, ks]
        s_prev = read_state(h)
        vb = v_ref[pl.ds(r0, C), vs].astype(BF16)
        o_inter = _dot((q * jnp.exp(b)).astype(BF16), s_prev.astype(BF16))
        blocks = []
        for s in range(nsub):
            lo = s * sub
            bs, qs, ksub = b[lo:lo + sub], q[lo:lo + sub], k[lo:lo + sub]
            acc = o_inter[lo:lo + sub]
            if s > 0:
                anchor = b[lo - 1:lo]
                qd = (qs * jnp.exp(bs - anchor)).astype(BF16)
                kd = (k[:lo] * jnp.exp(anchor - b[:lo])).astype(BF16)
                sc = lax.dot_general(qd, kd, (((1,), (1,)), ((), ())), preferred_element_type=F32)
                acc = acc + _dot(sc.astype(BF16), vb[:lo])
            scores = jnp.zeros((sub, 128), F32)
            for jj in range(sub):
                causal = sub_rows >= jj
                dec = jnp.exp(jnp.where(causal, bs - bs[jj:jj + 1], MASKED_EXPONENT))
                col = jnp.sum((qs * ksub[jj:jj + 1]) * dec, axis=-1, keepdims=True)
                scores = jnp.where(score_lanes == jj, col, scores)
            acc = acc + _dot(scores[:, :sub].astype(BF16), vb[lo:lo + sub])
            blocks.append(acc)
        o_ref[pl.ds(r0, C), vs] = jnp.concatenate(blocks, axis=0) if nsub > 1 else blocks[0]
        b_last = b[C - 1:C]
        kdec = (k * jnp.exp(b_last - b)).astype(BF16)
        upd = lax.dot_general(kdec, vb, (((0,), (0,)), ((), ())), preferred_element_type=F32)
        dcol = jnp.broadcast_to(jnp.exp(b_last), (GLA_DK, GLA_DK)).T
        write_state(h, s_prev * jnp.concatenate([dcol, dcol], axis=1) + upd)


def _gla_consts(C):
    sub = min(C, GLA_SUB)
    ri = lax.broadcasted_iota(jnp.int32, (C, C), 0)
    ci = lax.broadcasted_iota(jnp.int32, (C, C), 1)
    tril = jnp.where(ri >= ci, 1.0, 0.0).astype(BF16)
    sub_rows = lax.broadcasted_iota(jnp.int32, (sub, GLA_DK), 0)
    score_lanes = lax.broadcasted_iota(jnp.int32, (sub, 128), 1)
    return tril, sub_rows, score_lanes


def _gla_long_kernel(q_ref, k_ref, v_ref, g_ref, s0_ref, o_ref, sout_ref, s_scr, *, chunk, n_inner):
    j = pl.program_id(1)
    consts = _gla_consts(chunk)

    @pl.when(j == 0)
    def _init():
        s_scr[...] = s0_ref[0]

    def write_state(h, s):
        s_scr[h] = s

    def chunk_body(c, carry):
        _gla_chunk(q_ref, k_ref, v_ref, g_ref, o_ref, pl.multiple_of(c * chunk, chunk), chunk, consts,
                   lambda h: s_scr[h], write_state)
        return carry

    lax.fori_loop(0, n_inner, chunk_body, 0)

    @pl.when(j == pl.num_programs(1) - 1)
    def _fin():
        sout_ref[0] = s_scr[...]


def _gla_short_kernel(q_ref, k_ref, v_ref, g_ref, s0_ref, o_ref, sout_ref, *, chunk, nb):
    consts = _gla_consts(chunk)

    def seq_body(n, carry):
        def write_state(h, s):
            sout_ref[n, h] = s

        _gla_chunk(q_ref, k_ref, v_ref, g_ref, o_ref, pl.multiple_of(n * chunk, chunk), chunk, consts,
                   lambda h: s0_ref[n, h], write_state)
        return carry

    lax.fori_loop(0, nb, seq_body, 0, unroll=2 if nb % 2 == 0 else 1)


def _gla(q, k, v, g, s0, *, seq_len, chunk):
    nseq = s0.shape[0]
    n = q.shape[0]
    out_shape = [jax.ShapeDtypeStruct((n, GLA_WIDTH), F32),
                 jax.ShapeDtypeStruct((nseq, GLA_HEADS, GLA_DK, GLA_DV), F32)]
    if seq_len == chunk:
        nb = min(nseq, GLA_SEQS_PER_STEP)
        rows = lambda w: pl.BlockSpec((nb * chunk, w), lambda s: (s, 0))
        st = pl.BlockSpec((nb, GLA_HEADS, GLA_DK, GLA_DV), lambda s: (s, 0, 0, 0))
        return pl.pallas_call(
            functools.partial(_gla_short_kernel, chunk=chunk, nb=nb),
            grid=(nseq // nb,),
            in_specs=[rows(GLA_KEY_WIDTH), rows(GLA_KEY_WIDTH), rows(GLA_WIDTH), rows(GLA_KEY_WIDTH), st],
            out_specs=[rows(GLA_WIDTH), st],
            out_shape=out_shape,
            compiler_params=_params(("parallel",)),
            name="gla_short",
        )(q, k, v, g, s0)
    rb = min(seq_len, 4 * chunk)
    nblk = seq_len // rb
    rows = lambda w: pl.BlockSpec((rb, w), lambda s, j: (s * nblk + j, 0))
    st = pl.BlockSpec((1, GLA_HEADS, GLA_DK, GLA_DV), lambda s, j: (s, 0, 0, 0))
    return pl.pallas_call(
        functools.partial(_gla_long_kernel, chunk=chunk, n_inner=rb // chunk),
        grid=(nseq, nblk),
        in_specs=[rows(GLA_KEY_WIDTH), rows(GLA_KEY_WIDTH), rows(GLA_WIDTH), rows(GLA_KEY_WIDTH), st],
        out_specs=[rows(GLA_WIDTH), st],
        out_shape=out_shape,
        scratch_shapes=[pltpu.VMEM((GLA_HEADS, GLA_DK, GLA_DV), F32)],
        compiler_params=_params(("parallel", "arbitrary")),
        name="gla_long",
    )(q, k, v, g, s0)


def _cmul(ar, ai, br, bi):
    return ar * br - ai * bi, ar * bi + ai * br


def _cpow(ar, ai, n):
    res = None
    while n:
        if n & 1:
            res = (ar, ai) if res is None else _cmul(res[0], res[1], ar, ai)
        n >>= 1
        if n:
            ar, ai = _cmul(ar, ai, ar, ai)
    return res


def _s5_segment_len(seq_len):
    s = -(-seq_len // 8)
    s = -(-s // 4) * 4
    return s if (s // 4) % 2 == 1 else s + 4


def _st_store(st_scr, rows, val):
    for t in range(S5_LT):
        st_scr[t, rows, :] = val[:, t * 128:(t + 1) * 128]


def _st_load(st_scr, rows):
    return jnp.concatenate([st_scr[t, rows, :] for t in range(S5_LT)], axis=1)


def _s5_long_kernel(u_ref, bb_ref, cc_ref, dsk_ref, are_ref, aim_ref, h0re_ref, h0im_ref,
                    y_ref, hre_ref, him_ref, st_scr, *, seq_len, seg):
    T, S, SL = seq_len, seg, S5_SL
    rc = min(T, 512)
    for c in range(T // rc):
        sl = slice(c * rc, (c + 1) * rc)
        _st_store(st_scr, sl, _dot(u_ref[sl, :].astype(BF16), bb_ref[0]))
    if 8 * S > T:
        _st_store(st_scr, slice(T, 8 * S), jnp.zeros((8 * S - T, 2 * SL), F32))
    a_re, a_im = are_ref[0], aim_ref[0]
    ar = jnp.broadcast_to(a_re, (8, SL))
    ai = jnp.broadcast_to(a_im, (8, SL))

    def step(i, h, store):
        strand = pl.ds(i, 8, stride=S)
        x = _st_load(st_scr, strand)
        mr, mi = _cmul(ar, ai, h[0], h[1])
        nr, ni = mr + x[:, :SL], mi + x[:, SL:]
        if store:
            _st_store(st_scr, strand, jnp.concatenate([nr, ni], axis=1))
        return nr, ni

    zero = jnp.zeros((8, SL), F32)
    fr, fi = lax.fori_loop(0, S, functools.partial(step, store=False), (zero, zero), unroll=S5_UNROLL)

    as_re, as_im = _cpow(a_re, a_im, S)
    rows = lax.broadcasted_iota(jnp.int32, (8, SL), 0)
    cr, ci = h0re_ref[0, 0], h0im_ref[0, 0]
    car_r, car_i = zero, zero
    for r in range(8):
        car_r = jnp.where(rows == r, cr, car_r)
        car_i = jnp.where(rows == r, ci, car_i)
        if r < 7:
            mr, mi = _cmul(as_re, as_im, cr, ci)
            cr, ci = mr + fr[r:r + 1], mi + fi[r:r + 1]

    lax.fori_loop(0, S, functools.partial(step, store=True), (car_r, car_i), unroll=S5_UNROLL)

    last = _st_load(st_scr, slice(T - 1, T))
    hre_ref[0, 0] = last[:, :SL]
    him_ref[0, 0] = last[:, SL:]
    for c in range(T // rc):
        sl = slice(c * rc, (c + 1) * rc)
        y_ref[sl, :] = _dot(_st_load(st_scr, sl).astype(BF16), cc_ref[0]) + dsk_ref[0] * u_ref[sl, :]


def _s5_long(u, mats, h0_re, h0_im, *, seq_len):
    bb, cc, dsk, a_re, a_im = mats
    nseq = h0_re.shape[0]
    seg = _s5_segment_len(seq_len)
    gb3 = lambda shape: pl.BlockSpec((1,) + shape, lambda s, g: (g, 0, 0))
    st = pl.BlockSpec((1, 1, 1, S5_SL), lambda s, g: (s, g, 0, 0))
    urow = pl.BlockSpec((seq_len, S5_UL), lambda s, g: (s, g))
    st_shape = jax.ShapeDtypeStruct((nseq, S5_NGB, 1, S5_SL), F32)
    return pl.pallas_call(
        functools.partial(_s5_long_kernel, seq_len=seq_len, seg=seg),
        grid=(nseq, S5_NGB),
        in_specs=[urow, gb3((S5_UL, 2 * S5_SL)), gb3((2 * S5_SL, S5_UL)), gb3((1, S5_UL)),
                  gb3((1, S5_SL)), gb3((1, S5_SL)), st, st],
        out_specs=[urow, st, st],
        out_shape=[jax.ShapeDtypeStruct(u.shape, F32), st_shape, st_shape],
        scratch_shapes=[pltpu.VMEM((S5_LT, 8 * seg, 128), F32)],
        compiler_params=_params(("parallel", "parallel")),
        name="s5_long",
    )(u, bb, cc, dsk, a_re, a_im, h0_re, h0_im)


def _s5_short_kernel(u_ref, bb_ref, cc_ref, dsk_ref, are_ref, aim_ref, h0re_ref, h0im_ref,
                     y_ref, hre_ref, him_ref, st_scr, *, seq_len, nseq):
    SL = S5_SL
    _st_store(st_scr, slice(None), _dot(u_ref[...].astype(BF16), bb_ref[0]))
    ar = jnp.broadcast_to(are_ref[0], (nseq, SL))
    ai = jnp.broadcast_to(aim_ref[0], (nseq, SL))
    hr, hi = h0re_ref[...], h0im_ref[...]
    for t in range(seq_len):
        step = pl.ds(t, nseq, stride=seq_len)
        x = _st_load(st_scr, step)
        mr, mi = _cmul(ar, ai, hr, hi)
        hr, hi = mr + x[:, :SL], mi + x[:, SL:]
        _st_store(st_scr, step, jnp.concatenate([hr, hi], axis=1))
    hre_ref[...] = hr
    him_ref[...] = hi
    y_ref[...] = _dot(_st_load(st_scr, slice(None)).astype(BF16), cc_ref[0]) + dsk_ref[0] * u_ref[...]


def _s5_short(u, mats, h0_re, h0_im, *, seq_len):
    bb, cc, dsk, a_re, a_im = mats
    nseq = h0_re.shape[0]
    n = nseq * seq_len
    gb3 = lambda shape: pl.BlockSpec((1,) + shape, lambda g: (g, 0, 0))
    st = pl.BlockSpec((nseq, S5_SL), lambda g: (0, g))
    urow = pl.BlockSpec((n, S5_UL), lambda g: (0, g))
    st_shape = jax.ShapeDtypeStruct(h0_re.shape, F32)
    return pl.pallas_call(
        functools.partial(_s5_short_kernel, seq_len=seq_len, nseq=nseq),
        grid=(S5_NGB,),
        in_specs=[urow, gb3((S5_UL, 2 * S5_SL)), gb3((2 * S5_SL, S5_UL)), gb3((1, S5_UL)),
                  gb3((1, S5_SL)), gb3((1, S5_SL)), st, st],
        out_specs=[urow, st, st],
        out_shape=[jax.ShapeDtypeStruct(u.shape, F32), st_shape, st_shape],
        scratch_shapes=[pltpu.VMEM((S5_LT, n, 128), F32)],
        compiler_params=_params(("parallel",)),
        name="s5_short",
    )(u, bb, cc, dsk, a_re, a_im, h0_re, h0_im)


def _s5_matrices(lam_re, lam_im, log_dt, b_re, b_im, c_re, c_im, d_skip):
    dt = jnp.exp(log_dt)[:, None]
    mag = jnp.exp(lam_re * dt)
    ab_re, ab_im = mag * jnp.cos(lam_im * dt), mag * jnp.sin(lam_im * dt)
    den = lam_re * lam_re + lam_im * lam_im
    f_re = ((ab_re - 1.0) * lam_re + ab_im * lam_im) / den
    f_im = (ab_im * lam_re - (ab_re - 1.0) * lam_im) / den
    bb_re = f_re[..., None] * b_re - f_im[..., None] * b_im
    bb_im = f_re[..., None] * b_im + f_im[..., None] * b_re
    eye = jnp.eye(S5_GB, dtype=F32)

    def in_mat(m):
        m = m.reshape(S5_NGB, S5_GB, S5_STATE, S5_GROUP)
        return jnp.einsum('bgpc,gh->bgchp', m, eye).reshape(S5_NGB, S5_UL, S5_SL)

    def out_mat(m):
        m = m.reshape(S5_NGB, S5_GB, S5_GROUP, S5_STATE)
        return jnp.einsum('bgcp,gh->bgphc', m, eye).reshape(S5_NGB, S5_SL, S5_UL)

    bb = jnp.concatenate([in_mat(bb_re), in_mat(bb_im)], axis=2).astype(BF16)
    cc = jnp.concatenate([out_mat(c_re), out_mat(-c_im)], axis=1).astype(BF16)
    dsk = d_skip.reshape(S5_NGB, 1, S5_UL)
    return bb, cc, dsk, ab_re.reshape(S5_NGB, 1, S5_SL), ab_im.reshape(S5_NGB, 1, S5_SL)


def _post_mix_kernel(xa_ref, oa_ref, ra_ref, ya_ref, xb_ref, ob_ref, rb_ref, yb_ref, *rest, n_first):
    i = pl.program_id(0)

    @pl.when(i < n_first)
    def _first():
        _post_mix_tile(xa_ref, oa_ref, ra_ref, ya_ref, *rest)

    @pl.when(i >= n_first)
    def _second():
        _post_mix_tile(xb_ref, ob_ref, rb_ref, yb_ref, *rest)


def _post_mix_tile(x_ref, o_ref, r_ref, y5_ref, ggla_ref, wglu_ref, bglu_ref, wout_ref, gffn_ref,
                   wr_hi_ref, wr_lo_ref, br_ref, h1_ref, hn_ref, lg_ref):
    o = o_ref[...]
    parts = []
    for h in range(GLA_HEADS):
        oh = o[:, h * GLA_DV:(h + 1) * GLA_DV]
        parts.append(oh * lax.rsqrt(jnp.mean(oh * oh, axis=-1, keepdims=True) + EPS))
    r = r_ref[...]
    o_gla = (jnp.concatenate(parts, axis=1) * ggla_ref[...]) * (r * jax.nn.sigmoid(r))
    y5 = y5_ref[...]
    z = y5 * (0.5 * (1.0 + jnp.tanh(math.sqrt(2.0 / math.pi) * (y5 + 0.044715 * (y5 * y5 * y5)))))
    o_s5 = z * jax.nn.sigmoid(_dot(z.astype(BF16), wglu_ref[...]) + bglu_ref[...])
    att = (_dot(o_gla.astype(BF16), wout_ref[0:GLA_WIDTH, :])
           + _dot(o_s5.astype(BF16), wout_ref[GLA_WIDTH:GLA_WIDTH + S5_WIDTH, :]))
    h1 = x_ref[...] + att
    h1_ref[...] = h1
    hn = _rms(h1, gffn_ref[...])
    hn_ref[...] = _pack_bf16_pairs(hn)
    hn_hi, hn_lo = _split2(hn)
    lg_ref[...] = (_dot(hn_hi, wr_hi_ref[...]) + _dot(hn_hi, wr_lo_ref[...]) + _dot(hn_lo, wr_hi_ref[...])
                   + br_ref[...])


def _post_mix(rows_a, rows_b, g_gla, w_glu, b_glu, w_out, g_ffn, wr_hi, wr_lo, b_r, tm):
    na, nb = rows_a[0].shape[0], rows_b[0].shape[0]
    n_first = na // tm
    widths = [D_MODEL, GLA_WIDTH, GLA_WIDTH, S5_WIDTH]
    spec_a = [pl.BlockSpec((tm, w), lambda i: (jnp.minimum(i, n_first - 1), 0)) for w in widths]
    spec_b = [pl.BlockSpec((tm, w), lambda i: (jnp.maximum(i - n_first, 0), 0)) for w in widths]
    full = lambda a: pl.BlockSpec(a.shape, lambda i: (0,) * a.ndim)
    row = lambda w: pl.BlockSpec((tm, w), lambda i: (i, 0))
    weights = [g_gla, w_glu, b_glu, w_out, g_ffn, wr_hi, wr_lo, b_r]
    n = na + nb
    return pl.pallas_call(
        functools.partial(_post_mix_kernel, n_first=n_first),
        grid=(n // tm,),
        in_specs=spec_a + spec_b + [full(a) for a in weights],
        out_specs=[row(D_MODEL), row(D_MODEL // 2), row(ROUTER_LANES)],
        out_shape=[jax.ShapeDtypeStruct((n, D_MODEL), F32), jax.ShapeDtypeStruct((n, D_MODEL // 2), jnp.uint32),
                   jax.ShapeDtypeStruct((n, ROUTER_LANES), F32)],
        compiler_params=_params(("parallel",)),
        name="post_mix",
    )(*rows_a, *rows_b, *weights)


def _gather_start(idx_ref, idx_base, idx_stride, src_hbm, dst, sem, n):
    def body(r, carry):
        row = idx_ref[idx_base + r * idx_stride]
        pltpu.make_async_copy(src_hbm.at[pl.ds(row, 1)], dst.at[pl.ds(r, 1)], sem).start()
        return carry
    lax.fori_loop(0, n, body, 0, unroll=8)


def _pack_bf16_pairs(x):
    half = x.shape[1] // 2
    bits = lax.bitcast_convert_type(x.astype(BF16).astype(F32), jnp.uint32)
    return bits[:, half:] | (bits[:, :half] >> 16)


def _unpack_bf16_pairs(p):
    lo = lax.bitcast_convert_type(p << 16, F32).astype(BF16)
    hi = lax.bitcast_convert_type(p & jnp.uint32(0xFFFF0000), F32).astype(BF16)
    return jnp.concatenate([lo, hi], axis=1)


def _gather_wait(src_hbm, dst, sem, n):
    pltpu.make_async_copy(src_hbm.at[pl.ds(0, n)], dst.at[pl.ds(0, n)], sem).wait()


def _moe_kernel(bexp_ref, eord_ref, next_ref, nused_ref, slot_ref, x_hbm, wg_hbm, wu_hbm, wd_hbm, o_ref,
                xbuf0, xbuf1, xsem, wg_st, wu_st, wd_st, wsem, wg_bf, wu_bf, wd_bf):
    b = pl.program_id(0)
    nu = nused_ref[0]
    e = bexp_ref[b]
    new_expert = jnp.logical_or(b == 0, bexp_ref[jnp.maximum(b - 1, 0)] != e)

    def weight_copies(expert, slot):
        return [pltpu.make_async_copy(src.at[expert], dst.at[slot], wsem.at[slot])
                for src, dst in ((wg_hbm, wg_st), (wu_hbm, wu_st), (wd_hbm, wd_st))]

    @pl.when(b == 0)
    def _prologue():
        for c in weight_copies(e, 0):
            c.start(priority=WEIGHT_DMA_PRIORITY)
        _gather_start(slot_ref, 0, 1, x_hbm, xbuf0, xsem.at[0], MOE_TM)

    @pl.when(jnp.logical_and(b < nu, new_expert))
    def _new_expert():
        slot = eord_ref[b] % 2
        for c in weight_copies(e, slot):
            c.wait()
        nxt = next_ref[b]

        @pl.when(nxt >= 0)
        def _prefetch():
            for c in weight_copies(nxt, 1 - slot):
                c.start(priority=WEIGHT_DMA_PRIORITY)

        wg_bf[...] = wg_st[slot].astype(BF16)
        wu_bf[...] = wu_st[slot].astype(BF16)
        wd_bf[...] = wd_st[slot].astype(BF16)

    def run(xcur, sem_cur, xnext, sem_next):
        _gather_wait(x_hbm, xcur, sem_cur, MOE_TM)
        x = _unpack_bf16_pairs(xcur[...])
        todo = iter(range(MOE_TM))
        base = (b + 1) * MOE_TM

        def issue(count):
            for _ in range(count):
                r = next(todo, None)
                if r is not None:
                    pltpu.make_async_copy(x_hbm.at[pl.ds(slot_ref[base + r], 1)], xnext.at[pl.ds(r, 1)],
                                          sem_next).start()

        n_hid, n_out = EXPERT_HIDDEN // MXU_COLS, D_MODEL // MXU_COLS
        per = -(-MOE_TM // (2 * n_hid + n_out))
        halves = []
        for c in range(n_hid):
            cols = slice(c * MXU_COLS, (c + 1) * MXU_COLS)
            gate = _dot(x, wg_bf[:, cols])
            issue(per)
            up = _dot(x, wu_bf[:, cols])
            issue(per)
            halves.append(((gate * jax.nn.sigmoid(gate)) * up).astype(BF16))
        hid = jnp.concatenate(halves, axis=1)
        for c in range(n_out):
            cols = slice(c * MXU_COLS, (c + 1) * MXU_COLS)
            o_ref[:, cols] = _dot(hid, wd_bf[:, cols])
            issue(per)
        issue(MOE_TM)

    even = b % 2 == 0

    @pl.when(jnp.logical_and(b < nu, even))
    def _run_even():
        run(xbuf0, xsem.at[0], xbuf1, xsem.at[1])

    @pl.when(jnp.logical_and(b < nu, jnp.logical_not(even)))
    def _run_odd():
        run(xbuf1, xsem.at[1], xbuf0, xsem.at[0])

    @pl.when(jnp.logical_and(b == nu, even))
    def _drain_even():
        _gather_wait(x_hbm, xbuf0, xsem.at[0], MOE_TM)

    @pl.when(jnp.logical_and(b == nu, jnp.logical_not(even)))
    def _drain_odd():
        _gather_wait(x_hbm, xbuf1, xsem.at[1], MOE_TM)

    @pl.when(b >= nu)
    def _skip():
        o_ref[...] = jnp.zeros(o_ref.shape, o_ref.dtype)


def _moe(block_exp, block_ord, block_next, n_used, slot_tok, hn, w_gate, w_up, w_down):
    nblk = slot_tok.shape[0] // MOE_TM
    any_spec = pl.BlockSpec(memory_space=pl.ANY)
    return pl.pallas_call(
        _moe_kernel,
        grid_spec=pltpu.PrefetchScalarGridSpec(
            num_scalar_prefetch=5,
            grid=(nblk,),
            in_specs=[any_spec, any_spec, any_spec, any_spec],
            out_specs=pl.BlockSpec((MOE_TM, D_MODEL), lambda b, *_: (b, 0)),
            scratch_shapes=[pltpu.VMEM((MOE_TM, D_MODEL // 2), jnp.uint32),
                            pltpu.VMEM((MOE_TM, D_MODEL // 2), jnp.uint32),
                            pltpu.SemaphoreType.DMA((2,)),
                            pltpu.VMEM((2, D_MODEL, EXPERT_HIDDEN), F32), pltpu.VMEM((2, D_MODEL, EXPERT_HIDDEN), F32),
                            pltpu.VMEM((2, EXPERT_HIDDEN, D_MODEL), F32), pltpu.SemaphoreType.DMA((2,)),
                            pltpu.VMEM((D_MODEL, EXPERT_HIDDEN), BF16), pltpu.VMEM((D_MODEL, EXPERT_HIDDEN), BF16),
                            pltpu.VMEM((EXPERT_HIDDEN, D_MODEL), BF16)],
        ),
        out_shape=jax.ShapeDtypeStruct((nblk * MOE_TM, D_MODEL), F32),
        compiler_params=_params(("arbitrary",)),
        name="moe",
    )(block_exp, block_ord, block_next, n_used, slot_tok, hn, w_gate, w_up, w_down)


def _final_kernel(dest_ref, h1_ref, wa_ref, wb_ref, g_ref, rows_hbm, y_ref, ybuf, sem, *, tm, tok_off):
    i = pl.program_id(0)

    def start(tile):
        slot = tile % 2
        base = (tok_off + tile * tm) * TOP_K
        for k in range(TOP_K):
            _gather_start(dest_ref, base + k, TOP_K, rows_hbm, ybuf.at[slot, k], sem.at[slot], tm)

    @pl.when(i == 0)
    def _first():
        start(i)

    @pl.when(i + 1 < pl.num_programs(0))
    def _next():
        start(i + 1)

    slot = i % 2
    for k in range(TOP_K):
        _gather_wait(rows_hbm, ybuf.at[slot, k], sem.at[slot], tm)
    moe = ybuf[slot, 0] * wa_ref[...] + ybuf[slot, 1] * wb_ref[...]
    y_ref[...] = _rms(h1_ref[...] + moe, g_ref[...])


def _final(dest, h1, wa, wb, g_final, rows, *, tm, n, row_off):
    off = row_off // tm
    row = lambda w: pl.BlockSpec((tm, w), lambda i, d: (i + off, 0))
    return pl.pallas_call(
        functools.partial(_final_kernel, tm=tm, tok_off=row_off),
        grid_spec=pltpu.PrefetchScalarGridSpec(
            num_scalar_prefetch=1,
            grid=(n // tm,),
            in_specs=[row(D_MODEL), row(1), row(1), pl.BlockSpec((1, D_MODEL), lambda i, d: (0, 0)),
                      pl.BlockSpec(memory_space=pl.ANY)],
            out_specs=pl.BlockSpec((tm, D_MODEL), lambda i, d: (i, 0)),
            scratch_shapes=[pltpu.VMEM((2, TOP_K, tm, D_MODEL), F32), pltpu.SemaphoreType.DMA((2,))],
        ),
        out_shape=jax.ShapeDtypeStruct((n, D_MODEL), F32),
        compiler_params=_params(("arbitrary",)),
        name="final",
    )(dest, h1, wa, wb, g_final, rows)


def _route(logits):
    lg = logits[:, :N_GROUPS]
    le = logits[:, N_GROUPS:N_GROUPS + N_EXPERTS].reshape(-1, N_GROUPS, EXPERTS_PER_GROUP)
    pg = jax.nn.softmax(lg, axis=-1)
    gsel = jnp.argmax(lg, axis=-1)
    pg_sel = jnp.take_along_axis(pg, gsel[:, None], axis=-1)[:, 0]
    le_sel = jnp.take_along_axis(le, gsel[:, None, None], axis=1)[:, 0]
    vals, idx = lax.top_k(le_sel, TOP_K)
    pe = jax.nn.softmax(vals, axis=-1)
    eid = (gsel[:, None] * EXPERTS_PER_GROUP + idx).astype(jnp.int32)
    return eid, pg_sel[:, None] * pe


def _dispatch(eid):
    t = eid.shape[0]
    a = t * TOP_K
    assert a % MOE_TM == 0
    nblk = a // MOE_TM + N_EXPERTS
    flat = eid.reshape(-1)
    experts = jnp.arange(N_EXPERTS, dtype=jnp.int32)
    onehot = (flat[:, None] == experts[None, :]).astype(jnp.int32)
    rank = jnp.take_along_axis(jnp.cumsum(onehot, axis=0) - onehot, flat[:, None], axis=1)[:, 0]
    counts = jnp.sum(onehot, axis=0)
    padded = (counts + MOE_TM - 1) // MOE_TM * MOE_TM
    pends = jnp.cumsum(padded)
    dest = (pends - padded)[flat] + rank
    tok = jnp.arange(a, dtype=jnp.int32) // TOP_K
    slot_tok = jnp.zeros((nblk * MOE_TM,), jnp.int32).at[dest].set(tok)
    n_used = (pends[-1] // MOE_TM).astype(jnp.int32)
    blk = jnp.minimum(jnp.arange(nblk, dtype=jnp.int32), n_used - 1)
    block_exp = jnp.minimum(jnp.searchsorted(pends, blk * MOE_TM, side='right'), N_EXPERTS - 1).astype(jnp.int32)
    in_use = counts > 0
    ordinal = jnp.cumsum(in_use.astype(jnp.int32)) - 1
    later = lax.cummin(jnp.where(in_use, experts, N_EXPERTS), axis=0, reverse=True)
    nxt = jnp.concatenate([later[1:], jnp.full((1,), N_EXPERTS, jnp.int32)])
    nxt = jnp.where(nxt < N_EXPERTS, nxt, -1)
    return dest.reshape(t, TOP_K), slot_tok, block_exp, ordinal[block_exp], nxt[block_exp], n_used.reshape(1)


def kernel(x_prompt, x_sample, state_gla, state_s5_re, state_s5_im, meta, g_mix, w_in, w_gk2, b_gk, g_gla,
           lam_re, lam_im, log_dt, s5_b_re, s5_b_im, s5_c_re, s5_c_im, d_skip, w_glu, b_glu, w_out, g_ffn,
           w_rg, b_rg, w_re, b_re, w_gate, w_up, w_down, g_final):
    bp, tp, _ = x_prompt.shape
    bs, ts, _ = x_sample.shape
    l = 0
    kw = GLA_KEY_WIDTH
    a0 = 2 * kw + 2 * GLA_WIDTH

    w = w_in[l]
    w_cat = jnp.concatenate(
        [w[:, :a0], w[:, a0 + GLA_RANK:], jnp.pad(w[:, a0:a0 + GLA_RANK], ((0, 0), (0, 128 - GLA_RANK)))],
        axis=1).astype(BF16)
    wgk_hi, wgk_lo = _split2(jnp.pad(w_gk2[l], ((0, 128 - GLA_RANK), (0, 0))))
    g_mix2, b_gk2 = g_mix[l][None], b_gk[l][None]
    mats = _s5_matrices(lam_re[l], lam_im[l], log_dt[l], s5_b_re[l], s5_b_im[l], s5_c_re[l], s5_c_im[l],
                        d_skip[l])
    w_router = jnp.concatenate([w_rg[l], jnp.moveaxis(w_re[l], 0, 1).reshape(D_MODEL, N_EXPERTS)], axis=1)
    w_router = jnp.pad(w_router, ((0, 0), (0, ROUTER_LANES - N_GROUPS - N_EXPERTS)))
    wr_hi, wr_lo = _split2(w_router)
    b_router = jnp.pad(jnp.concatenate([b_rg[l], b_re[l].reshape(-1)]),
                       (0, ROUTER_LANES - N_GROUPS - N_EXPERTS))[None]
    w_glu_b, w_out_b = w_glu[l].astype(BF16), w_out[l].astype(BF16)
    g_gla2, b_glu2, g_ffn2 = g_gla[l].reshape(1, GLA_WIDTH), b_glu[l][None], g_ffn[l][None]

    proj = functools.partial(_in_proj, g_mix=g_mix2, w_cat=w_cat, wgk_hi=wgk_hi, wgk_lo=wgk_lo, b_gk=b_gk2)

    qm, km, vm, _, gm, um = proj(meta, tm=N_META)
    zero_s = jnp.zeros((1, GLA_HEADS, GLA_DK, GLA_DV), F32)
    _, s_meta = _gla(qm, km, vm, gm, zero_s, seq_len=N_META, chunk=N_META)
    zero_h = jnp.zeros((1, S5_NGB, 1, S5_SL), F32)
    _, hm_re, hm_im = _s5_long(um, mats, zero_h, zero_h, seq_len=N_META)

    xp = x_prompt.reshape(bp * tp, D_MODEL)
    qp, kp, vp, rp, gp, up = proj(xp, tm=512)
    op, gla_p = _gla(qp, kp, vp, gp, jnp.broadcast_to(s_meta, (bp,) + s_meta.shape[1:]),
                     seq_len=tp, chunk=GLA_CHUNK)
    y5p, hp_re, hp_im = _s5_long(up, mats, jnp.broadcast_to(hm_re, (bp,) + hm_re.shape[1:]),
                                 jnp.broadcast_to(hm_im, (bp,) + hm_im.shape[1:]), seq_len=tp)

    xs = x_sample.reshape(bs * ts, D_MODEL)
    qs, ks, vs, rs, gs, us = proj(xs, tm=512)
    chunk_s = GLA_CHUNK if ts % GLA_CHUNK == 0 else ts
    os_, gla_s = _gla(qs, ks, vs, gs, state_gla[l], seq_len=ts, chunk=chunk_s)
    y5s, hs_re, hs_im = _s5_short(us, mats, state_s5_re[l].reshape(bs, -1), state_s5_im[l].reshape(bs, -1),
                                  seq_len=ts)

    np_rows, ns_rows = bp * tp, bs * ts
    h1, hn, logits = _post_mix((xp, op, rp, y5p), (xs, os_, rs, y5s), g_gla2, w_glu_b, b_glu2, w_out_b, g_ffn2,
                               wr_hi, wr_lo, b_router, tm=256)

    eid, wts = _route(logits)
    dest, slot_tok, block_exp, block_ord, block_next, n_used = _dispatch(eid)
    out_rows = _moe(block_exp, block_ord, block_next, n_used, slot_tok, hn, w_gate[l], w_up[l], w_down[l])
    fin = functools.partial(_final, dest.reshape(-1), h1, wts[:, 0:1], wts[:, 1:2], g_final[None], out_rows, tm=256)
    y_prompt = fin(n=np_rows, row_off=0)
    y_sample = fin(n=ns_rows, row_off=np_rows)

    return (y_prompt.reshape(bp, tp, D_MODEL), y_sample.reshape(bs, ts, D_MODEL),
            gla_p[None], hp_re.reshape(1, bp, S5_GROUPS, S5_STATE), hp_im.reshape(1, bp, S5_GROUPS, S5_STATE),
            gla_s[None], hs_re.reshape(1, bs, S5_GROUPS, S5_STATE), hs_im.reshape(1, bs, S5_GROUPS, S5_STATE))
```

```python
import functools
import math

import jax
import jax.numpy as jnp
from jax import lax
from jax.experimental import pallas as pl
from jax.experimental.pallas import tpu as pltpu

F32 = jnp.float32
BF16 = jnp.bfloat16

D_MODEL = 2048
N_META = 16
GLA_HEADS = 4
GLA_DK = 128
GLA_DV = 256
GLA_KEY_WIDTH = GLA_HEADS * GLA_DK
GLA_WIDTH = GLA_HEADS * GLA_DV
GLA_RANK = 16
GLA_GATE_NORM = 16.0
GLA_CHUNK = 64
GLA_SUB = 16
GLA_SEQS_PER_STEP = 8
MASKED_EXPONENT = -1e30
S5_WIDTH = 1024
S5_GROUP = 16
S5_GROUPS = 64
S5_STATE = 64
S5_GB = 8
S5_NGB = S5_GROUPS // S5_GB
S5_UL = S5_GB * S5_GROUP
S5_SL = S5_GB * S5_STATE
S5_LT = 2 * S5_SL // 128
S5_UNROLL = 4
N_GROUPS = 4
EXPERTS_PER_GROUP = 8
N_EXPERTS = N_GROUPS * EXPERTS_PER_GROUP
EXPERT_HIDDEN = 512
TOP_K = 2
EPS = 1e-6
ROUTER_LANES = 128
MOE_TM = 256
VMEM_LIMIT = 56 * 1024 * 1024

_dot = functools.partial(jnp.dot, preferred_element_type=F32)


def _split2(x):
    hi = x.astype(BF16)
    lo = (x - hi.astype(F32)).astype(BF16)
    return hi, lo


def _rms(x, g):
    return x * lax.rsqrt(jnp.mean(x * x, axis=-1, keepdims=True) + EPS) * g


def _params(sem):
    return pltpu.CompilerParams(dimension_semantics=sem, vmem_limit_bytes=VMEM_LIMIT)


def _in_proj_kernel(x_ref, g_ref, w_ref, wgk_hi_ref, wgk_lo_ref, bgk_ref,
                    q_ref, k_ref, v_ref, r_ref, gk_ref, u_ref):
    xb = _rms(x_ref[...], g_ref[...]).astype(BF16)
    kw = GLA_KEY_WIDTH
    q_ref[...] = _dot(xb, w_ref[:, 0:kw]) * (GLA_DK ** -0.5)
    k_ref[...] = _dot(xb, w_ref[:, kw:2 * kw])
    v_ref[...] = _dot(xb, w_ref[:, 2 * kw:2 * kw + GLA_WIDTH])
    r_ref[...] = _dot(xb, w_ref[:, 2 * kw + GLA_WIDTH:2 * kw + 2 * GLA_WIDTH])
    c0 = 2 * kw + 2 * GLA_WIDTH
    u_ref[...] = _dot(xb, w_ref[:, c0:c0 + S5_WIDTH])
    a_low = _dot(xb, w_ref[:, c0 + S5_WIDTH:c0 + S5_WIDTH + 128])
    a_hi, a_lo = _split2(a_low)
    z = (_dot(a_hi, wgk_hi_ref[...]) + _dot(a_hi, wgk_lo_ref[...]) + _dot(a_lo, wgk_hi_ref[...])
         + bgk_ref[...])
    gk_ref[...] = (jnp.minimum(z, 0.0) - jnp.log1p(jnp.exp(-jnp.abs(z)))) * (1.0 / GLA_GATE_NORM)


def _in_proj(x, g_mix, w_cat, wgk_hi, wgk_lo, b_gk, tm):
    n = x.shape[0]
    wcols = w_cat.shape[1]
    row = lambda w: pl.BlockSpec((tm, w), lambda i: (i, 0))
    full = lambda a: pl.BlockSpec(a.shape, lambda i: (0,) * a.ndim)
    widths = [GLA_KEY_WIDTH, GLA_KEY_WIDTH, GLA_WIDTH, GLA_WIDTH, GLA_KEY_WIDTH, S5_WIDTH]
    return pl.pallas_call(
        _in_proj_kernel,
        grid=(n // tm,),
        in_specs=[row(D_MODEL), full(g_mix),
                  pl.BlockSpec((D_MODEL, wcols), lambda i: (0, 0), pipeline_mode=pl.Buffered(1)),
                  full(wgk_hi), full(wgk_lo), full(b_gk)],
        out_specs=[row(w) for w in widths],
        out_shape=[jax.ShapeDtypeStruct((n, w), F32) for w in widths],
        compiler_params=_params(("parallel",)),
        name="in_proj",
    )(x, g_mix, w_cat, wgk_hi, wgk_lo, b_gk)


def _gla_chunk(q_ref, k_ref, v_ref, g_ref, o_ref, r0, C, consts, read_state, write_state):
    tril, sub_rows, score_lanes = consts
    sub = min(C, GLA_SUB)
    nsub = C // sub
    g = g_ref[pl.ds(r0, C), :]
    g1 = g.astype(BF16)
    rem = g - g1.astype(F32)
    g2 = rem.astype(BF16)
    g3 = (rem - g2.astype(F32)).astype(BF16)
    b_all = _dot(tril, g1) + _dot(tril, g2) + _dot(tril, g3)
    for h in range(GLA_HEADS):
        ks = slice(h * GLA_DK, (h + 1) * GLA_DK)
        vs = slice(h * GLA_DV, (h + 1) * GLA_DV)
        q = q_ref[pl.ds(r0, C), ks]
        k = k_ref[pl.ds(r0, C), ks]
        b = b_all[:, ks]
        s_prev = read_state(h)
        vb = v_ref[pl.ds(r0, C), vs].astype(BF16)
        o_inter = _dot((q * jnp.exp(b)).astype(BF16), s_prev.astype(BF16))
        blocks = []
        for s in range(nsub):
            lo = s * sub
            bs, qs, ksub = b[lo:lo + sub], q[lo:lo + sub], k[lo:lo + sub]
            acc = o_inter[lo:lo + sub]
            if s > 0:
                anchor = b[lo - 1:lo]
                qd = (qs * jnp.exp(bs - anchor)).astype(BF16)
                kd = (k[:lo] * jnp.exp(anchor - b[:lo])).astype(BF16)
                sc = lax.dot_general(qd, kd, (((1,), (1,)), ((), ())), preferred_element_type=F32)
                acc = acc + _dot(sc.astype(BF16), vb[:lo])
            scores = jnp.zeros((sub, 128), F32)
            for jj in range(sub):
                causal = sub_rows >= jj
                dec = jnp.exp(jnp.where(causal, bs - bs[jj:jj + 1], MASKED_EXPONENT))
                col = jnp.sum((qs * ksub[jj:jj + 1]) * dec, axis=-1, keepdims=True)
                scores = jnp.where(score_lanes == jj, col, scores)
            acc = acc + _dot(scores[:, :sub].astype(BF16), vb[lo:lo + sub])
            blocks.append(acc)
        o_ref[pl.ds(r0, C), vs] = jnp.concatenate(blocks, axis=0) if nsub > 1 else blocks[0]
        b_last = b[C - 1:C]
        kdec = (k * jnp.exp(b_last - b)).astype(BF16)
        upd = lax.dot_general(kdec, vb, (((0,), (0,)), ((), ())), preferred_element_type=F32)
        dcol = jnp.broadcast_to(jnp.exp(b_last), (GLA_DK, GLA_DK)).T
        write_state(h, s_prev * jnp.concatenate([dcol, dcol], axis=1) + upd)


def _gla_consts(C):
    sub = min(C, GLA_SUB)
    ri = lax.broadcasted_iota(jnp.int32, (C, C), 0)
    ci = lax.broadcasted_iota(jnp.int32, (C, C), 1)
    tril = jnp.where(ri >= ci, 1.0, 0.0).astype(BF16)
    sub_rows = lax.broadcasted_iota(jnp.int32, (sub, GLA_DK), 0)
    score_lanes = lax.broadcasted_iota(jnp.int32, (sub, 128), 1)
    return tril, sub_rows, score_lanes


def _gla_long_kernel(q_ref, k_ref, v_ref, g_ref, s0_ref, o_ref, sout_ref, s_scr, *, chunk, n_inner):
    j = pl.program_id(1)
    consts = _gla_consts(chunk)

    @pl.when(j == 0)
    def _init():
        s_scr[...] = s0_ref[0]

    def write_state(h, s):
        s_scr[h] = s

    def chunk_body(c, carry):
        _gla_chunk(q_ref, k_ref, v_ref, g_ref, o_ref, pl.multiple_of(c * chunk, chunk), chunk, consts,
                   lambda h: s_scr[h], write_state)
        return carry

    lax.fori_loop(0, n_inner, chunk_body, 0)

    @pl.when(j == pl.num_programs(1) - 1)
    def _fin():
        sout_ref[0] = s_scr[...]


def _gla_short_kernel(q_ref, k_ref, v_ref, g_ref, s0_ref, o_ref, sout_ref, *, chunk, nb):
    consts = _gla_consts(chunk)

    def seq_body(n, carry):
        def write_state(h, s):
            sout_ref[n, h] = s

        _gla_chunk(q_ref, k_ref, v_ref, g_ref, o_ref, pl.multiple_of(n * chunk, chunk), chunk, consts,
                   lambda h: s0_ref[n, h], write_state)
        return carry

    lax.fori_loop(0, nb, seq_body, 0, unroll=2 if nb % 2 == 0 else 1)


def _gla(q, k, v, g, s0, *, seq_len, chunk):
    nseq = s0.shape[0]
    n = q.shape[0]
    out_shape = [jax.ShapeDtypeStruct((n, GLA_WIDTH), F32),
                 jax.ShapeDtypeStruct((nseq, GLA_HEADS, GLA_DK, GLA_DV), F32)]
    if seq_len == chunk:
        nb = min(nseq, GLA_SEQS_PER_STEP)
        rows = lambda w: pl.BlockSpec((nb * chunk, w), lambda s: (s, 0))
        st = pl.BlockSpec((nb, GLA_HEADS, GLA_DK, GLA_DV), lambda s: (s, 0, 0, 0))
        return pl.pallas_call(
            functools.partial(_gla_short_kernel, chunk=chunk, nb=nb),
            grid=(nseq // nb,),
            in_specs=[rows(GLA_KEY_WIDTH), rows(GLA_KEY_WIDTH), rows(GLA_WIDTH), rows(GLA_KEY_WIDTH), st],
            out_specs=[rows(GLA_WIDTH), st],
            out_shape=out_shape,
            compiler_params=_params(("parallel",)),
            name="gla_short",
        )(q, k, v, g, s0)
    rb = min(seq_len, 4 * chunk)
    nblk = seq_len // rb
    rows = lambda w: pl.BlockSpec((rb, w), lambda s, j: (s * nblk + j, 0))
    st = pl.BlockSpec((1, GLA_HEADS, GLA_DK, GLA_DV), lambda s, j: (s, 0, 0, 0))
    return pl.pallas_call(
        functools.partial(_gla_long_kernel, chunk=chunk, n_inner=rb // chunk),
        grid=(nseq, nblk),
        in_specs=[rows(GLA_KEY_WIDTH), rows(GLA_KEY_WIDTH), rows(GLA_WIDTH), rows(GLA_KEY_WIDTH), st],
        out_specs=[rows(GLA_WIDTH), st],
        out_shape=out_shape,
        scratch_shapes=[pltpu.VMEM((GLA_HEADS, GLA_DK, GLA_DV), F32)],
        compiler_params=_params(("parallel", "arbitrary")),
        name="gla_long",
    )(q, k, v, g, s0)


def _cmul(ar, ai, br, bi):
    return ar * br - ai * bi, ar * bi + ai * br


def _cpow(ar, ai, n):
    res = None
    while n:
        if n & 1:
            res = (ar, ai) if res is None else _cmul(res[0], res[1], ar, ai)
        n >>= 1
        if n:
            ar, ai = _cmul(ar, ai, ar, ai)
    return res


def _s5_segment_len(seq_len):
    s = -(-seq_len // 8)
    s = -(-s // 4) * 4
    return s if (s // 4) % 2 == 1 else s + 4


def _st_store(st_scr, rows, val):
    for t in range(S5_LT):
        st_scr[t, rows, :] = val[:, t * 128:(t + 1) * 128]


def _st_load(st_scr, rows):
    return jnp.concatenate([st_scr[t, rows, :] for t in range(S5_LT)], axis=1)


def _s5_long_kernel(u_ref, bb_ref, cc_ref, dsk_ref, are_ref, aim_ref, h0re_ref, h0im_ref,
                    y_ref, hre_ref, him_ref, st_scr, *, seq_len, seg):
    T, S, SL = seq_len, seg, S5_SL
    rc = min(T, 512)
    for c in range(T // rc):
        sl = slice(c * rc, (c + 1) * rc)
        _st_store(st_scr, sl, _dot(u_ref[sl, :].astype(BF16), bb_ref[0]))
    if 8 * S > T:
        _st_store(st_scr, slice(T, 8 * S), jnp.zeros((8 * S - T, 2 * SL), F32))
    a_re, a_im = are_ref[0], aim_ref[0]
    ar = jnp.broadcast_to(a_re, (8, SL))
    ai = jnp.broadcast_to(a_im, (8, SL))

    def step(i, h, store):
        strand = pl.ds(i, 8, stride=S)
        x = _st_load(st_scr, strand)
        mr, mi = _cmul(ar, ai, h[0], h[1])
        nr, ni = mr + x[:, :SL], mi + x[:, SL:]
        if store:
            _st_store(st_scr, strand, jnp.concatenate([nr, ni], axis=1))
        return nr, ni

    zero = jnp.zeros((8, SL), F32)
    fr, fi = lax.fori_loop(0, S, functools.partial(step, store=False), (zero, zero), unroll=S5_UNROLL)

    as_re, as_im = _cpow(a_re, a_im, S)
    rows = lax.broadcasted_iota(jnp.int32, (8, SL), 0)
    cr, ci = h0re_ref[0, 0], h0im_ref[0, 0]
    car_r, car_i = zero, zero
    for r in range(8):
        car_r = jnp.where(rows == r, cr, car_r)
        car_i = jnp.where(rows == r, ci, car_i)
        if r < 7:
            mr, mi = _cmul(as_re, as_im, cr, ci)
            cr, ci = mr + fr[r:r + 1], mi + fi[r:r + 1]

    lax.fori_loop(0, S, functools.partial(step, store=True), (car_r, car_i), unroll=S5_UNROLL)

    last = _st_load(st_scr, slice(T - 1, T))
    hre_ref[0, 0] = last[:, :SL]
    him_ref[0, 0] = last[:, SL:]
    for c in range(T // rc):
        sl = slice(c * rc, (c + 1) * rc)
        y_ref[sl, :] = _dot(_st_load(st_scr, sl).astype(BF16), cc_ref[0]) + dsk_ref[0] * u_ref[sl, :]


def _s5_long(u, mats, h0_re, h0_im, *, seq_len):
    bb, cc, dsk, a_re, a_im = mats
    nseq = h0_re.shape[0]
    seg = _s5_segment_len(seq_len)
    gb3 = lambda shape: pl.BlockSpec((1,) + shape, lambda s, g: (g, 0, 0))
    st = pl.BlockSpec((1, 1, 1, S5_SL), lambda s, g: (s, g, 0, 0))
    urow = pl.BlockSpec((seq_len, S5_UL), lambda s, g: (s, g))
    st_shape = jax.ShapeDtypeStruct((nseq, S5_NGB, 1, S5_SL), F32)
    return pl.pallas_call(
        functools.partial(_s5_long_kernel, seq_len=seq_len, seg=seg),
        grid=(nseq, S5_NGB),
        in_specs=[urow, gb3((S5_UL, 2 * S5_SL)), gb3((2 * S5_SL, S5_UL)), gb3((1, S5_UL)),
                  gb3((1, S5_SL)), gb3((1, S5_SL)), st, st],
        out_specs=[urow, st, st],
        out_shape=[jax.ShapeDtypeStruct(u.shape, F32), st_shape, st_shape],
        scratch_shapes=[pltpu.VMEM((S5_LT, 8 * seg, 128), F32)],
        compiler_params=_params(("parallel", "parallel")),
        name="s5_long",
    )(u, bb, cc, dsk, a_re, a_im, h0_re, h0_im)


def _s5_short_kernel(u_ref, bb_ref, cc_ref, dsk_ref, are_ref, aim_ref, h0re_ref, h0im_ref,
                     y_ref, hre_ref, him_ref, st_scr, *, seq_len, nseq):
    SL = S5_SL
    _st_store(st_scr, slice(None), _dot(u_ref[...].astype(BF16), bb_ref[0]))
    ar = jnp.broadcast_to(are_ref[0], (nseq, SL))
    ai = jnp.broadcast_to(aim_ref[0], (nseq, SL))
    hr, hi = h0re_ref[...], h0im_ref[...]
    for t in range(seq_len):
        step = pl.ds(t, nseq, stride=seq_len)
        x = _st_load(st_scr, step)
        mr, mi = _cmul(ar, ai, hr, hi)
        hr, hi = mr + x[:, :SL], mi + x[:, SL:]
        _st_store(st_scr, step, jnp.concatenate([hr, hi], axis=1))
    hre_ref[...] = hr
    him_ref[...] = hi
    y_ref[...] = _dot(_st_load(st_scr, slice(None)).astype(BF16), cc_ref[0]) + dsk_ref[0] * u_ref[...]


def _s5_short(u, mats, h0_re, h0_im, *, seq_len):
    bb, cc, dsk, a_re, a_im = mats
    nseq = h0_re.shape[0]
    n = nseq * seq_len
    gb3 = lambda shape: pl.BlockSpec((1,) + shape, lambda g: (g, 0, 0))
    st = pl.BlockSpec((nseq, S5_SL), lambda g: (0, g))
    urow = pl.BlockSpec((n, S5_UL), lambda g: (0, g))
    st_shape = jax.ShapeDtypeStruct(h0_re.shape, F32)
    return pl.pallas_call(
        functools.partial(_s5_short_kernel, seq_len=seq_len, nseq=nseq),
        grid=(S5_NGB,),
        in_specs=[urow, gb3((S5_UL, 2 * S5_SL)), gb3((2 * S5_SL, S5_UL)), gb3((1, S5_UL)),
                  gb3((1, S5_SL)), gb3((1, S5_SL)), st, st],
        out_specs=[urow, st, st],
        out_shape=[jax.ShapeDtypeStruct(u.shape, F32), st_shape, st_shape],
        scratch_shapes=[pltpu.VMEM((S5_LT, n, 128), F32)],
        compiler_params=_params(("parallel",)),
        name="s5_short",
    )(u, bb, cc, dsk, a_re, a_im, h0_re, h0_im)


def _s5_matrices(lam_re, lam_im, log_dt, b_re, b_im, c_re, c_im, d_skip):
    dt = jnp.exp(log_dt)[:, None]
    mag = jnp.exp(lam_re * dt)
    ab_re, ab_im = mag * jnp.cos(lam_im * dt), mag * jnp.sin(lam_im * dt)
    den = lam_re * lam_re + lam_im * lam_im
    f_re = ((ab_re - 1.0) * lam_re + ab_im * lam_im) / den
    f_im = (ab_im * lam_re - (ab_re - 1.0) * lam_im) / den
    bb_re = f_re[..., None] * b_re - f_im[..., None] * b_im
    bb_im = f_re[..., None] * b_im + f_im[..., None] * b_re
    eye = jnp.eye(S5_GB, dtype=F32)

    def in_mat(m):
        m = m.reshape(S5_NGB, S5_GB, S5_STATE, S5_GROUP)
        return jnp.einsum('bgpc,gh->bgchp', m, eye).reshape(S5_NGB, S5_UL, S5_SL)

    def out_mat(m):
        m = m.reshape(S5_NGB, S5_GB, S5_GROUP, S5_STATE)
        return jnp.einsum('bgcp,gh->bgphc', m, eye).reshape(S5_NGB, S5_SL, S5_UL)

    bb = jnp.concatenate([in_mat(bb_re), in_mat(bb_im)], axis=2).astype(BF16)
    cc = jnp.concatenate([out_mat(c_re), out_mat(-c_im)], axis=1).astype(BF16)
    dsk = d_skip.reshape(S5_NGB, 1, S5_UL)
    return bb, cc, dsk, ab_re.reshape(S5_NGB, 1, S5_SL), ab_im.reshape(S5_NGB, 1, S5_SL)


def _post_mix_kernel(xa_ref, oa_ref, ra_ref, ya_ref, xb_ref, ob_ref, rb_ref, yb_ref, *rest, n_first):
    i = pl.program_id(0)

    @pl.when(i < n_first)
    def _first():
        _post_mix_tile(xa_ref, oa_ref, ra_ref, ya_ref, *rest)

    @pl.when(i >= n_first)
    def _second():
        _post_mix_tile(xb_ref, ob_ref, rb_ref, yb_ref, *rest)


def _post_mix_tile(x_ref, o_ref, r_ref, y5_ref, ggla_ref, wglu_ref, bglu_ref, wout_ref, gffn_ref,
                   wr_hi_ref, wr_lo_ref, br_ref, h1_ref, hn_ref, lg_ref):
    o = o_ref[...]
    parts = []
    for h in range(GLA_HEADS):
        oh = o[:, h * GLA_DV:(h + 1) * GLA_DV]
        parts.append(oh * lax.rsqrt(jnp.mean(oh * oh, axis=-1, keepdims=True) + EPS))
    r = r_ref[...]
    o_gla = (jnp.concatenate(parts, axis=1) * ggla_ref[...]) * (r * jax.nn.sigmoid(r))
    y5 = y5_ref[...]
    z = y5 * (0.5 * (1.0 + jnp.tanh(math.sqrt(2.0 / math.pi) * (y5 + 0.044715 * (y5 * y5 * y5)))))
    o_s5 = z * jax.nn.sigmoid(_dot(z.astype(BF16), wglu_ref[...]) + bglu_ref[...])
    att = (_dot(o_gla.astype(BF16), wout_ref[0:GLA_WIDTH, :])
           + _dot(o_s5.astype(BF16), wout_ref[GLA_WIDTH:GLA_WIDTH + S5_WIDTH, :]))
    h1 = x_ref[...] + att
    h1_ref[...] = h1
    hn = _rms(h1, gffn_ref[...])
    hn_ref[...] = _pack_bf16_pairs(hn)
    hn_hi, hn_lo = _split2(hn)
    lg_ref[...] = (_dot(hn_hi, wr_hi_ref[...]) + _dot(hn_hi, wr_lo_ref[...]) + _dot(hn_lo, wr_hi_ref[...])
                   + br_ref[...])


def _post_mix(rows_a, rows_b, g_gla, w_glu, b_glu, w_out, g_ffn, wr_hi, wr_lo, b_r, tm):
    na, nb = rows_a[0].shape[0], rows_b[0].shape[0]
    n_first = na // tm
    widths = [D_MODEL, GLA_WIDTH, GLA_WIDTH, S5_WIDTH]
    spec_a = [pl.BlockSpec((tm, w), lambda i: (jnp.minimum(i, n_first - 1), 0)) for w in widths]
    spec_b = [pl.BlockSpec((tm, w), lambda i: (jnp.maximum(i - n_first, 0), 0)) for w in widths]
    full = lambda a: pl.BlockSpec(a.shape, lambda i: (0,) * a.ndim)
    row = lambda w: pl.BlockSpec((tm, w), lambda i: (i, 0))
    weights = [g_gla, w_glu, b_glu, w_out, g_ffn, wr_hi, wr_lo, b_r]
    n = na + nb
    return pl.pallas_call(
        functools.partial(_post_mix_kernel, n_first=n_first),
        grid=(n // tm,),
        in_specs=spec_a + spec_b + [full(a) for a in weights],
        out_specs=[row(D_MODEL), row(D_MODEL // 2), row(ROUTER_LANES)],
        out_shape=[jax.ShapeDtypeStruct((n, D_MODEL), F32), jax.ShapeDtypeStruct((n, D_MODEL // 2), jnp.uint32),
                   jax.ShapeDtypeStruct((n, ROUTER_LANES), F32)],
        compiler_params=_params(("parallel",)),
        name="post_mix",
    )(*rows_a, *rows_b, *weights)


def _gather_start(idx_ref, idx_base, idx_stride, src_hbm, dst, sem, n):
    def body(r, carry):
        row = idx_ref[idx_base + r * idx_stride]
        pltpu.make_async_copy(src_hbm.at[pl.ds(row, 1)], dst.at[pl.ds(r, 1)], sem).start()
        return carry
    lax.fori_loop(0, n, body, 0, unroll=8)


def _pack_bf16_pairs(x):
    half = x.shape[1] // 2
    bits = lax.bitcast_convert_type(x.astype(BF16).astype(F32), jnp.uint32)
    return bits[:, half:] | (bits[:, :half] >> 16)


def _unpack_bf16_pairs(p):
    lo = lax.bitcast_convert_type(p << 16, F32).astype(BF16)
    hi = lax.bitcast_convert_type(p & jnp.uint32(0xFFFF0000), F32).astype(BF16)
    return jnp.concatenate([lo, hi], axis=1)


def _gather_wait(src_hbm, dst, sem, n):
    pltpu.make_async_copy(src_hbm.at[pl.ds(0, n)], dst.at[pl.ds(0, n)], sem).wait()


def _dispatch_rows_kernel(dest_ref, fill_ref, x_ref, xs_hbm, buf, sem, zsem, *, tm):
    i = pl.program_id(0)
    slot = i % 2

    def row_copy(r, k, s):
        row = dest_ref[(i * tm + r) * TOP_K + k]
        return pltpu.make_async_copy(buf.at[s, pl.ds(r, 1)], xs_hbm.at[pl.ds(row, 1)], sem.at[s])

    def wait_slot(s):
        for _ in range(TOP_K):
            pltpu.make_async_copy(buf.at[s], xs_hbm.at[pl.ds(0, tm)], sem.at[s]).wait()

    @pl.when(i == 0)
    def _zero_fill():
        buf[1] = jnp.zeros(buf.shape[1:], buf.dtype)

        def fill(n, carry):
            @pl.when(fill_ref[n] >= 0)
            def _():
                pltpu.make_async_copy(buf.at[1, pl.ds(0, MOE_TM)], xs_hbm.at[pl.ds(fill_ref[n] * MOE_TM, MOE_TM)],
                                      zsem).start()
            return carry
        lax.fori_loop(0, 2 * N_EXPERTS, fill, 0)

        def drain(n, carry):
            @pl.when(fill_ref[n] >= 0)
            def _():
                pltpu.make_async_copy(buf.at[1, pl.ds(0, MOE_TM)], xs_hbm.at[pl.ds(0, MOE_TM)], zsem).wait()
            return carry
        lax.fori_loop(0, 2 * N_EXPERTS, drain, 0)

    @pl.when(i >= 2)
    def _reuse():
        wait_slot(slot)

    buf[slot] = x_ref[...]

    def issue(r, carry):
        for k in range(TOP_K):
            row_copy(r, k, slot).start()
        return carry
    lax.fori_loop(0, tm, issue, 0, unroll=4)

    @pl.when(i == pl.num_programs(0) - 1)
    def _finish():
        wait_slot(slot)

        @pl.when(i >= 1)
        def _():
            wait_slot(1 - slot)


def _dispatch_rows(dest, fill_blocks, hn, n_rows, tm):
    n = hn.shape[0]
    assert tm >= MOE_TM
    return pl.pallas_call(
        functools.partial(_dispatch_rows_kernel, tm=tm),
        grid_spec=pltpu.PrefetchScalarGridSpec(
            num_scalar_prefetch=2,
            grid=(n // tm,),
            in_specs=[pl.BlockSpec((tm, D_MODEL // 2), lambda i, d, lb: (i, 0))],
            out_specs=pl.BlockSpec(memory_space=pl.ANY),
            scratch_shapes=[pltpu.VMEM((2, tm, D_MODEL // 2), jnp.uint32), pltpu.SemaphoreType.DMA((2,)),
                            pltpu.SemaphoreType.DMA],
        ),
        out_shape=jax.ShapeDtypeStruct((n_rows, D_MODEL // 2), jnp.uint32),
        compiler_params=_params(("arbitrary",)),
        name="dispatch_rows",
    )(dest, fill_blocks, hn)


def _moe_kernel(bexp_ref, eord_ref, next_ref, nused_ref, x_ref, wg_hbm, wu_hbm, wd_hbm, o_ref,
                wg_st, wu_st, wd_st, wsem, wg_bf, wu_bf, wd_bf):
    b = pl.program_id(0)
    nu = nused_ref[0]
    e = bexp_ref[b]
    new_expert = jnp.logical_or(b == 0, bexp_ref[jnp.maximum(b - 1, 0)] != e)

    def weight_copies(expert, slot):
        return [pltpu.make_async_copy(src.at[expert], dst.at[slot], wsem.at[slot])
                for src, dst in ((wg_hbm, wg_st), (wu_hbm, wu_st), (wd_hbm, wd_st))]

    @pl.when(b == 0)
    def _prologue():
        for c in weight_copies(e, 0):
            c.start()

    @pl.when(jnp.logical_and(b < nu, new_expert))
    def _new_expert():
        slot = eord_ref[b] % 2
        for c in weight_copies(e, slot):
            c.wait()
        nxt = next_ref[b]

        @pl.when(nxt >= 0)
        def _prefetch():
            for c in weight_copies(nxt, 1 - slot):
                c.start()

        wg_bf[...] = wg_st[slot].astype(BF16)
        wu_bf[...] = wu_st[slot].astype(BF16)
        wd_bf[...] = wd_st[slot].astype(BF16)

    @pl.when(b < nu)
    def _run():
        x = _unpack_bf16_pairs(x_ref[...])
        gate = _dot(x, wg_bf[...])
        up = _dot(x, wu_bf[...])
        hid = ((gate * jax.nn.sigmoid(gate)) * up).astype(BF16)
        o_ref[...] = _dot(hid, wd_bf[...])

    @pl.when(b >= nu)
    def _skip():
        o_ref[...] = jnp.zeros(o_ref.shape, o_ref.dtype)


def _moe(block_exp, block_ord, block_next, n_used, xs, w_gate, w_up, w_down):
    nblk = xs.shape[0] // MOE_TM
    any_spec = pl.BlockSpec(memory_space=pl.ANY)
    return pl.pallas_call(
        _moe_kernel,
        grid_spec=pltpu.PrefetchScalarGridSpec(
            num_scalar_prefetch=4,
            grid=(nblk,),
            in_specs=[pl.BlockSpec((MOE_TM, D_MODEL // 2), lambda b, be, eo, nx, nu: (jnp.minimum(b, nu[0] - 1), 0)),
                      any_spec, any_spec, any_spec],
            out_specs=pl.BlockSpec((MOE_TM, D_MODEL), lambda b, *_: (b, 0)),
            scratch_shapes=[pltpu.VMEM((2, D_MODEL, EXPERT_HIDDEN), F32), pltpu.VMEM((2, D_MODEL, EXPERT_HIDDEN), F32),
                            pltpu.VMEM((2, EXPERT_HIDDEN, D_MODEL), F32), pltpu.SemaphoreType.DMA((2,)),
                            pltpu.VMEM((D_MODEL, EXPERT_HIDDEN), BF16), pltpu.VMEM((D_MODEL, EXPERT_HIDDEN), BF16),
                            pltpu.VMEM((EXPERT_HIDDEN, D_MODEL), BF16)],
        ),
        out_shape=jax.ShapeDtypeStruct((nblk * MOE_TM, D_MODEL), F32),
        compiler_params=_params(("arbitrary",)),
        name="moe",
    )(block_exp, block_ord, block_next, n_used, xs, w_gate, w_up, w_down)


def _final_kernel(dest_ref, h1_ref, wa_ref, wb_ref, g_ref, rows_hbm, y_ref, ybuf, sem, *, tm, tok_off):
    i = pl.program_id(0)

    def start(tile):
        slot = tile % 2
        base = (tok_off + tile * tm) * TOP_K
        for k in range(TOP_K):
            _gather_start(dest_ref, base + k, TOP_K, rows_hbm, ybuf.at[slot, k], sem.at[slot], tm)

    @pl.when(i == 0)
    def _first():
        start(i)

    @pl.when(i + 1 < pl.num_programs(0))
    def _next():
        start(i + 1)

    slot = i % 2
    for k in range(TOP_K):
        _gather_wait(rows_hbm, ybuf.at[slot, k], sem.at[slot], tm)
    moe = ybuf[slot, 0] * wa_ref[...] + ybuf[slot, 1] * wb_ref[...]
    y_ref[...] = _rms(h1_ref[...] + moe, g_ref[...])


def _final(dest, h1, wa, wb, g_final, rows, *, tm, n, row_off):
    off = row_off // tm
    row = lambda w: pl.BlockSpec((tm, w), lambda i, d: (i + off, 0))
    return pl.pallas_call(
        functools.partial(_final_kernel, tm=tm, tok_off=row_off),
        grid_spec=pltpu.PrefetchScalarGridSpec(
            num_scalar_prefetch=1,
            grid=(n // tm,),
            in_specs=[row(D_MODEL), row(1), row(1), pl.BlockSpec((1, D_MODEL), lambda i, d: (0, 0)),
                      pl.BlockSpec(memory_space=pl.ANY)],
            out_specs=pl.BlockSpec((tm, D_MODEL), lambda i, d: (i, 0)),
            scratch_shapes=[pltpu.VMEM((2, TOP_K, tm, D_MODEL), F32), pltpu.SemaphoreType.DMA((2,))],
        ),
        out_shape=jax.ShapeDtypeStruct((n, D_MODEL), F32),
        compiler_params=_params(("arbitrary",)),
        name="final",
    )(dest, h1, wa, wb, g_final, rows)


def _route(logits):
    lg = logits[:, :N_GROUPS]
    le = logits[:, N_GROUPS:N_GROUPS + N_EXPERTS].reshape(-1, N_GROUPS, EXPERTS_PER_GROUP)
    pg = jax.nn.softmax(lg, axis=-1)
    gsel = jnp.argmax(lg, axis=-1)
    pg_sel = jnp.take_along_axis(pg, gsel[:, None], axis=-1)[:, 0]
    le_sel = jnp.take_along_axis(le, gsel[:, None, None], axis=1)[:, 0]
    vals, idx = lax.top_k(le_sel, TOP_K)
    pe = jax.nn.softmax(vals, axis=-1)
    eid = (gsel[:, None] * EXPERTS_PER_GROUP + idx).astype(jnp.int32)
    return eid, pg_sel[:, None] * pe


def _dispatch(eid):
    t = eid.shape[0]
    a = t * TOP_K
    assert a % MOE_TM == 0
    nblk = a // MOE_TM + N_EXPERTS
    flat = eid.reshape(-1)
    experts = jnp.arange(N_EXPERTS, dtype=jnp.int32)
    onehot = (flat[:, None] == experts[None, :]).astype(jnp.int32)
    rank = jnp.take_along_axis(jnp.cumsum(onehot, axis=0) - onehot, flat[:, None], axis=1)[:, 0]
    counts = jnp.sum(onehot, axis=0)
    padded = (counts + MOE_TM - 1) // MOE_TM * MOE_TM
    pends = jnp.cumsum(padded)
    dest = (pends - padded)[flat] + rank
    n_used = (pends[-1] // MOE_TM).astype(jnp.int32)
    blk = jnp.minimum(jnp.arange(nblk, dtype=jnp.int32), n_used - 1)
    block_exp = jnp.minimum(jnp.searchsorted(pends, blk * MOE_TM, side='right'), N_EXPERTS - 1).astype(jnp.int32)
    in_use = counts > 0
    ordinal = jnp.cumsum(in_use.astype(jnp.int32)) - 1
    later = lax.cummin(jnp.where(in_use, experts, N_EXPERTS), axis=0, reverse=True)
    nxt = jnp.concatenate([later[1:], jnp.full((1,), N_EXPERTS, jnp.int32)])
    nxt = jnp.where(nxt < N_EXPERTS, nxt, -1)
    last_block = jnp.where(in_use, pends // MOE_TM - 1, -1)
    unused = n_used + experts
    fill_blocks = jnp.concatenate([last_block, jnp.where(unused < nblk, unused, -1)]).astype(jnp.int32)
    return (dest.astype(jnp.int32), nblk * MOE_TM, block_exp, ordinal[block_exp], nxt[block_exp], fill_blocks,
            n_used.reshape(1))


def kernel(x_prompt, x_sample, state_gla, state_s5_re, state_s5_im, meta, g_mix, w_in, w_gk2, b_gk, g_gla,
           lam_re, lam_im, log_dt, s5_b_re, s5_b_im, s5_c_re, s5_c_im, d_skip, w_glu, b_glu, w_out, g_ffn,
           w_rg, b_rg, w_re, b_re, w_gate, w_up, w_down, g_final):
    bp, tp, _ = x_prompt.shape
    bs, ts, _ = x_sample.shape
    l = 0
    kw = GLA_KEY_WIDTH
    a0 = 2 * kw + 2 * GLA_WIDTH

    w = w_in[l]
    w_cat = jnp.concatenate(
        [w[:, :a0], w[:, a0 + GLA_RANK:], jnp.pad(w[:, a0:a0 + GLA_RANK], ((0, 0), (0, 128 - GLA_RANK)))],
        axis=1).astype(BF16)
    wgk_hi, wgk_lo = _split2(jnp.pad(w_gk2[l], ((0, 128 - GLA_RANK), (0, 0))))
    g_mix2, b_gk2 = g_mix[l][None], b_gk[l][None]
    mats = _s5_matrices(lam_re[l], lam_im[l], log_dt[l], s5_b_re[l], s5_b_im[l], s5_c_re[l], s5_c_im[l],
                        d_skip[l])
    w_router = jnp.concatenate([w_rg[l], jnp.moveaxis(w_re[l], 0, 1).reshape(D_MODEL, N_EXPERTS)], axis=1)
    w_router = jnp.pad(w_router, ((0, 0), (0, ROUTER_LANES - N_GROUPS - N_EXPERTS)))
    wr_hi, wr_lo = _split2(w_router)
    b_router = jnp.pad(jnp.concatenate([b_rg[l], b_re[l].reshape(-1)]),
                       (0, ROUTER_LANES - N_GROUPS - N_EXPERTS))[None]
    w_glu_b, w_out_b = w_glu[l].astype(BF16), w_out[l].astype(BF16)
    g_gla2, b_glu2, g_ffn2 = g_gla[l].reshape(1, GLA_WIDTH), b_glu[l][None], g_ffn[l][None]

    proj = functools.partial(_in_proj, g_mix=g_mix2, w_cat=w_cat, wgk_hi=wgk_hi, wgk_lo=wgk_lo, b_gk=b_gk2)

    qm, km, vm, _, gm, um = proj(meta, tm=N_META)
    zero_s = jnp.zeros((1, GLA_HEADS, GLA_DK, GLA_DV), F32)
    _, s_meta = _gla(qm, km, vm, gm, zero_s, seq_len=N_META, chunk=N_META)
    zero_h = jnp.zeros((1, S5_NGB, 1, S5_SL), F32)
    _, hm_re, hm_im = _s5_long(um, mats, zero_h, zero_h, seq_len=N_META)

    xp = x_prompt.reshape(bp * tp, D_MODEL)
    qp, kp, vp, rp, gp, up = proj(xp, tm=512)
    op, gla_p = _gla(qp, kp, vp, gp, jnp.broadcast_to(s_meta, (bp,) + s_meta.shape[1:]),
                     seq_len=tp, chunk=GLA_CHUNK)
    y5p, hp_re, hp_im = _s5_long(up, mats, jnp.broadcast_to(hm_re, (bp,) + hm_re.shape[1:]),
                                 jnp.broadcast_to(hm_im, (bp,) + hm_im.shape[1:]), seq_len=tp)

    xs = x_sample.reshape(bs * ts, D_MODEL)
    qs, ks, vs, rs, gs, us = proj(xs, tm=512)
    chunk_s = GLA_CHUNK if ts % GLA_CHUNK == 0 else ts
    os_, gla_s = _gla(qs, ks, vs, gs, state_gla[l], seq_len=ts, chunk=chunk_s)
    y5s, hs_re, hs_im = _s5_short(us, mats, state_s5_re[l].reshape(bs, -1), state_s5_im[l].reshape(bs, -1),
                                  seq_len=ts)

    np_rows, ns_rows = bp * tp, bs * ts
    h1, hn, logits = _post_mix((xp, op, rp, y5p), (xs, os_, rs, y5s), g_gla2, w_glu_b, b_glu2, w_out_b, g_ffn2,
                               wr_hi, wr_lo, b_router, tm=256)

    eid, wts = _route(logits)
    dest, n_sorted, block_exp, block_ord, block_next, fill_blocks, n_used = _dispatch(eid)
    xs_rows = _dispatch_rows(dest, fill_blocks, hn, n_sorted, tm=256)
    out_rows = _moe(block_exp, block_ord, block_next, n_used, xs_rows, w_gate[l], w_up[l], w_down[l])
    fin = functools.partial(_final, dest, h1, wts[:, 0:1], wts[:, 1:2], g_final[None], out_rows, tm=256)
    y_prompt = fin(n=np_rows, row_off=0)
    y_sample = fin(n=ns_rows, row_off=np_rows)

    return (y_prompt.reshape(bp, tp, D_MODEL), y_sample.reshape(bs, ts, D_MODEL),
            gla_p[None], hp_re.reshape(1, bp, S5_GROUPS, S5_STATE), hp_im.reshape(1, bp, S5_GROUPS, S5_STATE),
            gla_s[None], hs_re.reshape(1, bs, S5_GROUPS, S5_STATE), hs_im.reshape(1, bs, S5_GROUPS, S5_STATE))
```

```python
import functools
import math

import jax
import jax.numpy as jnp
from jax import lax
from jax.experimental import pallas as pl
from jax.experimental.pallas import tpu as pltpu

F32 = jnp.float32
BF16 = jnp.bfloat16

D_MODEL = 2048
N_META = 16
GLA_HEADS = 4
GLA_DK = 128
GLA_DV = 256
GLA_KEY_WIDTH = GLA_HEADS * GLA_DK
GLA_WIDTH = GLA_HEADS * GLA_DV
GLA_RANK = 16
GLA_GATE_NORM = 16.0
GLA_CHUNK = 64
GLA_SUB = 16
GLA_SEQS_PER_STEP = 8
MASKED_EXPONENT = -1e30
S5_WIDTH = 1024
S5_GROUP = 16
S5_GROUPS = 64
S5_STATE = 64
S5_GB = 8
S5_NGB = S5_GROUPS // S5_GB
S5_UL = S5_GB * S5_GROUP
S5_SL = S5_GB * S5_STATE
S5_LT = 2 * S5_SL // 128
S5_UNROLL = 4
N_GROUPS = 4
EXPERTS_PER_GROUP = 8
N_EXPERTS = N_GROUPS * EXPERTS_PER_GROUP
EXPERT_HIDDEN = 512
TOP_K = 2
EPS = 1e-6
ROUTER_LANES = 128
ROUTE_ROWS = -(-(N_GROUPS + N_EXPERTS) // 8) * 8
MOE_TM = 256
VMEM_LIMIT = 56 * 1024 * 1024

_dot = functools.partial(jnp.dot, preferred_element_type=F32)


def _split2(x):
    hi = x.astype(BF16)
    lo = (x - hi.astype(F32)).astype(BF16)
    return hi, lo


def _rms(x, g):
    return x * lax.rsqrt(jnp.mean(x * x, axis=-1, keepdims=True) + EPS) * g


def _params(sem):
    return pltpu.CompilerParams(dimension_semantics=sem, vmem_limit_bytes=VMEM_LIMIT)


def _in_proj_kernel(x_ref, g_ref, w_ref, wgk_hi_ref, wgk_lo_ref, bgk_ref,
                    q_ref, k_ref, v_ref, r_ref, gk_ref, u_ref):
    xb = _rms(x_ref[...], g_ref[...]).astype(BF16)
    kw = GLA_KEY_WIDTH
    q_ref[...] = _dot(xb, w_ref[:, 0:kw]) * (GLA_DK ** -0.5)
    k_ref[...] = _dot(xb, w_ref[:, kw:2 * kw])
    v_ref[...] = _dot(xb, w_ref[:, 2 * kw:2 * kw + GLA_WIDTH])
    r_ref[...] = _dot(xb, w_ref[:, 2 * kw + GLA_WIDTH:2 * kw + 2 * GLA_WIDTH])
    c0 = 2 * kw + 2 * GLA_WIDTH
    tail = _dot(xb, w_ref[:, c0:])
    u_ref[...] = tail[:, GLA_RANK:GLA_RANK + S5_WIDTH]
    a_low = tail[:, :128]
    a_hi, a_lo = _split2(a_low)
    z = (_dot(a_hi, wgk_hi_ref[...]) + _dot(a_hi, wgk_lo_ref[...]) + _dot(a_lo, wgk_hi_ref[...])
         + bgk_ref[...])
    gk_ref[...] = (jnp.minimum(z, 0.0) - jnp.log1p(jnp.exp(-jnp.abs(z)))) * (1.0 / GLA_GATE_NORM)


def _in_proj(x, g_mix, w_cat, wgk_hi, wgk_lo, b_gk, tm):
    n = x.shape[0]
    wcols = w_cat.shape[1]
    row = lambda w: pl.BlockSpec((tm, w), lambda i: (i, 0))
    full = lambda a: pl.BlockSpec(a.shape, lambda i: (0,) * a.ndim)
    widths = [GLA_KEY_WIDTH, GLA_KEY_WIDTH, GLA_WIDTH, GLA_WIDTH, GLA_KEY_WIDTH, S5_WIDTH]
    return pl.pallas_call(
        _in_proj_kernel,
        grid=(n // tm,),
        in_specs=[row(D_MODEL), full(g_mix),
                  pl.BlockSpec((D_MODEL, wcols), lambda i: (0, 0), pipeline_mode=pl.Buffered(1)),
                  full(wgk_hi), full(wgk_lo), full(b_gk)],
        out_specs=[row(w) for w in widths],
        out_shape=[jax.ShapeDtypeStruct((n, w), F32) for w in widths],
        compiler_params=_params(("parallel",)),
        name="in_proj",
    )(x, g_mix, w_cat, wgk_hi, wgk_lo, b_gk)


def _gla_chunk(q_ref, k_ref, v_ref, g_ref, o_ref, r0, C, consts, read_state, write_state):
    tril, piece_rows, piece_lanes = consts
    sub = min(C, GLA_SUB)
    nsub = C // sub
    g = g_ref[pl.ds(r0, C), :]
    g1 = g.astype(BF16)
    rem = g - g1.astype(F32)
    g2 = rem.astype(BF16)
    g3 = (rem - g2.astype(F32)).astype(BF16)
    b_all = _dot(tril, g1) + _dot(tril, g2) + _dot(tril, g3)
    for h in range(GLA_HEADS):
        ks = slice(h * GLA_DK, (h + 1) * GLA_DK)
        vs = slice(h * GLA_DV, (h + 1) * GLA_DV)
        q = q_ref[pl.ds(r0, C), ks]
        k = k_ref[pl.ds(r0, C), ks]
        b = b_all[:, ks]
        s_prev = read_state(h)
        vb = v_ref[pl.ds(r0, C), vs].astype(BF16)
        o_inter = _dot((q * jnp.exp(b)).astype(BF16), s_prev.astype(BF16))
        blocks = []
        for s in range(nsub):
            lo = s * sub
            bs, qs, ksub = b[lo:lo + sub], q[lo:lo + sub], k[lo:lo + sub]
            acc = o_inter[lo:lo + sub]
            if s > 0:
                anchor = b[lo - 1:lo]
                qd = (qs * jnp.exp(bs - anchor)).astype(BF16)
                kd = (k[:lo] * jnp.exp(anchor - b[:lo])).astype(BF16)
                sc = lax.dot_general(qd, kd, (((1,), (1,)), ((), ())), preferred_element_type=F32)
                acc = acc + _dot(sc.astype(BF16), vb[:lo])
            pieces = [jnp.zeros((8, 128), F32) for _ in range(sub // 8)]
            for jj in range(sub):
                for p in range(jj // 8, sub // 8):
                    r8 = slice(8 * p, 8 * p + 8)
                    diff = bs[r8] - bs[jj:jj + 1]
                    if 8 * p < jj:
                        diff = jnp.where(piece_rows + 8 * p >= jj, diff, MASKED_EXPONENT)
                    col = jnp.sum((qs[r8] * ksub[jj:jj + 1]) * jnp.exp(diff), axis=-1, keepdims=True)
                    pieces[p] = jnp.where(piece_lanes == jj, col, pieces[p])
            scores = jnp.concatenate(pieces, axis=0) if len(pieces) > 1 else pieces[0]
            acc = acc + _dot(scores[:, :sub].astype(BF16), vb[lo:lo + sub])
            blocks.append(acc)
        o_ref[pl.ds(r0, C), vs] = jnp.concatenate(blocks, axis=0) if nsub > 1 else blocks[0]
        b_last = b[C - 1:C]
        kdec = (k * jnp.exp(b_last - b)).astype(BF16)
        upd = lax.dot_general(kdec, vb, (((0,), (0,)), ((), ())), preferred_element_type=F32)
        dcol = jnp.broadcast_to(jnp.exp(b_last), (GLA_DK, GLA_DK)).T
        write_state(h, s_prev * jnp.concatenate([dcol, dcol], axis=1) + upd)


def _gla_consts(C):
    sub = min(C, GLA_SUB)
    ri = lax.broadcasted_iota(jnp.int32, (C, C), 0)
    ci = lax.broadcasted_iota(jnp.int32, (C, C), 1)
    tril = jnp.where(ri >= ci, 1.0, 0.0).astype(BF16)
    piece_rows = lax.broadcasted_iota(jnp.int32, (8, GLA_DK), 0)
    piece_lanes = lax.broadcasted_iota(jnp.int32, (8, 128), 1)
    return tril, piece_rows, piece_lanes


def _gla_long_kernel(q_ref, k_ref, v_ref, g_ref, s0_ref, o_ref, sout_ref, s_scr, *, chunk, n_inner):
    j = pl.program_id(1)
    consts = _gla_consts(chunk)

    @pl.when(j == 0)
    def _init():
        s_scr[...] = s0_ref[0]

    def write_state(h, s):
        s_scr[h] = s

    def chunk_body(c, carry):
        _gla_chunk(q_ref, k_ref, v_ref, g_ref, o_ref, pl.multiple_of(c * chunk, chunk), chunk, consts,
                   lambda h: s_scr[h], write_state)
        return carry

    lax.fori_loop(0, n_inner, chunk_body, 0)

    @pl.when(j == pl.num_programs(1) - 1)
    def _fin():
        sout_ref[0] = s_scr[...]


def _gla_short_kernel(q_ref, k_ref, v_ref, g_ref, s0_ref, o_ref, sout_ref, *, chunk, nb):
    consts = _gla_consts(chunk)

    def seq_body(n, carry):
        def write_state(h, s):
            sout_ref[n, h] = s

        _gla_chunk(q_ref, k_ref, v_ref, g_ref, o_ref, pl.multiple_of(n * chunk, chunk), chunk, consts,
                   lambda h: s0_ref[n, h], write_state)
        return carry

    lax.fori_loop(0, nb, seq_body, 0, unroll=2 if nb % 2 == 0 else 1)


def _gla(q, k, v, g, s0, *, seq_len, chunk):
    nseq = s0.shape[0]
    n = q.shape[0]
    out_shape = [jax.ShapeDtypeStruct((n, GLA_WIDTH), F32),
                 jax.ShapeDtypeStruct((nseq, GLA_HEADS, GLA_DK, GLA_DV), F32)]
    if seq_len == chunk:
        nb = min(nseq, GLA_SEQS_PER_STEP)
        rows = lambda w: pl.BlockSpec((nb * chunk, w), lambda s: (s, 0))
        st = pl.BlockSpec((nb, GLA_HEADS, GLA_DK, GLA_DV), lambda s: (s, 0, 0, 0))
        return pl.pallas_call(
            functools.partial(_gla_short_kernel, chunk=chunk, nb=nb),
            grid=(nseq // nb,),
            in_specs=[rows(GLA_KEY_WIDTH), rows(GLA_KEY_WIDTH), rows(GLA_WIDTH), rows(GLA_KEY_WIDTH), st],
            out_specs=[rows(GLA_WIDTH), st],
            out_shape=out_shape,
            compiler_params=_params(("parallel",)),
            name="gla_short",
        )(q, k, v, g, s0)
    rb = min(seq_len, 4 * chunk)
    nblk = seq_len // rb
    rows = lambda w: pl.BlockSpec((rb, w), lambda s, j: (s * nblk + j, 0))
    st = pl.BlockSpec((1, GLA_HEADS, GLA_DK, GLA_DV), lambda s, j: (s, 0, 0, 0))
    return pl.pallas_call(
        functools.partial(_gla_long_kernel, chunk=chunk, n_inner=rb // chunk),
        grid=(nseq, nblk),
        in_specs=[rows(GLA_KEY_WIDTH), rows(GLA_KEY_WIDTH), rows(GLA_WIDTH), rows(GLA_KEY_WIDTH), st],
        out_specs=[rows(GLA_WIDTH), st],
        out_shape=out_shape,
        scratch_shapes=[pltpu.VMEM((GLA_HEADS, GLA_DK, GLA_DV), F32)],
        compiler_params=_params(("parallel", "arbitrary")),
        name="gla_long",
    )(q, k, v, g, s0)


def _cmul(ar, ai, br, bi):
    return ar * br - ai * bi, ar * bi + ai * br


def _cpow(ar, ai, n):
    res = None
    while n:
        if n & 1:
            res = (ar, ai) if res is None else _cmul(res[0], res[1], ar, ai)
        n >>= 1
        if n:
            ar, ai = _cmul(ar, ai, ar, ai)
    return res


def _s5_segment_len(seq_len):
    s = -(-seq_len // 8)
    s = -(-s // 4) * 4
    return s if (s // 4) % 2 == 1 else s + 4


def _st_store(st_scr, rows, val):
    for t in range(S5_LT):
        st_scr[t, rows, :] = val[:, t * 128:(t + 1) * 128]


def _st_load(st_scr, rows):
    return jnp.concatenate([st_scr[t, rows, :] for t in range(S5_LT)], axis=1)


def _s5_long_kernel(u_ref, bb_ref, cc_ref, dsk_ref, are_ref, aim_ref, h0re_ref, h0im_ref,
                    y_ref, hre_ref, him_ref, st_scr, *, seq_len, seg):
    T, S, SL = seq_len, seg, S5_SL
    rc = min(T, 512)
    for c in range(T // rc):
        sl = slice(c * rc, (c + 1) * rc)
        _st_store(st_scr, sl, _dot(u_ref[sl, :].astype(BF16), bb_ref[0]))
    if 8 * S > T:
        _st_store(st_scr, slice(T, 8 * S), jnp.zeros((8 * S - T, 2 * SL), F32))
    a_re, a_im = are_ref[0], aim_ref[0]
    ar = jnp.broadcast_to(a_re, (8, SL))
    ai = jnp.broadcast_to(a_im, (8, SL))

    def step(i, h, store):
        strand = pl.ds(i, 8, stride=S)
        x = _st_load(st_scr, strand)
        mr, mi = _cmul(ar, ai, h[0], h[1])
        nr, ni = mr + x[:, :SL], mi + x[:, SL:]
        if store:
            _st_store(st_scr, strand, jnp.concatenate([nr, ni], axis=1))
        return nr, ni

    zero = jnp.zeros((8, SL), F32)
    fr, fi = lax.fori_loop(0, S, functools.partial(step, store=False), (zero, zero), unroll=S5_UNROLL)

    as_re, as_im = _cpow(a_re, a_im, S)
    rows = lax.broadcasted_iota(jnp.int32, (8, SL), 0)
    cr, ci = h0re_ref[0, 0], h0im_ref[0, 0]
    car_r, car_i = zero, zero
    for r in range(8):
        car_r = jnp.where(rows == r, cr, car_r)
        car_i = jnp.where(rows == r, ci, car_i)
        if r < 7:
            mr, mi = _cmul(as_re, as_im, cr, ci)
            cr, ci = mr + fr[r:r + 1], mi + fi[r:r + 1]

    lax.fori_loop(0, S, functools.partial(step, store=True), (car_r, car_i), unroll=S5_UNROLL)

    last = _st_load(st_scr, slice(T - 1, T))
    hre_ref[0, 0] = last[:, :SL]
    him_ref[0, 0] = last[:, SL:]
    for c in range(T // rc):
        sl = slice(c * rc, (c + 1) * rc)
        y_ref[sl, :] = _dot(_st_load(st_scr, sl).astype(BF16), cc_ref[0]) + dsk_ref[0] * u_ref[sl, :]


def _s5_long(u, mats, h0_re, h0_im, *, seq_len):
    bb, cc, dsk, a_re, a_im = mats
    nseq = h0_re.shape[0]
    seg = _s5_segment_len(seq_len)
    gb3 = lambda shape: pl.BlockSpec((1,) + shape, lambda s, g: (g, 0, 0))
    st = pl.BlockSpec((1, 1, 1, S5_SL), lambda s, g: (s, g, 0, 0))
    urow = pl.BlockSpec((seq_len, S5_UL), lambda s, g: (s, g))
    st_shape = jax.ShapeDtypeStruct((nseq, S5_NGB, 1, S5_SL), F32)
    return pl.pallas_call(
        functools.partial(_s5_long_kernel, seq_len=seq_len, seg=seg),
        grid=(nseq, S5_NGB),
        in_specs=[urow, gb3((S5_UL, 2 * S5_SL)), gb3((2 * S5_SL, S5_UL)), gb3((1, S5_UL)),
                  gb3((1, S5_SL)), gb3((1, S5_SL)), st, st],
        out_specs=[urow, st, st],
        out_shape=[jax.ShapeDtypeStruct(u.shape, F32), st_shape, st_shape],
        scratch_shapes=[pltpu.VMEM((S5_LT, 8 * seg, 128), F32)],
        compiler_params=_params(("parallel", "parallel")),
        name="s5_long",
    )(u, bb, cc, dsk, a_re, a_im, h0_re, h0_im)


def _s5_short_kernel(u_ref, bb_ref, cc_ref, dsk_ref, are_ref, aim_ref, h0re_ref, h0im_ref,
                     y_ref, hre_ref, him_ref, st_scr, *, seq_len, nseq):
    SL = S5_SL
    _st_store(st_scr, slice(None), _dot(u_ref[...].astype(BF16), bb_ref[0]))
    ar = jnp.broadcast_to(are_ref[0], (nseq, SL))
    ai = jnp.broadcast_to(aim_ref[0], (nseq, SL))
    hr, hi = h0re_ref[...], h0im_ref[...]
    for t in range(seq_len):
        step = pl.ds(t, nseq, stride=seq_len)
        x = _st_load(st_scr, step)
        mr, mi = _cmul(ar, ai, hr, hi)
        hr, hi = mr + x[:, :SL], mi + x[:, SL:]
        _st_store(st_scr, step, jnp.concatenate([hr, hi], axis=1))
    hre_ref[...] = hr
    him_ref[...] = hi
    y_ref[...] = _dot(_st_load(st_scr, slice(None)).astype(BF16), cc_ref[0]) + dsk_ref[0] * u_ref[...]


def _s5_short(u, mats, h0_re, h0_im, *, seq_len):
    bb, cc, dsk, a_re, a_im = mats
    nseq = h0_re.shape[0]
    n = nseq * seq_len
    gb3 = lambda shape: pl.BlockSpec((1,) + shape, lambda g: (g, 0, 0))
    st = pl.BlockSpec((nseq, S5_SL), lambda g: (0, g))
    urow = pl.BlockSpec((n, S5_UL), lambda g: (0, g))
    st_shape = jax.ShapeDtypeStruct(h0_re.shape, F32)
    return pl.pallas_call(
        functools.partial(_s5_short_kernel, seq_len=seq_len, nseq=nseq),
        grid=(S5_NGB,),
        in_specs=[urow, gb3((S5_UL, 2 * S5_SL)), gb3((2 * S5_SL, S5_UL)), gb3((1, S5_UL)),
                  gb3((1, S5_SL)), gb3((1, S5_SL)), st, st],
        out_specs=[urow, st, st],
        out_shape=[jax.ShapeDtypeStruct(u.shape, F32), st_shape, st_shape],
        scratch_shapes=[pltpu.VMEM((S5_LT, n, 128), F32)],
        compiler_params=_params(("parallel",)),
        name="s5_short",
    )(u, bb, cc, dsk, a_re, a_im, h0_re, h0_im)


def _s5_matrices(lam_re, lam_im, log_dt, b_re, b_im, c_re, c_im, d_skip):
    dt = jnp.exp(log_dt)[:, None]
    mag = jnp.exp(lam_re * dt)
    ab_re, ab_im = mag * jnp.cos(lam_im * dt), mag * jnp.sin(lam_im * dt)
    den = lam_re * lam_re + lam_im * lam_im
    f_re = ((ab_re - 1.0) * lam_re + ab_im * lam_im) / den
    f_im = (ab_im * lam_re - (ab_re - 1.0) * lam_im) / den
    bb_re = f_re[..., None] * b_re - f_im[..., None] * b_im
    bb_im = f_re[..., None] * b_im + f_im[..., None] * b_re
    eye = jnp.eye(S5_GB, dtype=F32)

    def in_mat(m):
        m = m.reshape(S5_NGB, S5_GB, S5_STATE, S5_GROUP)
        return jnp.einsum('bgpc,gh->bgchp', m, eye).reshape(S5_NGB, S5_UL, S5_SL)

    def out_mat(m):
        m = m.reshape(S5_NGB, S5_GB, S5_GROUP, S5_STATE)
        return jnp.einsum('bgcp,gh->bgphc', m, eye).reshape(S5_NGB, S5_SL, S5_UL)

    bb = jnp.concatenate([in_mat(bb_re), in_mat(bb_im)], axis=2).astype(BF16)
    cc = jnp.concatenate([out_mat(c_re), out_mat(-c_im)], axis=1).astype(BF16)
    dsk = d_skip.reshape(S5_NGB, 1, S5_UL)
    return bb, cc, dsk, ab_re.reshape(S5_NGB, 1, S5_SL), ab_im.reshape(S5_NGB, 1, S5_SL)


def _post_mix_kernel(xa_ref, oa_ref, ra_ref, ya_ref, xb_ref, ob_ref, rb_ref, yb_ref, *rest, n_first):
    i = pl.program_id(0)

    @pl.when(i < n_first)
    def _first():
        _post_mix_tile(xa_ref, oa_ref, ra_ref, ya_ref, *rest)

    @pl.when(i >= n_first)
    def _second():
        _post_mix_tile(xb_ref, ob_ref, rb_ref, yb_ref, *rest)


def _post_mix_tile(x_ref, o_ref, r_ref, y5_ref, ggla_ref, wglu_ref, bglu_ref, wout_ref, gffn_ref,
                   wr_ref, br_ref, h1_ref, hn_ref, lg_ref):
    o = o_ref[...]
    parts = []
    for h in range(GLA_HEADS):
        oh = o[:, h * GLA_DV:(h + 1) * GLA_DV]
        parts.append(oh * lax.rsqrt(jnp.mean(oh * oh, axis=-1, keepdims=True) + EPS))
    r = r_ref[...]
    o_gla = (jnp.concatenate(parts, axis=1) * ggla_ref[...]) * (r * jax.nn.sigmoid(r))
    y5 = y5_ref[...]
    z = y5 * (0.5 * (1.0 + jnp.tanh(math.sqrt(2.0 / math.pi) * (y5 + 0.044715 * (y5 * y5 * y5)))))
    o_s5 = z * jax.nn.sigmoid(_dot(z.astype(BF16), wglu_ref[...]) + bglu_ref[...])
    att = (_dot(o_gla.astype(BF16), wout_ref[0:GLA_WIDTH, :])
           + _dot(o_s5.astype(BF16), wout_ref[GLA_WIDTH:GLA_WIDTH + S5_WIDTH, :]))
    h1 = x_ref[...] + att
    h1_ref[...] = h1
    hn = _rms(h1, gffn_ref[...])
    hn_ref[...] = _pack_bf16_pairs(hn)
    hn_hi, hn_lo = _split2(hn)
    both = _dot(hn_hi, wr_ref[...])
    logits = (both[:, :ROUTER_LANES] + both[:, ROUTER_LANES:] + _dot(hn_lo, wr_ref[:, :ROUTER_LANES])
              + br_ref[...])
    lg_ref[...] = _route_tile(logits)


def _route_tile(logits):
    lt = logits.T[:ROUTE_ROWS]
    row = lax.broadcasted_iota(jnp.int32, lt.shape, 0)
    ninf = float('-inf')

    def first_max(vals):
        m = jnp.max(vals, axis=0, keepdims=True)
        return m, jnp.min(jnp.where(vals == m, row, ROUTE_ROWS), axis=0, keepdims=True)

    is_group = row < N_GROUPS
    gmax, gsel = first_max(jnp.where(is_group, lt, ninf))
    p_group = 1.0 / jnp.sum(jnp.where(is_group, jnp.exp(lt - gmax), 0.0), axis=0, keepdims=True)
    first = N_GROUPS + gsel * EXPERTS_PER_GROUP
    in_group = jnp.logical_and(row >= first, row < first + EXPERTS_PER_GROUP)
    cand = jnp.where(in_group, lt, ninf)
    m1, i1 = first_max(cand)
    m2, i2 = first_max(jnp.where(row == i1, ninf, cand))
    t = jnp.exp(m2 - m1)
    p1 = 1.0 / (1.0 + t)
    out_row = lax.broadcasted_iota(jnp.int32, (8, lt.shape[1]), 0)
    out = jnp.where(out_row == 0, (i1 - N_GROUPS).astype(F32), 0.0)
    out = jnp.where(out_row == 1, (i2 - N_GROUPS).astype(F32), out)
    out = jnp.where(out_row == 2, p_group * p1, out)
    return jnp.where(out_row == 3, p_group * (t * p1), out)


def _post_mix(rows_a, rows_b, g_gla, w_glu, b_glu, w_out, g_ffn, wr_pair, b_r, tm):
    na, nb = rows_a[0].shape[0], rows_b[0].shape[0]
    n_first = na // tm
    widths = [D_MODEL, GLA_WIDTH, GLA_WIDTH, S5_WIDTH]
    spec_a = [pl.BlockSpec((tm, w), lambda i: (jnp.minimum(i, n_first - 1), 0)) for w in widths]
    spec_b = [pl.BlockSpec((tm, w), lambda i: (jnp.maximum(i - n_first, 0), 0)) for w in widths]
    full = lambda a: pl.BlockSpec(a.shape, lambda i: (0,) * a.ndim)
    row = lambda w: pl.BlockSpec((tm, w), lambda i: (i, 0))
    weights = [g_gla, w_glu, b_glu, w_out, g_ffn, wr_pair, b_r]
    n = na + nb
    return pl.pallas_call(
        functools.partial(_post_mix_kernel, n_first=n_first),
        grid=(n // tm,),
        in_specs=spec_a + spec_b + [full(a) for a in weights],
        out_specs=[row(D_MODEL), row(D_MODEL // 2), pl.BlockSpec((8, tm), lambda i: (0, i))],
        out_shape=[jax.ShapeDtypeStruct((n, D_MODEL), F32), jax.ShapeDtypeStruct((n, D_MODEL // 2), jnp.uint32),
                   jax.ShapeDtypeStruct((8, n), F32)],
        compiler_params=_params(("parallel",)),
        name="post_mix",
    )(*rows_a, *rows_b, *weights)


def _gather_start(idx_ref, idx_base, idx_stride, src_hbm, dst, sem, n):
    def body(r, carry):
        row = idx_ref[idx_base + r * idx_stride]
        pltpu.make_async_copy(src_hbm.at[pl.ds(row, 1)], dst.at[pl.ds(r, 1)], sem).start()
        return carry
    lax.fori_loop(0, n, body, 0, unroll=8)


def _pack_bf16_pairs(x):
    half = x.shape[1] // 2
    bits = lax.bitcast_convert_type(x.astype(BF16).astype(F32), jnp.uint32)
    return bits[:, half:] | (bits[:, :half] >> 16)


def _unpack_bf16_pairs(p):
    lo = lax.bitcast_convert_type(p << 16, F32).astype(BF16)
    hi = lax.bitcast_convert_type(p & jnp.uint32(0xFFFF0000), F32).astype(BF16)
    return jnp.concatenate([lo, hi], axis=1)


def _gather_wait(src_hbm, dst, sem, n):
    pltpu.make_async_copy(src_hbm.at[pl.ds(0, n)], dst.at[pl.ds(0, n)], sem).wait()


def _dispatch_rows_kernel(dest_ref, fill_ref, x_ref, xs_hbm, buf, sem, zsem, *, tm):
    i = pl.program_id(0)
    slot = i % 2

    def row_copy(r, k, s):
        row = dest_ref[(i * tm + r) * TOP_K + k]
        return pltpu.make_async_copy(buf.at[s, pl.ds(r, 1)], xs_hbm.at[pl.ds(row, 1)], sem.at[s])

    def wait_slot(s):
        for _ in range(TOP_K):
            pltpu.make_async_copy(buf.at[s], xs_hbm.at[pl.ds(0, tm)], sem.at[s]).wait()

    @pl.when(i == 0)
    def _zero_fill():
        buf[1] = jnp.zeros(buf.shape[1:], buf.dtype)

        def fill(n, carry):
            @pl.when(fill_ref[n] >= 0)
            def _():
                pltpu.make_async_copy(buf.at[1, pl.ds(0, MOE_TM)], xs_hbm.at[pl.ds(fill_ref[n] * MOE_TM, MOE_TM)],
                                      zsem).start()
            return carry
        lax.fori_loop(0, 2 * N_EXPERTS, fill, 0)

        def drain(n, carry):
            @pl.when(fill_ref[n] >= 0)
            def _():
                pltpu.make_async_copy(buf.at[1, pl.ds(0, MOE_TM)], xs_hbm.at[pl.ds(0, MOE_TM)], zsem).wait()
            return carry
        lax.fori_loop(0, 2 * N_EXPERTS, drain, 0)

    @pl.when(i >= 2)
    def _reuse():
        wait_slot(slot)

    buf[slot] = x_ref[...]

    def issue(r, carry):
        for k in range(TOP_K):
            row_copy(r, k, slot).start()
        return carry
    lax.fori_loop(0, tm, issue, 0, unroll=4)

    @pl.when(i == pl.num_programs(0) - 1)
    def _finish():
        wait_slot(slot)

        @pl.when(i >= 1)
        def _():
            wait_slot(1 - slot)


def _dispatch_rows(dest, fill_blocks, hn, n_rows, tm):
    n = hn.shape[0]
    assert tm >= MOE_TM
    return pl.pallas_call(
        functools.partial(_dispatch_rows_kernel, tm=tm),
        grid_spec=pltpu.PrefetchScalarGridSpec(
            num_scalar_prefetch=2,
            grid=(n // tm,),
            in_specs=[pl.BlockSpec((tm, D_MODEL // 2), lambda i, d, lb: (i, 0))],
            out_specs=pl.BlockSpec(memory_space=pl.ANY),
            scratch_shapes=[pltpu.VMEM((2, tm, D_MODEL // 2), jnp.uint32), pltpu.SemaphoreType.DMA((2,)),
                            pltpu.SemaphoreType.DMA],
        ),
        out_shape=jax.ShapeDtypeStruct((n_rows, D_MODEL // 2), jnp.uint32),
        compiler_params=_params(("arbitrary",)),
        name="dispatch_rows",
    )(dest, fill_blocks, hn)


def _moe_kernel(bexp_ref, eord_ref, next_ref, nused_ref, x_ref, wg_hbm, wu_hbm, wd_hbm, o_ref,
                wg_st, wu_st, wd_st, wsem, wg_bf, wu_bf, wd_bf):
    b = pl.program_id(0)
    nu = nused_ref[0]
    e = bexp_ref[b]
    new_expert = jnp.logical_or(b == 0, bexp_ref[jnp.maximum(b - 1, 0)] != e)

    def weight_copies(expert, slot):
        return [pltpu.make_async_copy(src.at[expert], dst.at[slot], wsem.at[slot])
                for src, dst in ((wg_hbm, wg_st), (wu_hbm, wu_st), (wd_hbm, wd_st))]

    @pl.when(b == 0)
    def _prologue():
        for c in weight_copies(e, 0):
            c.start()

    @pl.when(jnp.logical_and(b < nu, new_expert))
    def _new_expert():
        slot = eord_ref[b] % 2
        for c in weight_copies(e, slot):
            c.wait()
        nxt = next_ref[b]

        @pl.when(nxt >= 0)
        def _prefetch():
            for c in weight_copies(nxt, 1 - slot):
                c.start()

        wg_bf[...] = wg_st[slot].astype(BF16)
        wu_bf[...] = wu_st[slot].astype(BF16)
        wd_bf[...] = wd_st[slot].astype(BF16)

    @pl.when(b < nu)
    def _run():
        x = _unpack_bf16_pairs(x_ref[...])
        gate = _dot(x, wg_bf[...])
        up = _dot(x, wu_bf[...])
        hid = ((gate * jax.nn.sigmoid(gate)) * up).astype(BF16)
        o_ref[...] = _dot(hid, wd_bf[...])

    @pl.when(b >= nu)
    def _skip():
        o_ref[...] = jnp.zeros(o_ref.shape, o_ref.dtype)


def _moe(block_exp, block_ord, block_next, n_used, xs, w_gate, w_up, w_down):
    nblk = xs.shape[0] // MOE_TM
    any_spec = pl.BlockSpec(memory_space=pl.ANY)
    return pl.pallas_call(
        _moe_kernel,
        grid_spec=pltpu.PrefetchScalarGridSpec(
            num_scalar_prefetch=4,
            grid=(nblk,),
            in_specs=[pl.BlockSpec((MOE_TM, D_MODEL // 2), lambda b, be, eo, nx, nu: (jnp.minimum(b, nu[0] - 1), 0)),
                      any_spec, any_spec, any_spec],
            out_specs=pl.BlockSpec((MOE_TM, D_MODEL), lambda b, *_: (b, 0)),
            scratch_shapes=[pltpu.VMEM((2, D_MODEL, EXPERT_HIDDEN), F32), pltpu.VMEM((2, D_MODEL, EXPERT_HIDDEN), F32),
                            pltpu.VMEM((2, EXPERT_HIDDEN, D_MODEL), F32), pltpu.SemaphoreType.DMA((2,)),
                            pltpu.VMEM((D_MODEL, EXPERT_HIDDEN), BF16), pltpu.VMEM((D_MODEL, EXPERT_HIDDEN), BF16),
                            pltpu.VMEM((EXPERT_HIDDEN, D_MODEL), BF16)],
        ),
        out_shape=jax.ShapeDtypeStruct((nblk * MOE_TM, D_MODEL), F32),
        compiler_params=_params(("arbitrary",)),
        name="moe",
    )(block_exp, block_ord, block_next, n_used, xs, w_gate, w_up, w_down)


def _final_kernel(dest_ref, h1_ref, wa_ref, wb_ref, g_ref, rows_hbm, y_ref, ybuf, sem, *, tm, tok_off):
    i = pl.program_id(0)

    def start(tile):
        slot = tile % 2
        base = (tok_off + tile * tm) * TOP_K
        for k in range(TOP_K):
            _gather_start(dest_ref, base + k, TOP_K, rows_hbm, ybuf.at[slot, k], sem.at[slot], tm)

    @pl.when(i == 0)
    def _first():
        start(i)

    @pl.when(i + 1 < pl.num_programs(0))
    def _next():
        start(i + 1)

    slot = i % 2
    for k in range(TOP_K):
        _gather_wait(rows_hbm, ybuf.at[slot, k], sem.at[slot], tm)
    moe = ybuf[slot, 0] * wa_ref[...] + ybuf[slot, 1] * wb_ref[...]
    y_ref[...] = _rms(h1_ref[...] + moe, g_ref[...])


def _final(dest, h1, wa, wb, g_final, rows, *, tm, n, row_off):
    off = row_off // tm
    row = lambda w: pl.BlockSpec((tm, w), lambda i, d: (i + off, 0))
    return pl.pallas_call(
        functools.partial(_final_kernel, tm=tm, tok_off=row_off),
        grid_spec=pltpu.PrefetchScalarGridSpec(
            num_scalar_prefetch=1,
            grid=(n // tm,),
            in_specs=[row(D_MODEL), row(1), row(1), pl.BlockSpec((1, D_MODEL), lambda i, d: (0, 0)),
                      pl.BlockSpec(memory_space=pl.ANY)],
            out_specs=pl.BlockSpec((tm, D_MODEL), lambda i, d: (i, 0)),
            scratch_shapes=[pltpu.VMEM((2, TOP_K, tm, D_MODEL), F32), pltpu.SemaphoreType.DMA((2,))],
        ),
        out_shape=jax.ShapeDtypeStruct((n, D_MODEL), F32),
        compiler_params=_params(("arbitrary",)),
        name="final",
    )(dest, h1, wa, wb, g_final, rows)


def _dispatch(eid):
    t = eid.shape[0]
    a = t * TOP_K
    assert a % MOE_TM == 0
    nblk = a // MOE_TM + N_EXPERTS
    flat = eid.reshape(-1)
    experts = jnp.arange(N_EXPERTS, dtype=jnp.int32)
    onehot = (flat[:, None] == experts[None, :]).astype(F32).reshape(a // MOE_TM, MOE_TM, N_EXPERTS)
    strict_lower = jnp.tril(jnp.ones((MOE_TM, MOE_TM), F32), -1)
    within = jnp.einsum('ij,bjk->bik', strict_lower, onehot)
    totals = jnp.sum(onehot, axis=1)
    before = jnp.cumsum(totals, axis=0) - totals
    rank = jnp.sum((within + before[:, None, :]) * onehot, axis=-1).reshape(-1).astype(jnp.int32)
    counts = jnp.sum(totals, axis=0).astype(jnp.int32)
    padded = (counts + MOE_TM - 1) // MOE_TM * MOE_TM
    pends = jnp.cumsum(padded)
    dest = (pends - padded)[flat] + rank
    n_used = (pends[-1] // MOE_TM).astype(jnp.int32)
    blk = jnp.minimum(jnp.arange(nblk, dtype=jnp.int32), n_used - 1)
    block_exp = jnp.minimum(jnp.searchsorted(pends, blk * MOE_TM, side='right'), N_EXPERTS - 1).astype(jnp.int32)
    in_use = counts > 0
    ordinal = jnp.cumsum(in_use.astype(jnp.int32)) - 1
    later = lax.cummin(jnp.where(in_use, experts, N_EXPERTS), axis=0, reverse=True)
    nxt = jnp.concatenate([later[1:], jnp.full((1,), N_EXPERTS, jnp.int32)])
    nxt = jnp.where(nxt < N_EXPERTS, nxt, -1)
    last_block = jnp.where(in_use, pends // MOE_TM - 1, -1)
    unused = n_used + experts
    fill_blocks = jnp.concatenate([last_block, jnp.where(unused < nblk, unused, -1)]).astype(jnp.int32)
    return (dest.astype(jnp.int32), nblk * MOE_TM, block_exp, ordinal[block_exp], nxt[block_exp], fill_blocks,
            n_used.reshape(1))


def kernel(x_prompt, x_sample, state_gla, state_s5_re, state_s5_im, meta, g_mix, w_in, w_gk2, b_gk, g_gla,
           lam_re, lam_im, log_dt, s5_b_re, s5_b_im, s5_c_re, s5_c_im, d_skip, w_glu, b_glu, w_out, g_ffn,
           w_rg, b_rg, w_re, b_re, w_gate, w_up, w_down, g_final):
    bp, tp, _ = x_prompt.shape
    bs, ts, _ = x_sample.shape
    l = 0

    w_cat = w_in[l].astype(BF16)
    wgk_hi, wgk_lo = _split2(jnp.pad(w_gk2[l], ((0, 128 - GLA_RANK), (0, 0))))
    g_mix2, b_gk2 = g_mix[l][None], b_gk[l][None]
    mats = _s5_matrices(lam_re[l], lam_im[l], log_dt[l], s5_b_re[l], s5_b_im[l], s5_c_re[l], s5_c_im[l],
                        d_skip[l])
    w_router = jnp.concatenate([w_rg[l], jnp.moveaxis(w_re[l], 0, 1).reshape(D_MODEL, N_EXPERTS)], axis=1)
    w_router = jnp.pad(w_router, ((0, 0), (0, ROUTER_LANES - N_GROUPS - N_EXPERTS)))
    wr_pair = jnp.concatenate(_split2(w_router), axis=1)
    b_router = jnp.pad(jnp.concatenate([b_rg[l], b_re[l].reshape(-1)]),
                       (0, ROUTER_LANES - N_GROUPS - N_EXPERTS))[None]
    w_glu_b, w_out_b = w_glu[l].astype(BF16), w_out[l].astype(BF16)
    g_gla2, b_glu2, g_ffn2 = g_gla[l].reshape(1, GLA_WIDTH), b_glu[l][None], g_ffn[l][None]

    proj = functools.partial(_in_proj, g_mix=g_mix2, w_cat=w_cat, wgk_hi=wgk_hi, wgk_lo=wgk_lo, b_gk=b_gk2)

    qm, km, vm, _, gm, um = proj(meta, tm=N_META)
    zero_s = jnp.zeros((1, GLA_HEADS, GLA_DK, GLA_DV), F32)
    _, s_meta = _gla(qm, km, vm, gm, zero_s, seq_len=N_META, chunk=N_META)
    zero_h = jnp.zeros((1, S5_NGB, 1, S5_SL), F32)
    _, hm_re, hm_im = _s5_long(um, mats, zero_h, zero_h, seq_len=N_META)

    xp = x_prompt.reshape(bp * tp, D_MODEL)
    qp, kp, vp, rp, gp, up = proj(xp, tm=512)
    op, gla_p = _gla(qp, kp, vp, gp, jnp.broadcast_to(s_meta, (bp,) + s_meta.shape[1:]),
                     seq_len=tp, chunk=GLA_CHUNK)
    y5p, hp_re, hp_im = _s5_long(up, mats, jnp.broadcast_to(hm_re, (bp,) + hm_re.shape[1:]),
                                 jnp.broadcast_to(hm_im, (bp,) + hm_im.shape[1:]), seq_len=tp)

    xs = x_sample.reshape(bs * ts, D_MODEL)
    qs, ks, vs, rs, gs, us = proj(xs, tm=512)
    chunk_s = GLA_CHUNK if ts % GLA_CHUNK == 0 else ts
    os_, gla_s = _gla(qs, ks, vs, gs, state_gla[l], seq_len=ts, chunk=chunk_s)
    y5s, hs_re, hs_im = _s5_short(us, mats, state_s5_re[l].reshape(bs, -1), state_s5_im[l].reshape(bs, -1),
                                  seq_len=ts)

    np_rows, ns_rows = bp * tp, bs * ts
    h1, hn, route = _post_mix((xp, op, rp, y5p), (xs, os_, rs, y5s), g_gla2, w_glu_b, b_glu2, w_out_b, g_ffn2,
                               wr_pair, b_router, tm=256)

    eid = route[:TOP_K].T.astype(jnp.int32)
    wts = route[TOP_K:2 * TOP_K].T
    dest, n_sorted, block_exp, block_ord, block_next, fill_blocks, n_used = _dispatch(eid)
    xs_rows = _dispatch_rows(dest, fill_blocks, hn, n_sorted, tm=MOE_TM)
    out_rows = _moe(block_exp, block_ord, block_next, n_used, xs_rows, w_gate[l], w_up[l], w_down[l])
    fin = functools.partial(_final, dest, h1, wts[:, 0:1], wts[:, 1:2], g_final[None], out_rows, tm=256)
    y_prompt = fin(n=np_rows, row_off=0)
    y_sample = fin(n=ns_rows, row_off=np_rows)

    return (y_prompt.reshape(bp, tp, D_MODEL), y_sample.reshape(bs, ts, D_MODEL),
            gla_p[None], hp_re.reshape(1, bp, S5_GROUPS, S5_STATE), hp_im.reshape(1, bp, S5_GROUPS, S5_STATE),
            gla_s[None], hs_re.reshape(1, bs, S5_GROUPS, S5_STATE), hs_im.reshape(1, bs, S5_GROUPS, S5_STATE))
```

```python
import functools
import math

import jax
import jax.numpy as jnp
from jax import lax
from jax.experimental import pallas as pl
from jax.experimental.pallas import tpu as pltpu

F32 = jnp.float32
BF16 = jnp.bfloat16

D_MODEL = 2048
N_META = 16
GLA_HEADS = 4
GLA_DK = 128
GLA_DV = 256
GLA_KEY_WIDTH = GLA_HEADS * GLA_DK
GLA_WIDTH = GLA_HEADS * GLA_DV
GLA_RANK = 16
GLA_GATE_NORM = 16.0
GLA_CHUNK = 64
GLA_SUB = 16
GLA_SEQS_PER_STEP = 8
MASKED_EXPONENT = -1e30
S5_WIDTH = 1024
S5_GROUP = 16
S5_GROUPS = 64
S5_STATE = 64
S5_GB = 8
S5_NGB = S5_GROUPS // S5_GB
S5_UL = S5_GB * S5_GROUP
S5_SL = S5_GB * S5_STATE
S5_LT = 2 * S5_SL // 128
S5_UNROLL = 4
N_GROUPS = 4
EXPERTS_PER_GROUP = 8
N_EXPERTS = N_GROUPS * EXPERTS_PER_GROUP
EXPERT_HIDDEN = 512
TOP_K = 2
EPS = 1e-6
ROUTER_LANES = 128
ROUTE_ROWS = -(-(N_GROUPS + N_EXPERTS) // 8) * 8
MOE_TM = 256
VMEM_LIMIT = 56 * 1024 * 1024

_dot = functools.partial(jnp.dot, preferred_element_type=F32)


def _split2(x):
    hi = x.astype(BF16)
    lo = (x - hi.astype(F32)).astype(BF16)
    return hi, lo


def _rms(x, g):
    return x * lax.rsqrt(jnp.mean(x * x, axis=-1, keepdims=True) + EPS) * g


def _params(sem):
    return pltpu.CompilerParams(dimension_semantics=sem, vmem_limit_bytes=VMEM_LIMIT)


def _in_proj_kernel(x_ref, g_ref, w_ref, wgk_hi_ref, wgk_lo_ref, bgk_ref,
                    q_ref, k_ref, v_ref, r_ref, gk_ref, u_ref):
    xb = _rms(x_ref[...], g_ref[...]).astype(BF16)
    kw = GLA_KEY_WIDTH
    q_ref[...] = _dot(xb, w_ref[:, 0:kw]) * (GLA_DK ** -0.5)
    k_ref[...] = _dot(xb, w_ref[:, kw:2 * kw])
    v_ref[...] = _dot(xb, w_ref[:, 2 * kw:2 * kw + GLA_WIDTH])
    r_ref[...] = _dot(xb, w_ref[:, 2 * kw + GLA_WIDTH:2 * kw + 2 * GLA_WIDTH])
    c0 = 2 * kw + 2 * GLA_WIDTH
    tail = _dot(xb, w_ref[:, c0:])
    u_ref[...] = tail[:, GLA_RANK:GLA_RANK + S5_WIDTH]
    a_low = tail[:, :128]
    a_hi, a_lo = _split2(a_low)
    z = (_dot(a_hi, wgk_hi_ref[...]) + _dot(a_hi, wgk_lo_ref[...]) + _dot(a_lo, wgk_hi_ref[...])
         + bgk_ref[...])
    gk_ref[...] = (jnp.minimum(z, 0.0) - jnp.log1p(jnp.exp(-jnp.abs(z)))) * (1.0 / GLA_GATE_NORM)


def _in_proj(x, g_mix, w_cat, wgk_hi, wgk_lo, b_gk, tm):
    n = x.shape[0]
    wcols = w_cat.shape[1]
    row = lambda w: pl.BlockSpec((tm, w), lambda i: (i, 0))
    full = lambda a: pl.BlockSpec(a.shape, lambda i: (0,) * a.ndim)
    widths = [GLA_KEY_WIDTH, GLA_KEY_WIDTH, GLA_WIDTH, GLA_WIDTH, GLA_KEY_WIDTH, S5_WIDTH]
    return pl.pallas_call(
        _in_proj_kernel,
        grid=(n // tm,),
        in_specs=[row(D_MODEL), full(g_mix),
                  pl.BlockSpec((D_MODEL, wcols), lambda i: (0, 0), pipeline_mode=pl.Buffered(1)),
                  full(wgk_hi), full(wgk_lo), full(b_gk)],
        out_specs=[row(w) for w in widths],
        out_shape=[jax.ShapeDtypeStruct((n, w), F32) for w in widths],
        compiler_params=_params(("parallel",)),
        name="in_proj",
    )(x, g_mix, w_cat, wgk_hi, wgk_lo, b_gk)


def _gla_chunk(q_ref, k_ref, v_ref, g_ref, o_ref, r0, C, consts, read_state, write_state):
    tril, piece_rows, piece_lanes = consts
    sub = min(C, GLA_SUB)
    nsub = C // sub
    g = g_ref[pl.ds(r0, C), :]
    g1 = g.astype(BF16)
    rem = g - g1.astype(F32)
    g2 = rem.astype(BF16)
    g3 = (rem - g2.astype(F32)).astype(BF16)
    b_all = _dot(tril, g1) + _dot(tril, g2) + _dot(tril, g3)
    for h in range(GLA_HEADS):
        ks = slice(h * GLA_DK, (h + 1) * GLA_DK)
        vs = slice(h * GLA_DV, (h + 1) * GLA_DV)
        q = q_ref[pl.ds(r0, C), ks]
        k = k_ref[pl.ds(r0, C), ks]
        b = b_all[:, ks]
        s_prev = read_state(h)
        vb = v_ref[pl.ds(r0, C), vs].astype(BF16)
        o_inter = _dot((q * jnp.exp(b)).astype(BF16), s_prev.astype(BF16))
        blocks = []
        for s in range(nsub):
            lo = s * sub
            bs, qs, ksub = b[lo:lo + sub], q[lo:lo + sub], k[lo:lo + sub]
            acc = o_inter[lo:lo + sub]
            if s > 0:
                anchor = b[lo - 1:lo]
                qd = (qs * jnp.exp(bs - anchor)).astype(BF16)
                kd = (k[:lo] * jnp.exp(anchor - b[:lo])).astype(BF16)
                sc = lax.dot_general(qd, kd, (((1,), (1,)), ((), ())), preferred_element_type=F32)
                acc = acc + _dot(sc.astype(BF16), vb[:lo])
            pieces = [jnp.zeros((8, 128), F32) for _ in range(sub // 8)]
            for jj in range(sub):
                for p in range(jj // 8, sub // 8):
                    r8 = slice(8 * p, 8 * p + 8)
                    diff = bs[r8] - bs[jj:jj + 1]
                    if 8 * p < jj:
                        diff = jnp.where(piece_rows + 8 * p >= jj, diff, MASKED_EXPONENT)
                    col = jnp.sum((qs[r8] * ksub[jj:jj + 1]) * jnp.exp(diff), axis=-1, keepdims=True)
                    pieces[p] = jnp.where(piece_lanes == jj, col, pieces[p])
            scores = jnp.concatenate(pieces, axis=0) if len(pieces) > 1 else pieces[0]
            acc = acc + _dot(scores[:, :sub].astype(BF16), vb[lo:lo + sub])
            blocks.append(acc)
        o_ref[pl.ds(r0, C), vs] = jnp.concatenate(blocks, axis=0) if nsub > 1 else blocks[0]
        b_last = b[C - 1:C]
        kdec = (k * jnp.exp(b_last - b)).astype(BF16)
        upd = lax.dot_general(kdec, vb, (((0,), (0,)), ((), ())), preferred_element_type=F32)
        dcol = jnp.broadcast_to(jnp.exp(b_last), (GLA_DK, GLA_DK)).T
        write_state(h, s_prev * jnp.concatenate([dcol, dcol], axis=1) + upd)


def _gla_consts(C):
    sub = min(C, GLA_SUB)
    ri = lax.broadcasted_iota(jnp.int32, (C, C), 0)
    ci = lax.broadcasted_iota(jnp.int32, (C, C), 1)
    tril = jnp.where(ri >= ci, 1.0, 0.0).astype(BF16)
    piece_rows = lax.broadcasted_iota(jnp.int32, (8, GLA_DK), 0)
    piece_lanes = lax.broadcasted_iota(jnp.int32, (8, 128), 1)
    return tril, piece_rows, piece_lanes


def _gla_long_kernel(q_ref, k_ref, v_ref, g_ref, s0_ref, o_ref, sout_ref, s_scr, *, chunk, n_inner):
    j = pl.program_id(1)
    consts = _gla_consts(chunk)

    @pl.when(j == 0)
    def _init():
        s_scr[...] = s0_ref[0]

    def write_state(h, s):
        s_scr[h] = s

    def chunk_body(c, carry):
        _gla_chunk(q_ref, k_ref, v_ref, g_ref, o_ref, pl.multiple_of(c * chunk, chunk), chunk, consts,
                   lambda h: s_scr[h], write_state)
        return carry

    lax.fori_loop(0, n_inner, chunk_body, 0)

    @pl.when(j == pl.num_programs(1) - 1)
    def _fin():
        sout_ref[0] = s_scr[...]


def _gla_short_kernel(q_ref, k_ref, v_ref, g_ref, s0_ref, o_ref, sout_ref, *, chunk, nb):
    consts = _gla_consts(chunk)

    def seq_body(n, carry):
        def write_state(h, s):
            sout_ref[n, h] = s

        _gla_chunk(q_ref, k_ref, v_ref, g_ref, o_ref, pl.multiple_of(n * chunk, chunk), chunk, consts,
                   lambda h: s0_ref[n, h], write_state)
        return carry

    lax.fori_loop(0, nb, seq_body, 0, unroll=2 if nb % 2 == 0 else 1)


def _gla(q, k, v, g, s0, *, seq_len, chunk):
    nseq = s0.shape[0]
    n = q.shape[0]
    out_shape = [jax.ShapeDtypeStruct((n, GLA_WIDTH), F32),
                 jax.ShapeDtypeStruct((nseq, GLA_HEADS, GLA_DK, GLA_DV), F32)]
    if seq_len == chunk:
        nb = min(nseq, GLA_SEQS_PER_STEP)
        rows = lambda w: pl.BlockSpec((nb * chunk, w), lambda s: (s, 0))
        st = pl.BlockSpec((nb, GLA_HEADS, GLA_DK, GLA_DV), lambda s: (s, 0, 0, 0))
        return pl.pallas_call(
            functools.partial(_gla_short_kernel, chunk=chunk, nb=nb),
            grid=(nseq // nb,),
            in_specs=[rows(GLA_KEY_WIDTH), rows(GLA_KEY_WIDTH), rows(GLA_WIDTH), rows(GLA_KEY_WIDTH), st],
            out_specs=[rows(GLA_WIDTH), st],
            out_shape=out_shape,
            compiler_params=_params(("parallel",)),
            name="gla_short",
        )(q, k, v, g, s0)
    rb = min(seq_len, 4 * chunk)
    nblk = seq_len // rb
    rows = lambda w: pl.BlockSpec((rb, w), lambda s, j: (s * nblk + j, 0))
    st = pl.BlockSpec((1, GLA_HEADS, GLA_DK, GLA_DV), lambda s, j: (s, 0, 0, 0))
    return pl.pallas_call(
        functools.partial(_gla_long_kernel, chunk=chunk, n_inner=rb // chunk),
        grid=(nseq, nblk),
        in_specs=[rows(GLA_KEY_WIDTH), rows(GLA_KEY_WIDTH), rows(GLA_WIDTH), rows(GLA_KEY_WIDTH), st],
        out_specs=[rows(GLA_WIDTH), st],
        out_shape=out_shape,
        scratch_shapes=[pltpu.VMEM((GLA_HEADS, GLA_DK, GLA_DV), F32)],
        compiler_params=_params(("parallel", "arbitrary")),
        name="gla_long",
    )(q, k, v, g, s0)


def _cmul(ar, ai, br, bi):
    return ar * br - ai * bi, ar * bi + ai * br


def _cpow(ar, ai, n):
    res = None
    while n:
        if n & 1:
            res = (ar, ai) if res is None else _cmul(res[0], res[1], ar, ai)
        n >>= 1
        if n:
            ar, ai = _cmul(ar, ai, ar, ai)
    return res


def _s5_segment_len(seq_len):
    s = -(-seq_len // 8)
    s = -(-s // 4) * 4
    return s if (s // 4) % 2 == 1 else s + 4


def _st_store(st_scr, rows, val):
    for t in range(S5_LT):
        st_scr[t, rows, :] = val[:, t * 128:(t + 1) * 128]


def _st_load(st_scr, rows):
    return jnp.concatenate([st_scr[t, rows, :] for t in range(S5_LT)], axis=1)


def _s5_long_kernel(u_ref, bb_ref, cc_ref, dsk_ref, are_ref, aim_ref, h0re_ref, h0im_ref,
                    y_ref, hre_ref, him_ref, st_scr, *, seq_len, seg):
    T, S, SL = seq_len, seg, S5_SL
    rc = min(T, 512)
    for c in range(T // rc):
        sl = slice(c * rc, (c + 1) * rc)
        _st_store(st_scr, sl, _dot(u_ref[sl, :].astype(BF16), bb_ref[0]))
    if 8 * S > T:
        _st_store(st_scr, slice(T, 8 * S), jnp.zeros((8 * S - T, 2 * SL), F32))
    a_re, a_im = are_ref[0], aim_ref[0]
    ar = jnp.broadcast_to(a_re, (8, SL))
    ai = jnp.broadcast_to(a_im, (8, SL))

    def step(i, h, store):
        strand = pl.ds(i, 8, stride=S)
        x = _st_load(st_scr, strand)
        mr, mi = _cmul(ar, ai, h[0], h[1])
        nr, ni = mr + x[:, :SL], mi + x[:, SL:]
        if store:
            _st_store(st_scr, strand, jnp.concatenate([nr, ni], axis=1))
        return nr, ni

    zero = jnp.zeros((8, SL), F32)
    fr, fi = lax.fori_loop(0, S, functools.partial(step, store=False), (zero, zero), unroll=S5_UNROLL)

    as_re, as_im = _cpow(a_re, a_im, S)
    rows = lax.broadcasted_iota(jnp.int32, (8, SL), 0)
    cr, ci = h0re_ref[0, 0], h0im_ref[0, 0]
    car_r, car_i = zero, zero
    for r in range(8):
        car_r = jnp.where(rows == r, cr, car_r)
        car_i = jnp.where(rows == r, ci, car_i)
        if r < 7:
            mr, mi = _cmul(as_re, as_im, cr, ci)
            cr, ci = mr + fr[r:r + 1], mi + fi[r:r + 1]

    lax.fori_loop(0, S, functools.partial(step, store=True), (car_r, car_i), unroll=S5_UNROLL)

    last = _st_load(st_scr, slice(T - 1, T))
    hre_ref[0, 0] = last[:, :SL]
    him_ref[0, 0] = last[:, SL:]
    for c in range(T // rc):
        sl = slice(c * rc, (c + 1) * rc)
        y_ref[sl, :] = _dot(_st_load(st_scr, sl).astype(BF16), cc_ref[0]) + dsk_ref[0] * u_ref[sl, :]


def _s5_long(u, mats, h0_re, h0_im, *, seq_len):
    bb, cc, dsk, a_re, a_im = mats
    nseq = h0_re.shape[0]
    seg = _s5_segment_len(seq_len)
    gb3 = lambda shape: pl.BlockSpec((1,) + shape, lambda s, g: (g, 0, 0))
    st = pl.BlockSpec((1, 1, 1, S5_SL), lambda s, g: (s, g, 0, 0))
    urow = pl.BlockSpec((seq_len, S5_UL), lambda s, g: (s, g))
    st_shape = jax.ShapeDtypeStruct((nseq, S5_NGB, 1, S5_SL), F32)
    return pl.pallas_call(
        functools.partial(_s5_long_kernel, seq_len=seq_len, seg=seg),
        grid=(nseq, S5_NGB),
        in_specs=[urow, gb3((S5_UL, 2 * S5_SL)), gb3((2 * S5_SL, S5_UL)), gb3((1, S5_UL)),
                  gb3((1, S5_SL)), gb3((1, S5_SL)), st, st],
        out_specs=[urow, st, st],
        out_shape=[jax.ShapeDtypeStruct(u.shape, F32), st_shape, st_shape],
        scratch_shapes=[pltpu.VMEM((S5_LT, 8 * seg, 128), F32)],
        compiler_params=_params(("parallel", "parallel")),
        name="s5_long",
    )(u, bb, cc, dsk, a_re, a_im, h0_re, h0_im)


def _s5_short_kernel(u_ref, bb_ref, cc_ref, dsk_ref, are_ref, aim_ref, h0re_ref, h0im_ref,
                     y_ref, hre_ref, him_ref, st_scr, *, seq_len, nseq):
    SL = S5_SL
    _st_store(st_scr, slice(None), _dot(u_ref[...].astype(BF16), bb_ref[0]))
    ar = jnp.broadcast_to(are_ref[0], (nseq, SL))
    ai = jnp.broadcast_to(aim_ref[0], (nseq, SL))
    hr, hi = h0re_ref[...], h0im_ref[...]
    for t in range(seq_len):
        step = pl.ds(t, nseq, stride=seq_len)
        x = _st_load(st_scr, step)
        mr, mi = _cmul(ar, ai, hr, hi)
        hr, hi = mr + x[:, :SL], mi + x[:, SL:]
        _st_store(st_scr, step, jnp.concatenate([hr, hi], axis=1))
    hre_ref[...] = hr
    him_ref[...] = hi
    y_ref[...] = _dot(_st_load(st_scr, slice(None)).astype(BF16), cc_ref[0]) + dsk_ref[0] * u_ref[...]


def _s5_short(u, mats, h0_re, h0_im, *, seq_len):
    bb, cc, dsk, a_re, a_im = mats
    nseq = h0_re.shape[0]
    n = nseq * seq_len
    gb3 = lambda shape: pl.BlockSpec((1,) + shape, lambda g: (g, 0, 0))
    st = pl.BlockSpec((nseq, S5_SL), lambda g: (0, g))
    urow = pl.BlockSpec((n, S5_UL), lambda g: (0, g))
    st_shape = jax.ShapeDtypeStruct(h0_re.shape, F32)
    return pl.pallas_call(
        functools.partial(_s5_short_kernel, seq_len=seq_len, nseq=nseq),
        grid=(S5_NGB,),
        in_specs=[urow, gb3((S5_UL, 2 * S5_SL)), gb3((2 * S5_SL, S5_UL)), gb3((1, S5_UL)),
                  gb3((1, S5_SL)), gb3((1, S5_SL)), st, st],
        out_specs=[urow, st, st],
        out_shape=[jax.ShapeDtypeStruct(u.shape, F32), st_shape, st_shape],
        scratch_shapes=[pltpu.VMEM((S5_LT, n, 128), F32)],
        compiler_params=_params(("parallel",)),
        name="s5_short",
    )(u, bb, cc, dsk, a_re, a_im, h0_re, h0_im)


def _s5_matrices(lam_re, lam_im, log_dt, b_re, b_im, c_re, c_im, d_skip):
    dt = jnp.exp(log_dt)[:, None]
    mag = jnp.exp(lam_re * dt)
    ab_re, ab_im = mag * jnp.cos(lam_im * dt), mag * jnp.sin(lam_im * dt)
    den = lam_re * lam_re + lam_im * lam_im
    f_re = ((ab_re - 1.0) * lam_re + ab_im * lam_im) / den
    f_im = (ab_im * lam_re - (ab_re - 1.0) * lam_im) / den
    bb_re = f_re[..., None] * b_re - f_im[..., None] * b_im
    bb_im = f_re[..., None] * b_im + f_im[..., None] * b_re
    eye = jnp.eye(S5_GB, dtype=F32)

    def in_mat(m):
        m = m.reshape(S5_NGB, S5_GB, S5_STATE, S5_GROUP)
        return jnp.einsum('bgpc,gh->bgchp', m, eye).reshape(S5_NGB, S5_UL, S5_SL)

    def out_mat(m):
        m = m.reshape(S5_NGB, S5_GB, S5_GROUP, S5_STATE)
        return jnp.einsum('bgcp,gh->bgphc', m, eye).reshape(S5_NGB, S5_SL, S5_UL)

    bb = jnp.concatenate([in_mat(bb_re), in_mat(bb_im)], axis=2).astype(BF16)
    cc = jnp.concatenate([out_mat(c_re), out_mat(-c_im)], axis=1).astype(BF16)
    dsk = d_skip.reshape(S5_NGB, 1, S5_UL)
    return bb, cc, dsk, ab_re.reshape(S5_NGB, 1, S5_SL), ab_im.reshape(S5_NGB, 1, S5_SL)


def _post_mix_kernel(xa_ref, oa_ref, ra_ref, ya_ref, xb_ref, ob_ref, rb_ref, yb_ref, *rest, n_first):
    i = pl.program_id(0)

    @pl.when(i < n_first)
    def _first():
        _post_mix_tile(xa_ref, oa_ref, ra_ref, ya_ref, *rest)

    @pl.when(i >= n_first)
    def _second():
        _post_mix_tile(xb_ref, ob_ref, rb_ref, yb_ref, *rest)


def _post_mix_tile(x_ref, o_ref, r_ref, y5_ref, ggla_ref, wglu_ref, bglu_ref, wout_ref, gffn_ref,
                   wr_ref, br_ref, h1_ref, hn_ref, lg_ref):
    o = o_ref[...]
    parts = []
    for h in range(GLA_HEADS):
        oh = o[:, h * GLA_DV:(h + 1) * GLA_DV]
        parts.append(oh * lax.rsqrt(jnp.mean(oh * oh, axis=-1, keepdims=True) + EPS))
    r = r_ref[...]
    o_gla = (jnp.concatenate(parts, axis=1) * ggla_ref[...]) * (r * jax.nn.sigmoid(r))
    y5 = y5_ref[...]
    z = y5 * (0.5 * (1.0 + jnp.tanh(math.sqrt(2.0 / math.pi) * (y5 + 0.044715 * (y5 * y5 * y5)))))
    o_s5 = z * jax.nn.sigmoid(_dot(z.astype(BF16), wglu_ref[...]) + bglu_ref[...])
    att = (_dot(o_gla.astype(BF16), wout_ref[0:GLA_WIDTH, :])
           + _dot(o_s5.astype(BF16), wout_ref[GLA_WIDTH:GLA_WIDTH + S5_WIDTH, :]))
    h1 = x_ref[...] + att
    h1_ref[...] = h1
    hn = _rms(h1, gffn_ref[...])
    hn_ref[...] = _pack_bf16_pairs(hn)
    hn_hi, hn_lo = _split2(hn)
    both = _dot(hn_hi, wr_ref[...])
    logits = (both[:, :ROUTER_LANES] + both[:, ROUTER_LANES:] + _dot(hn_lo, wr_ref[:, :ROUTER_LANES])
              + br_ref[...])
    lg_ref[...] = _route_tile(logits)


def _route_tile(logits):
    lt = logits.T[:ROUTE_ROWS]
    row = lax.broadcasted_iota(jnp.int32, lt.shape, 0)
    ninf = float('-inf')

    def first_max(vals):
        m = jnp.max(vals, axis=0, keepdims=True)
        return m, jnp.min(jnp.where(vals == m, row, ROUTE_ROWS), axis=0, keepdims=True)

    is_group = row < N_GROUPS
    gmax, gsel = first_max(jnp.where(is_group, lt, ninf))
    p_group = 1.0 / jnp.sum(jnp.where(is_group, jnp.exp(lt - gmax), 0.0), axis=0, keepdims=True)
    first = N_GROUPS + gsel * EXPERTS_PER_GROUP
    in_group = jnp.logical_and(row >= first, row < first + EXPERTS_PER_GROUP)
    cand = jnp.where(in_group, lt, ninf)
    m1, i1 = first_max(cand)
    m2, i2 = first_max(jnp.where(row == i1, ninf, cand))
    t = jnp.exp(m2 - m1)
    p1 = 1.0 / (1.0 + t)
    out_row = lax.broadcasted_iota(jnp.int32, (8, lt.shape[1]), 0)
    out = jnp.where(out_row == 0, (i1 - N_GROUPS).astype(F32), 0.0)
    out = jnp.where(out_row == 1, (i2 - N_GROUPS).astype(F32), out)
    out = jnp.where(out_row == 2, p_group * p1, out)
    return jnp.where(out_row == 3, p_group * (t * p1), out)


def _post_mix(rows_a, rows_b, g_gla, w_glu, b_glu, w_out, g_ffn, wr_pair, b_r, tm):
    na, nb = rows_a[0].shape[0], rows_b[0].shape[0]
    n_first = na // tm
    widths = [D_MODEL, GLA_WIDTH, GLA_WIDTH, S5_WIDTH]
    spec_a = [pl.BlockSpec((tm, w), lambda i: (jnp.minimum(i, n_first - 1), 0)) for w in widths]
    spec_b = [pl.BlockSpec((tm, w), lambda i: (jnp.maximum(i - n_first, 0), 0)) for w in widths]
    full = lambda a: pl.BlockSpec(a.shape, lambda i: (0,) * a.ndim)
    row = lambda w: pl.BlockSpec((tm, w), lambda i: (i, 0))
    weights = [g_gla, w_glu, b_glu, w_out, g_ffn, wr_pair, b_r]
    n = na + nb
    return pl.pallas_call(
        functools.partial(_post_mix_kernel, n_first=n_first),
        grid=(n // tm,),
        in_specs=spec_a + spec_b + [full(a) for a in weights],
        out_specs=[row(D_MODEL), row(D_MODEL // 2), pl.BlockSpec((8, tm), lambda i: (0, i))],
        out_shape=[jax.ShapeDtypeStruct((n, D_MODEL), F32), jax.ShapeDtypeStruct((n, D_MODEL // 2), jnp.uint32),
                   jax.ShapeDtypeStruct((8, n), F32)],
        compiler_params=_params(("parallel",)),
        name="post_mix",
    )(*rows_a, *rows_b, *weights)


def _gather_start(idx_ref, idx_base, idx_stride, src_hbm, dst, sem, n):
    def body(r, carry):
        row = idx_ref[idx_base + r * idx_stride]
        pltpu.make_async_copy(src_hbm.at[pl.ds(row, 1)], dst.at[pl.ds(r, 1)], sem).start()
        return carry
    lax.fori_loop(0, n, body, 0, unroll=8)


def _pack_bf16_pairs(x):
    half = x.shape[1] // 2
    bits = lax.bitcast_convert_type(x.astype(BF16).astype(F32), jnp.uint32)
    return bits[:, half:] | (bits[:, :half] >> 16)


def _unpack_bf16_pairs(p):
    lo = lax.bitcast_convert_type(p << 16, F32).astype(BF16)
    hi = lax.bitcast_convert_type(p & jnp.uint32(0xFFFF0000), F32).astype(BF16)
    return jnp.concatenate([lo, hi], axis=1)


def _gather_wait(src_hbm, dst, sem, n):
    pltpu.make_async_copy(src_hbm.at[pl.ds(0, n)], dst.at[pl.ds(0, n)], sem).wait()


def _dispatch_rows_kernel(dest_ref, fill_ref, x_ref, xs_hbm, buf, sem, zsem, *, tm):
    i = pl.program_id(0)
    slot = i % 2

    def row_copy(r, k, s):
        row = dest_ref[(i * tm + r) * TOP_K + k]
        return pltpu.make_async_copy(buf.at[s, pl.ds(r, 1)], xs_hbm.at[pl.ds(row, 1)], sem.at[s])

    def wait_slot(s):
        for _ in range(TOP_K):
            pltpu.make_async_copy(buf.at[s], xs_hbm.at[pl.ds(0, tm)], sem.at[s]).wait()

    @pl.when(i == 0)
    def _zero_fill():
        buf[1] = jnp.zeros(buf.shape[1:], buf.dtype)

        def fill(n, carry):
            @pl.when(fill_ref[n] >= 0)
            def _():
                pltpu.make_async_copy(buf.at[1, pl.ds(0, MOE_TM)], xs_hbm.at[pl.ds(fill_ref[n] * MOE_TM, MOE_TM)],
                                      zsem).start()
            return carry
        lax.fori_loop(0, 2 * N_EXPERTS, fill, 0)

        def drain(n, carry):
            @pl.when(fill_ref[n] >= 0)
            def _():
                pltpu.make_async_copy(buf.at[1, pl.ds(0, MOE_TM)], xs_hbm.at[pl.ds(0, MOE_TM)], zsem).wait()
            return carry
        lax.fori_loop(0, 2 * N_EXPERTS, drain, 0)

    @pl.when(i >= 2)
    def _reuse():
        wait_slot(slot)

    buf[slot] = x_ref[...]

    def issue(r, carry):
        for k in range(TOP_K):
            row_copy(r, k, slot).start()
        return carry
    lax.fori_loop(0, tm, issue, 0, unroll=4)

    @pl.when(i == pl.num_programs(0) - 1)
    def _finish():
        wait_slot(slot)

        @pl.when(i >= 1)
        def _():
            wait_slot(1 - slot)


def _dispatch_rows(dest, fill_blocks, hn, n_rows, tm):
    n = hn.shape[0]
    assert tm >= MOE_TM
    return pl.pallas_call(
        functools.partial(_dispatch_rows_kernel, tm=tm),
        grid_spec=pltpu.PrefetchScalarGridSpec(
            num_scalar_prefetch=2,
            grid=(n // tm,),
            in_specs=[pl.BlockSpec((tm, D_MODEL // 2), lambda i, d, lb: (i, 0))],
            out_specs=pl.BlockSpec(memory_space=pl.ANY),
            scratch_shapes=[pltpu.VMEM((2, tm, D_MODEL // 2), jnp.uint32), pltpu.SemaphoreType.DMA((2,)),
                            pltpu.SemaphoreType.DMA],
        ),
        out_shape=jax.ShapeDtypeStruct((n_rows, D_MODEL // 2), jnp.uint32),
        compiler_params=_params(("arbitrary",)),
        name="dispatch_rows",
    )(dest, fill_blocks, hn)


def _moe_kernel(bexp_ref, eord_ref, next_ref, nused_ref, x_ref, wg_hbm, wu_hbm, wd_hbm, o_ref,
                wg_st, wu_st, wd_st, wsem, wg_bf, wu_bf, wd_bf):
    b = pl.program_id(0)
    nu = nused_ref[0]
    e = bexp_ref[b]
    new_expert = jnp.logical_or(b == 0, bexp_ref[jnp.maximum(b - 1, 0)] != e)

    def weight_copies(expert, slot):
        return [pltpu.make_async_copy(src.at[expert], dst.at[slot], wsem.at[slot])
                for src, dst in ((wg_hbm, wg_st), (wu_hbm, wu_st), (wd_hbm, wd_st))]

    @pl.when(b == 0)
    def _prologue():
        for c in weight_copies(e, 0):
            c.start()

    @pl.when(jnp.logical_and(b < nu, new_expert))
    def _new_expert():
        slot = eord_ref[b] % 2
        for c in weight_copies(e, slot):
            c.wait()
        nxt = next_ref[b]

        @pl.when(nxt >= 0)
        def _prefetch():
            for c in weight_copies(nxt, 1 - slot):
                c.start()

        wg_bf[...] = wg_st[slot].astype(BF16)
        wu_bf[...] = wu_st[slot].astype(BF16)
        wd_bf[...] = wd_st[slot].astype(BF16)

    @pl.when(b < nu)
    def _run():
        x = _unpack_bf16_pairs(x_ref[...])
        gate = _dot(x, wg_bf[...])
        up = _dot(x, wu_bf[...])
        hid = ((gate * jax.nn.sigmoid(gate)) * up).astype(BF16)
        o_ref[...] = _dot(hid, wd_bf[...])

    @pl.when(b >= nu)
    def _skip():
        o_ref[...] = jnp.zeros(o_ref.shape, o_ref.dtype)


def _moe(block_exp, block_ord, block_next, n_used, xs, w_gate, w_up, w_down):
    nblk = xs.shape[0] // MOE_TM
    any_spec = pl.BlockSpec(memory_space=pl.ANY)
    return pl.pallas_call(
        _moe_kernel,
        grid_spec=pltpu.PrefetchScalarGridSpec(
            num_scalar_prefetch=4,
            grid=(nblk,),
            in_specs=[pl.BlockSpec((MOE_TM, D_MODEL // 2), lambda b, be, eo, nx, nu: (jnp.minimum(b, nu[0] - 1), 0)),
                      any_spec, any_spec, any_spec],
            out_specs=pl.BlockSpec((MOE_TM, D_MODEL), lambda b, *_: (b, 0)),
            scratch_shapes=[pltpu.VMEM((2, D_MODEL, EXPERT_HIDDEN), F32), pltpu.VMEM((2, D_MODEL, EXPERT_HIDDEN), F32),
                            pltpu.VMEM((2, EXPERT_HIDDEN, D_MODEL), F32), pltpu.SemaphoreType.DMA((2,)),
                            pltpu.VMEM((D_MODEL, EXPERT_HIDDEN), BF16), pltpu.VMEM((D_MODEL, EXPERT_HIDDEN), BF16),
                            pltpu.VMEM((EXPERT_HIDDEN, D_MODEL), BF16)],
        ),
        out_shape=jax.ShapeDtypeStruct((nblk * MOE_TM, D_MODEL), F32),
        compiler_params=_params(("arbitrary",)),
        name="moe",
    )(block_exp, block_ord, block_next, n_used, xs, w_gate, w_up, w_down)


def _final_kernel(dest_ref, h1_ref, wa_ref, wb_ref, g_ref, rows_hbm, y_ref, ybuf, sem, *, tm, tok_off):
    i = pl.program_id(0)

    def start(tile):
        slot = tile % 2
        base = (tok_off + tile * tm) * TOP_K
        for k in range(TOP_K):
            _gather_start(dest_ref, base + k, TOP_K, rows_hbm, ybuf.at[slot, k], sem.at[slot], tm)

    @pl.when(i == 0)
    def _first():
        start(i)

    @pl.when(i + 1 < pl.num_programs(0))
    def _next():
        start(i + 1)

    slot = i % 2
    for k in range(TOP_K):
        _gather_wait(rows_hbm, ybuf.at[slot, k], sem.at[slot], tm)
    moe = ybuf[slot, 0] * wa_ref[...] + ybuf[slot, 1] * wb_ref[...]
    y_ref[...] = _rms(h1_ref[...] + moe, g_ref[...])


def _final(dest, h1, wa, wb, g_final, rows, *, tm, n, row_off):
    off = row_off // tm
    row = lambda w: pl.BlockSpec((tm, w), lambda i, d: (i + off, 0))
    return pl.pallas_call(
        functools.partial(_final_kernel, tm=tm, tok_off=row_off),
        grid_spec=pltpu.PrefetchScalarGridSpec(
            num_scalar_prefetch=1,
            grid=(n // tm,),
            in_specs=[row(D_MODEL), row(1), row(1), pl.BlockSpec((1, D_MODEL), lambda i, d: (0, 0)),
                      pl.BlockSpec(memory_space=pl.ANY)],
            out_specs=pl.BlockSpec((tm, D_MODEL), lambda i, d: (i, 0)),
            scratch_shapes=[pltpu.VMEM((2, TOP_K, tm, D_MODEL), F32), pltpu.SemaphoreType.DMA((2,))],
        ),
        out_shape=jax.ShapeDtypeStruct((n, D_MODEL), F32),
        compiler_params=_params(("arbitrary",)),
        name="final",
    )(dest, h1, wa, wb, g_final, rows)


def _dispatch(eid):
    t = eid.shape[0]
    a = t * TOP_K
    assert a % MOE_TM == 0
    nblk = a // MOE_TM + N_EXPERTS
    flat = eid.reshape(-1)
    experts = jnp.arange(N_EXPERTS, dtype=jnp.int32)
    onehot = (flat[:, None] == experts[None, :]).astype(F32).reshape(a // MOE_TM, MOE_TM, N_EXPERTS)
    strict_lower = jnp.tril(jnp.ones((MOE_TM, MOE_TM), F32), -1)
    within = jnp.einsum('ij,bjk->bik', strict_lower, onehot)
    totals = jnp.sum(onehot, axis=1)
    before = jnp.cumsum(totals, axis=0) - totals
    counts = jnp.sum(totals, axis=0).astype(jnp.int32)
    padded = (counts + MOE_TM - 1) // MOE_TM * MOE_TM
    pends = jnp.cumsum(padded)
    offset = before + (pends - padded).astype(F32)[None, :]
    dest = jnp.sum((within + offset[:, None, :]) * onehot, axis=-1).reshape(-1)
    n_used = (pends[-1] // MOE_TM).astype(jnp.int32)
    blk = jnp.minimum(jnp.arange(nblk, dtype=jnp.int32), n_used - 1)
    block_exp = jnp.minimum(jnp.searchsorted(pends, blk * MOE_TM, side='right'), N_EXPERTS - 1).astype(jnp.int32)
    in_use = counts > 0
    ordinal = jnp.cumsum(in_use.astype(jnp.int32)) - 1
    later = lax.cummin(jnp.where(in_use, experts, N_EXPERTS), axis=0, reverse=True)
    nxt = jnp.concatenate([later[1:], jnp.full((1,), N_EXPERTS, jnp.int32)])
    nxt = jnp.where(nxt < N_EXPERTS, nxt, -1)
    last_block = jnp.where(in_use, pends // MOE_TM - 1, -1)
    unused = n_used + experts
    fill_blocks = jnp.concatenate([last_block, jnp.where(unused < nblk, unused, -1)]).astype(jnp.int32)
    return (dest.astype(jnp.int32), nblk * MOE_TM, block_exp, ordinal[block_exp], nxt[block_exp], fill_blocks,
            n_used.reshape(1))


def kernel(x_prompt, x_sample, state_gla, state_s5_re, state_s5_im, meta, g_mix, w_in, w_gk2, b_gk, g_gla,
           lam_re, lam_im, log_dt, s5_b_re, s5_b_im, s5_c_re, s5_c_im, d_skip, w_glu, b_glu, w_out, g_ffn,
           w_rg, b_rg, w_re, b_re, w_gate, w_up, w_down, g_final):
    bp, tp, _ = x_prompt.shape
    bs, ts, _ = x_sample.shape
    l = 0

    w_cat = w_in[l].astype(BF16)
    wgk_hi, wgk_lo = _split2(jnp.pad(w_gk2[l], ((0, 128 - GLA_RANK), (0, 0))))
    g_mix2, b_gk2 = g_mix[l][None], b_gk[l][None]
    mats = _s5_matrices(lam_re[l], lam_im[l], log_dt[l], s5_b_re[l], s5_b_im[l], s5_c_re[l], s5_c_im[l],
                        d_skip[l])
    w_router = jnp.concatenate([w_rg[l], jnp.moveaxis(w_re[l], 0, 1).reshape(D_MODEL, N_EXPERTS)], axis=1)
    w_router = jnp.pad(w_router, ((0, 0), (0, ROUTER_LANES - N_GROUPS - N_EXPERTS)))
    wr_pair = jnp.concatenate(_split2(w_router), axis=1)
    b_router = jnp.pad(jnp.concatenate([b_rg[l], b_re[l].reshape(-1)]),
                       (0, ROUTER_LANES - N_GROUPS - N_EXPERTS))[None]
    w_glu_b, w_out_b = w_glu[l].astype(BF16), w_out[l].astype(BF16)
    g_gla2, b_glu2, g_ffn2 = g_gla[l].reshape(1, GLA_WIDTH), b_glu[l][None], g_ffn[l][None]

    proj = functools.partial(_in_proj, g_mix=g_mix2, w_cat=w_cat, wgk_hi=wgk_hi, wgk_lo=wgk_lo, b_gk=b_gk2)

    qm, km, vm, _, gm, um = proj(meta, tm=N_META)
    zero_s = jnp.zeros((1, GLA_HEADS, GLA_DK, GLA_DV), F32)
    _, s_meta = _gla(qm, km, vm, gm, zero_s, seq_len=N_META, chunk=N_META)
    zero_h = jnp.zeros((1, S5_NGB, 1, S5_SL), F32)
    _, hm_re, hm_im = _s5_long(um, mats, zero_h, zero_h, seq_len=N_META)

    xp = x_prompt.reshape(bp * tp, D_MODEL)
    qp, kp, vp, rp, gp, up = proj(xp, tm=512)
    op, gla_p = _gla(qp, kp, vp, gp, jnp.broadcast_to(s_meta, (bp,) + s_meta.shape[1:]),
                     seq_len=tp, chunk=GLA_CHUNK)
    y5p, hp_re, hp_im = _s5_long(up, mats, jnp.broadcast_to(hm_re, (bp,) + hm_re.shape[1:]),
                                 jnp.broadcast_to(hm_im, (bp,) + hm_im.shape[1:]), seq_len=tp)

    xs = x_sample.reshape(bs * ts, D_MODEL)
    qs, ks, vs, rs, gs, us = proj(xs, tm=512)
    chunk_s = GLA_CHUNK if ts % GLA_CHUNK == 0 else ts
    os_, gla_s = _gla(qs, ks, vs, gs, state_gla[l], seq_len=ts, chunk=chunk_s)
    y5s, hs_re, hs_im = _s5_short(us, mats, state_s5_re[l].reshape(bs, -1), state_s5_im[l].reshape(bs, -1),
                                  seq_len=ts)

    np_rows, ns_rows = bp * tp, bs * ts
    h1, hn, route = _post_mix((xp, op, rp, y5p), (xs, os_, rs, y5s), g_gla2, w_glu_b, b_glu2, w_out_b, g_ffn2,
                               wr_pair, b_router, tm=256)

    eid = route[:TOP_K].T.astype(jnp.int32)
    wts = route[TOP_K:2 * TOP_K].T
    dest, n_sorted, block_exp, block_ord, block_next, fill_blocks, n_used = _dispatch(eid)
    xs_rows = _dispatch_rows(dest, fill_blocks, hn, n_sorted, tm=512)
    out_rows = _moe(block_exp, block_ord, block_next, n_used, xs_rows, w_gate[l], w_up[l], w_down[l])
    fin = functools.partial(_final, dest, h1, wts[:, 0:1], wts[:, 1:2], g_final[None], out_rows, tm=512)
    y_prompt = fin(n=np_rows, row_off=0)
    y_sample = fin(n=ns_rows, row_off=np_rows)

    return (y_prompt.reshape(bp, tp, D_MODEL), y_sample.reshape(bs, ts, D_MODEL),
            gla_p[None], hp_re.reshape(1, bp, S5_GROUPS, S5_STATE), hp_im.reshape(1, bp, S5_GROUPS, S5_STATE),
            gla_s[None], hs_re.reshape(1, bs, S5_GROUPS, S5_STATE), hs_im.reshape(1, bs, S5_GROUPS, S5_STATE))
```

```python
import functools
import math

import jax
import jax.numpy as jnp
from jax import lax
from jax.experimental import pallas as pl
from jax.experimental.pallas import tpu as pltpu

F32 = jnp.float32
BF16 = jnp.bfloat16

D_MODEL = 2048
N_META = 16
GLA_HEADS = 4
GLA_DK = 128
GLA_DV = 256
GLA_KEY_WIDTH = GLA_HEADS * GLA_DK
GLA_WIDTH = GLA_HEADS * GLA_DV
GLA_RANK = 16
GLA_GATE_NORM = 16.0
GLA_CHUNK = 64
GLA_SUB = 16
GLA_SEQS_PER_STEP = 8
MASKED_EXPONENT = -1e30
S5_WIDTH = 1024
S5_GROUP = 16
S5_GROUPS = 64
S5_STATE = 64
S5_GB = 8
S5_NGB = S5_GROUPS // S5_GB
S5_UL = S5_GB * S5_GROUP
S5_SL = S5_GB * S5_STATE
S5_LT = 2 * S5_SL // 128
S5_UNROLL = 4
S5_PIPE_CHUNKS = 1
N_GROUPS = 4
EXPERTS_PER_GROUP = 8
N_EXPERTS = N_GROUPS * EXPERTS_PER_GROUP
EXPERT_HIDDEN = 512
TOP_K = 2
EPS = 1e-6
ROUTER_LANES = 128
ROUTE_ROWS = -(-(N_GROUPS + N_EXPERTS) // 8) * 8
MOE_TM = 256
VMEM_LIMIT = 56 * 1024 * 1024

_dot = functools.partial(jnp.dot, preferred_element_type=F32)


def _split2(x):
    hi = x.astype(BF16)
    lo = (x - hi.astype(F32)).astype(BF16)
    return hi, lo


def _rms(x, g):
    return x * lax.rsqrt(jnp.mean(x * x, axis=-1, keepdims=True) + EPS) * g


def _params(sem):
    return pltpu.CompilerParams(dimension_semantics=sem, vmem_limit_bytes=VMEM_LIMIT)


def _in_proj_kernel(x_ref, g_ref, w_ref, wgk_hi_ref, wgk_lo_ref, bgk_ref,
                    q_ref, k_ref, v_ref, r_ref, gk_ref, u_ref):
    xb = _rms(x_ref[...], g_ref[...]).astype(BF16)
    kw = GLA_KEY_WIDTH
    q_ref[...] = _dot(xb, w_ref[:, 0:kw]) * (GLA_DK ** -0.5)
    k_ref[...] = _dot(xb, w_ref[:, kw:2 * kw])
    v_ref[...] = _dot(xb, w_ref[:, 2 * kw:2 * kw + GLA_WIDTH])
    r_ref[...] = _dot(xb, w_ref[:, 2 * kw + GLA_WIDTH:2 * kw + 2 * GLA_WIDTH])
    c0 = 2 * kw + 2 * GLA_WIDTH
    tail = _dot(xb, w_ref[:, c0:])
    u_ref[...] = tail[:, GLA_RANK:GLA_RANK + S5_WIDTH]
    a_low = tail[:, :128]
    a_hi, a_lo = _split2(a_low)
    z = (_dot(a_hi, wgk_hi_ref[...]) + _dot(a_hi, wgk_lo_ref[...]) + _dot(a_lo, wgk_hi_ref[...])
         + bgk_ref[...])
    gk_ref[...] = (jnp.minimum(z, 0.0) - jnp.log1p(jnp.exp(-jnp.abs(z)))) * (1.0 / GLA_GATE_NORM)


def _in_proj(x, g_mix, w_cat, wgk_hi, wgk_lo, b_gk, tm):
    n = x.shape[0]
    wcols = w_cat.shape[1]
    row = lambda w: pl.BlockSpec((tm, w), lambda i: (i, 0))
    full = lambda a: pl.BlockSpec(a.shape, lambda i: (0,) * a.ndim)
    widths = [GLA_KEY_WIDTH, GLA_KEY_WIDTH, GLA_WIDTH, GLA_WIDTH, GLA_KEY_WIDTH, S5_WIDTH]
    return pl.pallas_call(
        _in_proj_kernel,
        grid=(n // tm,),
        in_specs=[row(D_MODEL), full(g_mix),
                  pl.BlockSpec((D_MODEL, wcols), lambda i: (0, 0), pipeline_mode=pl.Buffered(1)),
                  full(wgk_hi), full(wgk_lo), full(b_gk)],
        out_specs=[row(w) for w in widths],
        out_shape=[jax.ShapeDtypeStruct((n, w), F32) for w in widths],
        compiler_params=_params(("parallel",)),
        name="in_proj",
    )(x, g_mix, w_cat, wgk_hi, wgk_lo, b_gk)


def _gla_chunk(q_ref, k_ref, v_ref, g_ref, o_ref, r0, C, consts, read_state, write_state):
    tril, piece_rows, piece_lanes = consts
    sub = min(C, GLA_SUB)
    nsub = C // sub
    g = g_ref[pl.ds(r0, C), :]
    g1 = g.astype(BF16)
    rem = g - g1.astype(F32)
    g2 = rem.astype(BF16)
    g3 = (rem - g2.astype(F32)).astype(BF16)
    b_all = _dot(tril, g1) + _dot(tril, g2) + _dot(tril, g3)
    for h in range(GLA_HEADS):
        ks = slice(h * GLA_DK, (h + 1) * GLA_DK)
        vs = slice(h * GLA_DV, (h + 1) * GLA_DV)
        q = q_ref[pl.ds(r0, C), ks]
        k = k_ref[pl.ds(r0, C), ks]
        b = b_all[:, ks]
        s_prev = read_state(h)
        vb = v_ref[pl.ds(r0, C), vs].astype(BF16)
        o_inter = _dot((q * jnp.exp(b)).astype(BF16), s_prev.astype(BF16))
        blocks = []
        for s in range(nsub):
            lo = s * sub
            bs, qs, ksub = b[lo:lo + sub], q[lo:lo + sub], k[lo:lo + sub]
            acc = o_inter[lo:lo + sub]
            if s > 0:
                anchor = b[lo - 1:lo]
                qd = (qs * jnp.exp(bs - anchor)).astype(BF16)
                kd = (k[:lo] * jnp.exp(anchor - b[:lo])).astype(BF16)
                sc = lax.dot_general(qd, kd, (((1,), (1,)), ((), ())), preferred_element_type=F32)
                acc = acc + _dot(sc.astype(BF16), vb[:lo])
            pieces = [jnp.zeros((8, 128), F32) for _ in range(sub // 8)]
            for jj in range(sub):
                for p in range(jj // 8, sub // 8):
                    r8 = slice(8 * p, 8 * p + 8)
                    diff = bs[r8] - bs[jj:jj + 1]
                    if 8 * p < jj:
                        diff = jnp.where(piece_rows + 8 * p >= jj, diff, MASKED_EXPONENT)
                    col = jnp.sum((qs[r8] * ksub[jj:jj + 1]) * jnp.exp(diff), axis=-1, keepdims=True)
                    pieces[p] = jnp.where(piece_lanes == jj, col, pieces[p])
            scores = jnp.concatenate(pieces, axis=0) if len(pieces) > 1 else pieces[0]
            acc = acc + _dot(scores[:, :sub].astype(BF16), vb[lo:lo + sub])
            blocks.append(acc)
        o_ref[pl.ds(r0, C), vs] = jnp.concatenate(blocks, axis=0) if nsub > 1 else blocks[0]
        b_last = b[C - 1:C]
        kdec = (k * jnp.exp(b_last - b)).astype(BF16)
        upd = lax.dot_general(kdec, vb, (((0,), (0,)), ((), ())), preferred_element_type=F32)
        dcol = jnp.broadcast_to(jnp.exp(b_last), (GLA_DK, GLA_DK)).T
        write_state(h, s_prev * jnp.concatenate([dcol, dcol], axis=1) + upd)


def _gla_consts(C):
    sub = min(C, GLA_SUB)
    ri = lax.broadcasted_iota(jnp.int32, (C, C), 0)
    ci = lax.broadcasted_iota(jnp.int32, (C, C), 1)
    tril = jnp.where(ri >= ci, 1.0, 0.0).astype(BF16)
    piece_rows = lax.broadcasted_iota(jnp.int32, (8, GLA_DK), 0)
    piece_lanes = lax.broadcasted_iota(jnp.int32, (8, 128), 1)
    return tril, piece_rows, piece_lanes


def _gla_long_kernel(q_ref, k_ref, v_ref, g_ref, s0_ref, o_ref, sout_ref, s_scr, *, chunk, n_inner):
    j = pl.program_id(1)
    consts = _gla_consts(chunk)

    @pl.when(j == 0)
    def _init():
        s_scr[...] = s0_ref[0]

    def write_state(h, s):
        s_scr[h] = s

    def chunk_body(c, carry):
        _gla_chunk(q_ref, k_ref, v_ref, g_ref, o_ref, pl.multiple_of(c * chunk, chunk), chunk, consts,
                   lambda h: s_scr[h], write_state)
        return carry

    lax.fori_loop(0, n_inner, chunk_body, 0)

    @pl.when(j == pl.num_programs(1) - 1)
    def _fin():
        sout_ref[0] = s_scr[...]


def _gla_short_kernel(q_ref, k_ref, v_ref, g_ref, s0_ref, o_ref, sout_ref, *, chunk, nb):
    consts = _gla_consts(chunk)

    def seq_body(n, carry):
        def write_state(h, s):
            sout_ref[n, h] = s

        _gla_chunk(q_ref, k_ref, v_ref, g_ref, o_ref, pl.multiple_of(n * chunk, chunk), chunk, consts,
                   lambda h: s0_ref[n, h], write_state)
        return carry

    lax.fori_loop(0, nb, seq_body, 0, unroll=2 if nb % 2 == 0 else 1)


def _gla(q, k, v, g, s0, *, seq_len, chunk):
    nseq = s0.shape[0]
    n = q.shape[0]
    out_shape = [jax.ShapeDtypeStruct((n, GLA_WIDTH), F32),
                 jax.ShapeDtypeStruct((nseq, GLA_HEADS, GLA_DK, GLA_DV), F32)]
    if seq_len == chunk:
        nb = min(nseq, GLA_SEQS_PER_STEP)
        rows = lambda w: pl.BlockSpec((nb * chunk, w), lambda s: (s, 0))
        st = pl.BlockSpec((nb, GLA_HEADS, GLA_DK, GLA_DV), lambda s: (s, 0, 0, 0))
        return pl.pallas_call(
            functools.partial(_gla_short_kernel, chunk=chunk, nb=nb),
            grid=(nseq // nb,),
            in_specs=[rows(GLA_KEY_WIDTH), rows(GLA_KEY_WIDTH), rows(GLA_WIDTH), rows(GLA_KEY_WIDTH), st],
            out_specs=[rows(GLA_WIDTH), st],
            out_shape=out_shape,
            compiler_params=_params(("parallel",)),
            name="gla_short",
        )(q, k, v, g, s0)
    rb = min(seq_len, 4 * chunk)
    nblk = seq_len // rb
    rows = lambda w: pl.BlockSpec((rb, w), lambda s, j: (s * nblk + j, 0))
    st = pl.BlockSpec((1, GLA_HEADS, GLA_DK, GLA_DV), lambda s, j: (s, 0, 0, 0))
    return pl.pallas_call(
        functools.partial(_gla_long_kernel, chunk=chunk, n_inner=rb // chunk),
        grid=(nseq, nblk),
        in_specs=[rows(GLA_KEY_WIDTH), rows(GLA_KEY_WIDTH), rows(GLA_WIDTH), rows(GLA_KEY_WIDTH), st],
        out_specs=[rows(GLA_WIDTH), st],
        out_shape=out_shape,
        scratch_shapes=[pltpu.VMEM((GLA_HEADS, GLA_DK, GLA_DV), F32)],
        compiler_params=_params(("parallel", "arbitrary")),
        name="gla_long",
    )(q, k, v, g, s0)


def _cmul(ar, ai, br, bi):
    return ar * br - ai * bi, ar * bi + ai * br


def _cpow(ar, ai, n):
    res = None
    while n:
        if n & 1:
            res = (ar, ai) if res is None else _cmul(res[0], res[1], ar, ai)
        n >>= 1
        if n:
            ar, ai = _cmul(ar, ai, ar, ai)
    return res


def _s5_segment_len(seq_len):
    s = -(-seq_len // 8)
    s = -(-s // 4) * 4
    return s if (s // 4) % 2 == 1 else s + 4


def _st_store(st_scr, rows, val):
    for t in range(S5_LT):
        st_scr[t, rows, :] = val[:, t * 128:(t + 1) * 128]


def _st_load(st_scr, rows):
    return jnp.concatenate([st_scr[t, rows, :] for t in range(S5_LT)], axis=1)


def _s5_long_kernel(u_ref, bb_ref, cc_ref, dsk_ref, are_ref, aim_ref, h0re_ref, h0im_ref,
                    y_ref, hre_ref, him_ref, st_scr, *, seq_len, seg):
    T, S, SL = seq_len, seg, S5_SL
    rc = min(T, 512)
    for c in range(T // rc):
        sl = slice(c * rc, (c + 1) * rc)
        _st_store(st_scr, sl, _dot(u_ref[sl, :].astype(BF16), bb_ref[0]))
    if 8 * S > T:
        _st_store(st_scr, slice(T, 8 * S), jnp.zeros((8 * S - T, 2 * SL), F32))
    a_re, a_im = are_ref[0], aim_ref[0]
    ar = jnp.broadcast_to(a_re, (8, SL))
    ai = jnp.broadcast_to(a_im, (8, SL))

    def step(i, h, store):
        strand = pl.ds(i, 8, stride=S)
        x = _st_load(st_scr, strand)
        mr, mi = _cmul(ar, ai, h[0], h[1])
        nr, ni = mr + x[:, :SL], mi + x[:, SL:]
        if store:
            _st_store(st_scr, strand, jnp.concatenate([nr, ni], axis=1))
        return nr, ni

    zero = jnp.zeros((8, SL), F32)
    fr, fi = lax.fori_loop(0, S, functools.partial(step, store=False), (zero, zero), unroll=S5_UNROLL)

    car_r, car_i = _s5_carries(fr, fi, a_re, a_im, S, h0re_ref[0, 0], h0im_ref[0, 0])

    lax.fori_loop(0, S, functools.partial(step, store=True), (car_r, car_i), unroll=S5_UNROLL)

    last = _st_load(st_scr, slice(T - 1, T))
    hre_ref[0, 0] = last[:, :SL]
    him_ref[0, 0] = last[:, SL:]
    for c in range(T // rc):
        sl = slice(c * rc, (c + 1) * rc)
        y_ref[sl, :] = _dot(_st_load(st_scr, sl).astype(BF16), cc_ref[0]) + dsk_ref[0] * u_ref[sl, :]


def _s5_long(u, mats, h0_re, h0_im, *, seq_len):
    bb, cc, dsk, a_re, a_im = mats
    nseq = h0_re.shape[0]
    seg = _s5_segment_len(seq_len)
    gb3 = lambda shape: pl.BlockSpec((1,) + shape, lambda s, g: (g, 0, 0))
    st = pl.BlockSpec((1, 1, 1, S5_SL), lambda s, g: (s, g, 0, 0))
    urow = pl.BlockSpec((seq_len, S5_UL), lambda s, g: (s, g))
    st_shape = jax.ShapeDtypeStruct((nseq, S5_NGB, 1, S5_SL), F32)
    return pl.pallas_call(
        functools.partial(_s5_long_kernel, seq_len=seq_len, seg=seg),
        grid=(nseq, S5_NGB),
        in_specs=[urow, gb3((S5_UL, 2 * S5_SL)), gb3((2 * S5_SL, S5_UL)), gb3((1, S5_UL)),
                  gb3((1, S5_SL)), gb3((1, S5_SL)), st, st],
        out_specs=[urow, st, st],
        out_shape=[jax.ShapeDtypeStruct(u.shape, F32), st_shape, st_shape],
        scratch_shapes=[pltpu.VMEM((S5_LT, 8 * seg, 128), F32)],
        compiler_params=_params(("parallel", "parallel")),
        name="s5_long",
    )(u, bb, cc, dsk, a_re, a_im, h0_re, h0_im)


def _s5_pipe_step(b_buf, s_buf, c_buf, u16, c16, ub_ref, uc_ref, bb_ref, cc_ref, dsk_ref, are_ref, aim_ref,
                  h0re_ref, h0im_ref, y_ref, hre_ref, him_ref, *, seq_len, seg):
    T, S, SL = seq_len, seg, S5_SL
    rc, sc = T // S5_PIPE_CHUNKS, S // S5_PIPE_CHUNKS
    if 8 * S > T:
        _st_store(b_buf, slice(T, 8 * S), jnp.zeros((8 * S - T, 2 * SL), F32))
    a_re, a_im = are_ref[0], aim_ref[0]
    ar = jnp.broadcast_to(a_re, (8, SL))
    ai = jnp.broadcast_to(a_im, (8, SL))

    def scan_chunk(k, h, store):
        for d in range(sc):
            strand = pl.ds(k * sc + d, 8, stride=S)
            x = _st_load(s_buf, strand)
            mr, mi = _cmul(ar, ai, h[0], h[1])
            h = (mr + x[:, :SL], mi + x[:, SL:])
            if store:
                _st_store(s_buf, strand, jnp.concatenate(h, axis=1))
        return h

    u16[...] = ub_ref[...].astype(BF16)
    for c in range(S5_PIPE_CHUNKS):
        rows = slice(c * rc, (c + 1) * rc)
        c16[rows, :] = _st_load(c_buf, rows).astype(BF16)

    def end_states(k, h):
        rows = pl.ds(pl.multiple_of(k * rc, rc), rc)
        _st_store(b_buf, rows, _dot(u16[rows, :], bb_ref[0]))
        return scan_chunk(k, h, False)

    zero = jnp.zeros((8, SL), F32)
    fr, fi = lax.fori_loop(0, S5_PIPE_CHUNKS, end_states, (zero, zero))
    car_r, car_i = _s5_carries(fr, fi, a_re, a_im, S, h0re_ref[0, 0], h0im_ref[0, 0])

    def states(k, h):
        rows = pl.ds(pl.multiple_of(k * rc, rc), rc)
        y_ref[rows, :] = _dot(c16[rows, :], cc_ref[0]) + dsk_ref[0] * uc_ref[rows, :]
        return scan_chunk(k, h, True)

    lax.fori_loop(0, S5_PIPE_CHUNKS, states, (car_r, car_i))
    last = _st_load(s_buf, slice(T - 1, T))
    hre_ref[0, 0] = last[:, :SL]
    him_ref[0, 0] = last[:, SL:]


def _s5_carries(fr, fi, a_re, a_im, seg, h0_re, h0_im):
    as_re, as_im = _cpow(a_re, a_im, seg)
    rows = lax.broadcasted_iota(jnp.int32, fr.shape, 0)
    cr, ci = h0_re, h0_im
    car_r, car_i = jnp.zeros_like(fr), jnp.zeros_like(fi)
    for r in range(8):
        car_r = jnp.where(rows == r, cr, car_r)
        car_i = jnp.where(rows == r, ci, car_i)
        if r < 7:
            mr, mi = _cmul(as_re, as_im, cr, ci)
            cr, ci = mr + fr[r:r + 1], mi + fi[r:r + 1]
    return car_r, car_i


def _s5_pipe_kernel(*refs, seq_len, seg):
    bufs, u16, c16 = refs[-5:-2], refs[-2], refs[-1]
    refs = refs[:-5]
    t = pl.program_id(0)

    @pl.when(t == 0)
    def _init():
        for buf in bufs:
            buf[...] = jnp.zeros(buf.shape, buf.dtype)

    for r in range(3):
        @pl.when(t % 3 == r)
        def _rotation(r=r):
            _s5_pipe_step(bufs[r], bufs[(r + 2) % 3], bufs[(r + 1) % 3], u16, c16, *refs, seq_len=seq_len, seg=seg)


def _s5_pipelined(u, mats, h0_re, h0_im, *, seq_len):
    bb, cc, dsk, a_re, a_im = mats
    nseq = h0_re.shape[0]
    n_items = nseq * S5_NGB
    seg = _s5_segment_len(seq_len)
    assert seq_len % (8 * S5_PIPE_CHUNKS) == 0 and seg % S5_PIPE_CHUNKS == 0

    def item(lag):
        return lambda t: jnp.clip(t - lag, 0, n_items - 1)

    def per_gb(shape, lag):
        return pl.BlockSpec((1,) + shape, lambda t: (item(lag)(t) % S5_NGB, 0, 0))

    def rows(lag):
        return pl.BlockSpec((seq_len, S5_UL), lambda t: (item(lag)(t) // S5_NGB, item(lag)(t) % S5_NGB))

    st = pl.BlockSpec((1, 1, 1, S5_SL), lambda t: (item(1)(t) // S5_NGB, item(1)(t) % S5_NGB, 0, 0))
    st_shape = jax.ShapeDtypeStruct((nseq, S5_NGB, 1, S5_SL), F32)
    buf = pltpu.VMEM((S5_LT, 8 * seg, 128), F32)
    return pl.pallas_call(
        functools.partial(_s5_pipe_kernel, seq_len=seq_len, seg=seg),
        grid=(n_items + 2,),
        in_specs=[rows(0), rows(2), per_gb((S5_UL, 2 * S5_SL), 0), per_gb((2 * S5_SL, S5_UL), 2),
                  per_gb((1, S5_UL), 2), per_gb((1, S5_SL), 1), per_gb((1, S5_SL), 1), st, st],
        out_specs=[rows(2), st, st],
        out_shape=[jax.ShapeDtypeStruct(u.shape, F32), st_shape, st_shape],
        scratch_shapes=[buf, buf, buf, pltpu.VMEM((seq_len, S5_UL), BF16), pltpu.VMEM((seq_len, 2 * S5_SL), BF16)],
        compiler_params=_params(("arbitrary",)),
        name="s5_pipelined",
    )(u, u, bb, cc, dsk, a_re, a_im, h0_re, h0_im)


def _s5_short_kernel(u_ref, bb_ref, cc_ref, dsk_ref, are_ref, aim_ref, h0re_ref, h0im_ref,
                     y_ref, hre_ref, him_ref, st_scr, *, seq_len, nseq):
    SL = S5_SL
    _st_store(st_scr, slice(None), _dot(u_ref[...].astype(BF16), bb_ref[0]))
    ar = jnp.broadcast_to(are_ref[0], (nseq, SL))
    ai = jnp.broadcast_to(aim_ref[0], (nseq, SL))
    hr, hi = h0re_ref[...], h0im_ref[...]
    for t in range(seq_len):
        step = pl.ds(t, nseq, stride=seq_len)
        x = _st_load(st_scr, step)
        mr, mi = _cmul(ar, ai, hr, hi)
        hr, hi = mr + x[:, :SL], mi + x[:, SL:]
        _st_store(st_scr, step, jnp.concatenate([hr, hi], axis=1))
    hre_ref[...] = hr
    him_ref[...] = hi
    y_ref[...] = _dot(_st_load(st_scr, slice(None)).astype(BF16), cc_ref[0]) + dsk_ref[0] * u_ref[...]


def _s5_short(u, mats, h0_re, h0_im, *, seq_len):
    bb, cc, dsk, a_re, a_im = mats
    nseq = h0_re.shape[0]
    n = nseq * seq_len
    gb3 = lambda shape: pl.BlockSpec((1,) + shape, lambda g: (g, 0, 0))
    st = pl.BlockSpec((nseq, S5_SL), lambda g: (0, g))
    urow = pl.BlockSpec((n, S5_UL), lambda g: (0, g))
    st_shape = jax.ShapeDtypeStruct(h0_re.shape, F32)
    return pl.pallas_call(
        functools.partial(_s5_short_kernel, seq_len=seq_len, nseq=nseq),
        grid=(S5_NGB,),
        in_specs=[urow, gb3((S5_UL, 2 * S5_SL)), gb3((2 * S5_SL, S5_UL)), gb3((1, S5_UL)),
                  gb3((1, S5_SL)), gb3((1, S5_SL)), st, st],
        out_specs=[urow, st, st],
        out_shape=[jax.ShapeDtypeStruct(u.shape, F32), st_shape, st_shape],
        scratch_shapes=[pltpu.VMEM((S5_LT, n, 128), F32)],
        compiler_params=_params(("parallel",)),
        name="s5_short",
    )(u, bb, cc, dsk, a_re, a_im, h0_re, h0_im)


def _s5_matrices(lam_re, lam_im, log_dt, b_re, b_im, c_re, c_im, d_skip):
    dt = jnp.exp(log_dt)[:, None]
    mag = jnp.exp(lam_re * dt)
    ab_re, ab_im = mag * jnp.cos(lam_im * dt), mag * jnp.sin(lam_im * dt)
    den = lam_re * lam_re + lam_im * lam_im
    f_re = ((ab_re - 1.0) * lam_re + ab_im * lam_im) / den
    f_im = (ab_im * lam_re - (ab_re - 1.0) * lam_im) / den
    bb_re = f_re[..., None] * b_re - f_im[..., None] * b_im
    bb_im = f_re[..., None] * b_im + f_im[..., None] * b_re
    eye = jnp.eye(S5_GB, dtype=F32)

    def in_mat(m):
        m = m.reshape(S5_NGB, S5_GB, S5_STATE, S5_GROUP)
        return jnp.einsum('bgpc,gh->bgchp', m, eye).reshape(S5_NGB, S5_UL, S5_SL)

    def out_mat(m):
        m = m.reshape(S5_NGB, S5_GB, S5_GROUP, S5_STATE)
        return jnp.einsum('bgcp,gh->bgphc', m, eye).reshape(S5_NGB, S5_SL, S5_UL)

    bb = jnp.concatenate([in_mat(bb_re), in_mat(bb_im)], axis=2).astype(BF16)
    cc = jnp.concatenate([out_mat(c_re), out_mat(-c_im)], axis=1).astype(BF16)
    dsk = d_skip.reshape(S5_NGB, 1, S5_UL)
    return bb, cc, dsk, ab_re.reshape(S5_NGB, 1, S5_SL), ab_im.reshape(S5_NGB, 1, S5_SL)


def _post_mix_kernel(xa_ref, oa_ref, ra_ref, ya_ref, xb_ref, ob_ref, rb_ref, yb_ref, *rest, n_first):
    i = pl.program_id(0)

    @pl.when(i < n_first)
    def _first():
        _post_mix_tile(xa_ref, oa_ref, ra_ref, ya_ref, *rest)

    @pl.when(i >= n_first)
    def _second():
        _post_mix_tile(xb_ref, ob_ref, rb_ref, yb_ref, *rest)


def _post_mix_tile(x_ref, o_ref, r_ref, y5_ref, ggla_ref, wglu_ref, bglu_ref, wout_ref, gffn_ref,
                   wr_ref, br_ref, h1_ref, hn_ref, lg_ref):
    o = o_ref[...]
    parts = []
    for h in range(GLA_HEADS):
        oh = o[:, h * GLA_DV:(h + 1) * GLA_DV]
        parts.append(oh * lax.rsqrt(jnp.mean(oh * oh, axis=-1, keepdims=True) + EPS))
    r = r_ref[...]
    o_gla = (jnp.concatenate(parts, axis=1) * ggla_ref[...]) * (r * jax.nn.sigmoid(r))
    y5 = y5_ref[...]
    z = y5 * (0.5 * (1.0 + jnp.tanh(math.sqrt(2.0 / math.pi) * (y5 + 0.044715 * (y5 * y5 * y5)))))
    o_s5 = z * jax.nn.sigmoid(_dot(z.astype(BF16), wglu_ref[...]) + bglu_ref[...])
    att = (_dot(o_gla.astype(BF16), wout_ref[0:GLA_WIDTH, :])
           + _dot(o_s5.astype(BF16), wout_ref[GLA_WIDTH:GLA_WIDTH + S5_WIDTH, :]))
    h1 = x_ref[...] + att
    h1_ref[...] = h1
    hn = _rms(h1, gffn_ref[...])
    hn_ref[...] = _pack_bf16_pairs(hn)
    hn_hi, hn_lo = _split2(hn)
    both = _dot(hn_hi, wr_ref[...])
    logits = (both[:, :ROUTER_LANES] + both[:, ROUTER_LANES:] + _dot(hn_lo, wr_ref[:, :ROUTER_LANES])
              + br_ref[...])
    lg_ref[...] = _route_tile(logits)


def _route_tile(logits):
    lt = logits.T[:ROUTE_ROWS]
    row = lax.broadcasted_iota(jnp.int32, lt.shape, 0)
    ninf = float('-inf')

    def first_max(vals):
        m = jnp.max(vals, axis=0, keepdims=True)
        return m, jnp.min(jnp.where(vals == m, row, ROUTE_ROWS), axis=0, keepdims=True)

    is_group = row < N_GROUPS
    gmax, gsel = first_max(jnp.where(is_group, lt, ninf))
    p_group = 1.0 / jnp.sum(jnp.where(is_group, jnp.exp(lt - gmax), 0.0), axis=0, keepdims=True)
    first = N_GROUPS + gsel * EXPERTS_PER_GROUP
    in_group = jnp.logical_and(row >= first, row < first + EXPERTS_PER_GROUP)
    cand = jnp.where(in_group, lt, ninf)
    m1, i1 = first_max(cand)
    m2, i2 = first_max(jnp.where(row == i1, ninf, cand))
    t = jnp.exp(m2 - m1)
    p1 = 1.0 / (1.0 + t)
    out_row = lax.broadcasted_iota(jnp.int32, (8, lt.shape[1]), 0)
    out = jnp.where(out_row == 0, (i1 - N_GROUPS).astype(F32), 0.0)
    out = jnp.where(out_row == 1, (i2 - N_GROUPS).astype(F32), out)
    out = jnp.where(out_row == 2, p_group * p1, out)
    return jnp.where(out_row == 3, p_group * (t * p1), out)


def _post_mix(rows_a, rows_b, g_gla, w_glu, b_glu, w_out, g_ffn, wr_pair, b_r, tm):
    na, nb = rows_a[0].shape[0], rows_b[0].shape[0]
    n_first = na // tm
    widths = [D_MODEL, GLA_WIDTH, GLA_WIDTH, S5_WIDTH]
    spec_a = [pl.BlockSpec((tm, w), lambda i: (jnp.minimum(i, n_first - 1), 0)) for w in widths]
    spec_b = [pl.BlockSpec((tm, w), lambda i: (jnp.maximum(i - n_first, 0), 0)) for w in widths]
    full = lambda a: pl.BlockSpec(a.shape, lambda i: (0,) * a.ndim)
    row = lambda w: pl.BlockSpec((tm, w), lambda i: (i, 0))
    weights = [g_gla, w_glu, b_glu, w_out, g_ffn, wr_pair, b_r]
    n = na + nb
    return pl.pallas_call(
        functools.partial(_post_mix_kernel, n_first=n_first),
        grid=(n // tm,),
        in_specs=spec_a + spec_b + [full(a) for a in weights],
        out_specs=[row(D_MODEL), row(D_MODEL // 2), pl.BlockSpec((8, tm), lambda i: (0, i))],
        out_shape=[jax.ShapeDtypeStruct((n, D_MODEL), F32), jax.ShapeDtypeStruct((n, D_MODEL // 2), jnp.uint32),
                   jax.ShapeDtypeStruct((8, n), F32)],
        compiler_params=_params(("parallel",)),
        name="post_mix",
    )(*rows_a, *rows_b, *weights)


def _gather_start(idx_ref, idx_base, idx_stride, src_hbm, dst, sem, n):
    def body(r, carry):
        row = idx_ref[idx_base + r * idx_stride]
        pltpu.make_async_copy(src_hbm.at[pl.ds(row, 1)], dst.at[pl.ds(r, 1)], sem).start()
        return carry
    lax.fori_loop(0, n, body, 0, unroll=8)


def _pack_bf16_pairs(x):
    half = x.shape[1] // 2
    bits = lax.bitcast_convert_type(x.astype(BF16).astype(F32), jnp.uint32)
    return bits[:, half:] | (bits[:, :half] >> 16)


def _unpack_bf16_pairs(p):
    lo = lax.bitcast_convert_type(p << 16, F32).astype(BF16)
    hi = lax.bitcast_convert_type(p & jnp.uint32(0xFFFF0000), F32).astype(BF16)
    return jnp.concatenate([lo, hi], axis=1)


def _gather_wait(src_hbm, dst, sem, n):
    pltpu.make_async_copy(src_hbm.at[pl.ds(0, n)], dst.at[pl.ds(0, n)], sem).wait()


def _dispatch_rows_kernel(dest_ref, fill_ref, x_ref, xs_hbm, buf, sem, zsem, *, tm):
    i = pl.program_id(0)
    slot = i % 2

    def row_copy(r, k, s):
        row = dest_ref[(i * tm + r) * TOP_K + k]
        return pltpu.make_async_copy(buf.at[s, pl.ds(r, 1)], xs_hbm.at[pl.ds(row, 1)], sem.at[s])

    def wait_slot(s):
        for _ in range(TOP_K):
            pltpu.make_async_copy(buf.at[s], xs_hbm.at[pl.ds(0, tm)], sem.at[s]).wait()

    @pl.when(i == 0)
    def _zero_fill():
        buf[1] = jnp.zeros(buf.shape[1:], buf.dtype)

        def fill(n, carry):
            @pl.when(fill_ref[n] >= 0)
            def _():
                pltpu.make_async_copy(buf.at[1, pl.ds(0, MOE_TM)], xs_hbm.at[pl.ds(fill_ref[n] * MOE_TM, MOE_TM)],
                                      zsem).start()
            return carry
        lax.fori_loop(0, 2 * N_EXPERTS, fill, 0)

        def drain(n, carry):
            @pl.when(fill_ref[n] >= 0)
            def _():
                pltpu.make_async_copy(buf.at[1, pl.ds(0, MOE_TM)], xs_hbm.at[pl.ds(0, MOE_TM)], zsem).wait()
            return carry
        lax.fori_loop(0, 2 * N_EXPERTS, drain, 0)

    @pl.when(i >= 2)
    def _reuse():
        wait_slot(slot)

    buf[slot] = x_ref[...]

    def issue(r, carry):
        for k in range(TOP_K):
            row_copy(r, k, slot).start()
        return carry
    lax.fori_loop(0, tm, issue, 0, unroll=4)

    @pl.when(i == pl.num_programs(0) - 1)
    def _finish():
        wait_slot(slot)

        @pl.when(i >= 1)
        def _():
            wait_slot(1 - slot)


def _dispatch_rows(dest, fill_blocks, hn, n_rows, tm):
    n = hn.shape[0]
    assert tm >= MOE_TM
    return pl.pallas_call(
        functools.partial(_dispatch_rows_kernel, tm=tm),
        grid_spec=pltpu.PrefetchScalarGridSpec(
            num_scalar_prefetch=2,
            grid=(n // tm,),
            in_specs=[pl.BlockSpec((tm, D_MODEL // 2), lambda i, d, lb: (i, 0))],
            out_specs=pl.BlockSpec(memory_space=pl.ANY),
            scratch_shapes=[pltpu.VMEM((2, tm, D_MODEL // 2), jnp.uint32), pltpu.SemaphoreType.DMA((2,)),
                            pltpu.SemaphoreType.DMA],
        ),
        out_shape=jax.ShapeDtypeStruct((n_rows, D_MODEL // 2), jnp.uint32),
        compiler_params=_params(("arbitrary",)),
        name="dispatch_rows",
    )(dest, fill_blocks, hn)


def _moe_kernel(bexp_ref, eord_ref, next_ref, nused_ref, x_ref, wg_hbm, wu_hbm, wd_hbm, o_ref,
                wg_st, wu_st, wd_st, wsem, wg_bf, wu_bf, wd_bf):
    b = pl.program_id(0)
    nu = nused_ref[0]
    e = bexp_ref[b]
    new_expert = jnp.logical_or(b == 0, bexp_ref[jnp.maximum(b - 1, 0)] != e)

    def weight_copies(expert, slot):
        return [pltpu.make_async_copy(src.at[expert], dst.at[slot], wsem.at[slot])
                for src, dst in ((wg_hbm, wg_st), (wu_hbm, wu_st), (wd_hbm, wd_st))]

    @pl.when(b == 0)
    def _prologue():
        for c in weight_copies(e, 0):
            c.start()

    @pl.when(jnp.logical_and(b < nu, new_expert))
    def _new_expert():
        slot = eord_ref[b] % 2
        for c in weight_copies(e, slot):
            c.wait()
        nxt = next_ref[b]

        @pl.when(nxt >= 0)
        def _prefetch():
            for c in weight_copies(nxt, 1 - slot):
                c.start()

        wg_bf[...] = wg_st[slot].astype(BF16)
        wu_bf[...] = wu_st[slot].astype(BF16)
        wd_bf[...] = wd_st[slot].astype(BF16)

    @pl.when(b < nu)
    def _run():
        x = _unpack_bf16_pairs(x_ref[...])
        gate = _dot(x, wg_bf[...])
        up = _dot(x, wu_bf[...])
        hid = ((gate * jax.nn.sigmoid(gate)) * up).astype(BF16)
        o_ref[...] = _dot(hid, wd_bf[...])

    @pl.when(b >= nu)
    def _skip():
        o_ref[...] = jnp.zeros(o_ref.shape, o_ref.dtype)


def _moe(block_exp, block_ord, block_next, n_used, xs, w_gate, w_up, w_down):
    nblk = xs.shape[0] // MOE_TM
    any_spec = pl.BlockSpec(memory_space=pl.ANY)
    return pl.pallas_call(
        _moe_kernel,
        grid_spec=pltpu.PrefetchScalarGridSpec(
            num_scalar_prefetch=4,
            grid=(nblk,),
            in_specs=[pl.BlockSpec((MOE_TM, D_MODEL // 2), lambda b, be, eo, nx, nu: (jnp.minimum(b, nu[0] - 1), 0)),
                      any_spec, any_spec, any_spec],
            out_specs=pl.BlockSpec((MOE_TM, D_MODEL), lambda b, *_: (b, 0)),
            scratch_shapes=[pltpu.VMEM((2, D_MODEL, EXPERT_HIDDEN), F32), pltpu.VMEM((2, D_MODEL, EXPERT_HIDDEN), F32),
                            pltpu.VMEM((2, EXPERT_HIDDEN, D_MODEL), F32), pltpu.SemaphoreType.DMA((2,)),
                            pltpu.VMEM((D_MODEL, EXPERT_HIDDEN), BF16), pltpu.VMEM((D_MODEL, EXPERT_HIDDEN), BF16),
                            pltpu.VMEM((EXPERT_HIDDEN, D_MODEL), BF16)],
        ),
        out_shape=jax.ShapeDtypeStruct((nblk * MOE_TM, D_MODEL), F32),
        compiler_params=_params(("arbitrary",)),
        name="moe",
    )(block_exp, block_ord, block_next, n_used, xs, w_gate, w_up, w_down)


def _final_kernel(dest_ref, h1_ref, wa_ref, wb_ref, g_ref, rows_hbm, y_ref, ybuf, sem, *, tm, tok_off):
    i = pl.program_id(0)

    def start(tile):
        slot = tile % 2
        base = (tok_off + tile * tm) * TOP_K
        for k in range(TOP_K):
            _gather_start(dest_ref, base + k, TOP_K, rows_hbm, ybuf.at[slot, k], sem.at[slot], tm)

    @pl.when(i == 0)
    def _first():
        start(i)

    @pl.when(i + 1 < pl.num_programs(0))
    def _next():
        start(i + 1)

    slot = i % 2
    for k in range(TOP_K):
        _gather_wait(rows_hbm, ybuf.at[slot, k], sem.at[slot], tm)
    moe = ybuf[slot, 0] * wa_ref[...] + ybuf[slot, 1] * wb_ref[...]
    y_ref[...] = _rms(h1_ref[...] + moe, g_ref[...])


def _final(dest, h1, wa, wb, g_final, rows, *, tm, n, row_off):
    off = row_off // tm
    row = lambda w: pl.BlockSpec((tm, w), lambda i, d: (i + off, 0))
    return pl.pallas_call(
        functools.partial(_final_kernel, tm=tm, tok_off=row_off),
        grid_spec=pltpu.PrefetchScalarGridSpec(
            num_scalar_prefetch=1,
            grid=(n // tm,),
            in_specs=[row(D_MODEL), row(1), row(1), pl.BlockSpec((1, D_MODEL), lambda i, d: (0, 0)),
                      pl.BlockSpec(memory_space=pl.ANY)],
            out_specs=pl.BlockSpec((tm, D_MODEL), lambda i, d: (i, 0)),
            scratch_shapes=[pltpu.VMEM((2, TOP_K, tm, D_MODEL), F32), pltpu.SemaphoreType.DMA((2,))],
        ),
        out_shape=jax.ShapeDtypeStruct((n, D_MODEL), F32),
        compiler_params=_params(("arbitrary",)),
        name="final",
    )(dest, h1, wa, wb, g_final, rows)


def _dispatch(eid):
    t = eid.shape[0]
    a = t * TOP_K
    assert a % MOE_TM == 0
    nblk = a // MOE_TM + N_EXPERTS
    flat = eid.reshape(-1)
    experts = jnp.arange(N_EXPERTS, dtype=jnp.int32)
    onehot = (flat[:, None] == experts[None, :]).astype(F32).reshape(a // MOE_TM, MOE_TM, N_EXPERTS)
    strict_lower = jnp.tril(jnp.ones((MOE_TM, MOE_TM), F32), -1)
    within = jnp.einsum('ij,bjk->bik', strict_lower, onehot)
    totals = jnp.sum(onehot, axis=1)
    before = jnp.cumsum(totals, axis=0) - totals
    counts = jnp.sum(totals, axis=0).astype(jnp.int32)
    padded = (counts + MOE_TM - 1) // MOE_TM * MOE_TM
    pends = jnp.cumsum(padded)
    offset = before + (pends - padded).astype(F32)[None, :]
    dest = jnp.sum((within + offset[:, None, :]) * onehot, axis=-1).reshape(-1)
    n_used = (pends[-1] // MOE_TM).astype(jnp.int32)
    blk = jnp.minimum(jnp.arange(nblk, dtype=jnp.int32), n_used - 1)
    block_exp = jnp.minimum(jnp.searchsorted(pends, blk * MOE_TM, side='right'), N_EXPERTS - 1).astype(jnp.int32)
    in_use = counts > 0
    ordinal = jnp.cumsum(in_use.astype(jnp.int32)) - 1
    later = lax.cummin(jnp.where(in_use, experts, N_EXPERTS), axis=0, reverse=True)
    nxt = jnp.concatenate([later[1:], jnp.full((1,), N_EXPERTS, jnp.int32)])
    nxt = jnp.where(nxt < N_EXPERTS, nxt, -1)
    last_block = jnp.where(in_use, pends // MOE_TM - 1, -1)
    unused = n_used + experts
    fill_blocks = jnp.concatenate([last_block, jnp.where(unused < nblk, unused, -1)]).astype(jnp.int32)
    return (dest.astype(jnp.int32), nblk * MOE_TM, block_exp, ordinal[block_exp], nxt[block_exp], fill_blocks,
            n_used.reshape(1))


def kernel(x_prompt, x_sample, state_gla, state_s5_re, state_s5_im, meta, g_mix, w_in, w_gk2, b_gk, g_gla,
           lam_re, lam_im, log_dt, s5_b_re, s5_b_im, s5_c_re, s5_c_im, d_skip, w_glu, b_glu, w_out, g_ffn,
           w_rg, b_rg, w_re, b_re, w_gate, w_up, w_down, g_final):
    bp, tp, _ = x_prompt.shape
    bs, ts, _ = x_sample.shape
    l = 0

    w_cat = w_in[l].astype(BF16)
    wgk_hi, wgk_lo = _split2(jnp.pad(w_gk2[l], ((0, 128 - GLA_RANK), (0, 0))))
    g_mix2, b_gk2 = g_mix[l][None], b_gk[l][None]
    mats = _s5_matrices(lam_re[l], lam_im[l], log_dt[l], s5_b_re[l], s5_b_im[l], s5_c_re[l], s5_c_im[l],
                        d_skip[l])
    w_router = jnp.concatenate([w_rg[l], jnp.moveaxis(w_re[l], 0, 1).reshape(D_MODEL, N_EXPERTS)], axis=1)
    w_router = jnp.pad(w_router, ((0, 0), (0, ROUTER_LANES - N_GROUPS - N_EXPERTS)))
    wr_pair = jnp.concatenate(_split2(w_router), axis=1)
    b_router = jnp.pad(jnp.concatenate([b_rg[l], b_re[l].reshape(-1)]),
                       (0, ROUTER_LANES - N_GROUPS - N_EXPERTS))[None]
    w_glu_b, w_out_b = w_glu[l].astype(BF16), w_out[l].astype(BF16)
    g_gla2, b_glu2, g_ffn2 = g_gla[l].reshape(1, GLA_WIDTH), b_glu[l][None], g_ffn[l][None]

    proj = functools.partial(_in_proj, g_mix=g_mix2, w_cat=w_cat, wgk_hi=wgk_hi, wgk_lo=wgk_lo, b_gk=b_gk2)

    qm, km, vm, _, gm, um = proj(meta, tm=N_META)
    zero_s = jnp.zeros((1, GLA_HEADS, GLA_DK, GLA_DV), F32)
    _, s_meta = _gla(qm, km, vm, gm, zero_s, seq_len=N_META, chunk=N_META)
    zero_h = jnp.zeros((1, S5_NGB, 1, S5_SL), F32)
    _, hm_re, hm_im = _s5_long(um, mats, zero_h, zero_h, seq_len=N_META)

    xp = x_prompt.reshape(bp * tp, D_MODEL)
    qp, kp, vp, rp, gp, up = proj(xp, tm=512)
    op, gla_p = _gla(qp, kp, vp, gp, jnp.broadcast_to(s_meta, (bp,) + s_meta.shape[1:]),
                     seq_len=tp, chunk=GLA_CHUNK)
    y5p, hp_re, hp_im = _s5_pipelined(up, mats, jnp.broadcast_to(hm_re, (bp,) + hm_re.shape[1:]),
                                      jnp.broadcast_to(hm_im, (bp,) + hm_im.shape[1:]), seq_len=tp)

    xs = x_sample.reshape(bs * ts, D_MODEL)
    qs, ks, vs, rs, gs, us = proj(xs, tm=512)
    chunk_s = GLA_CHUNK if ts % GLA_CHUNK == 0 else ts
    os_, gla_s = _gla(qs, ks, vs, gs, state_gla[l], seq_len=ts, chunk=chunk_s)
    y5s, hs_re, hs_im = _s5_short(us, mats, state_s5_re[l].reshape(bs, -1), state_s5_im[l].reshape(bs, -1),
                                  seq_len=ts)

    np_rows, ns_rows = bp * tp, bs * ts
    h1, hn, route = _post_mix((xp, op, rp, y5p), (xs, os_, rs, y5s), g_gla2, w_glu_b, b_glu2, w_out_b, g_ffn2,
                               wr_pair, b_router, tm=256)

    eid = route[:TOP_K].T.astype(jnp.int32)
    wts = route[TOP_K:2 * TOP_K].T
    dest, n_sorted, block_exp, block_ord, block_next, fill_blocks, n_used = _dispatch(eid)
    xs_rows = _dispatch_rows(dest, fill_blocks, hn, n_sorted, tm=512)
    out_rows = _moe(block_exp, block_ord, block_next, n_used, xs_rows, w_gate[l], w_up[l], w_down[l])
    fin = functools.partial(_final, dest, h1, wts[:, 0:1], wts[:, 1:2], g_final[None], out_rows, tm=512)
    y_prompt = fin(n=np_rows, row_off=0)
    y_sample = fin(n=ns_rows, row_off=np_rows)

    return (y_prompt.reshape(bp, tp, D_MODEL), y_sample.reshape(bs, ts, D_MODEL),
            gla_p[None], hp_re.reshape(1, bp, S5_GROUPS, S5_STATE), hp_im.reshape(1, bp, S5_GROUPS, S5_STATE),
            gla_s[None], hs_re.reshape(1, bs, S5_GROUPS, S5_STATE), hs_im.reshape(1, bs, S5_GROUPS, S5_STATE))
```

```python
import functools
import math

import jax
import jax.numpy as jnp
from jax import lax
from jax.experimental import pallas as pl
from jax.experimental.pallas import tpu as pltpu

F32 = jnp.float32
BF16 = jnp.bfloat16

D_MODEL = 2048
N_META = 16
GLA_HEADS = 4
GLA_DK = 128
GLA_DV = 256
GLA_KEY_WIDTH = GLA_HEADS * GLA_DK
GLA_WIDTH = GLA_HEADS * GLA_DV
GLA_RANK = 16
GLA_GATE_NORM = 16.0
GLA_CHUNK = 64
GLA_SUB = 16
GLA_SEQS_PER_STEP = 8
MASKED_EXPONENT = -1e30
S5_WIDTH = 1024
S5_GROUP = 16
S5_GROUPS = 64
S5_STATE = 64
S5_GB = 8
S5_NGB = S5_GROUPS // S5_GB
S5_UL = S5_GB * S5_GROUP
S5_SL = S5_GB * S5_STATE
S5_LT = 2 * S5_SL // 128
S5_UNROLL = 4
S5_PIPE_CHUNKS = 1
N_GROUPS = 4
EXPERTS_PER_GROUP = 8
N_EXPERTS = N_GROUPS * EXPERTS_PER_GROUP
EXPERT_HIDDEN = 512
TOP_K = 2
EPS = 1e-6
ROUTER_LANES = 128
ROUTE_ROWS = -(-(N_GROUPS + N_EXPERTS) // 8) * 8
MOE_TM = 256
VMEM_LIMIT = 56 * 1024 * 1024

_dot = functools.partial(jnp.dot, preferred_element_type=F32)


def _split2(x):
    hi = x.astype(BF16)
    lo = (x - hi.astype(F32)).astype(BF16)
    return hi, lo


def _rms(x, g):
    return x * lax.rsqrt(jnp.mean(x * x, axis=-1, keepdims=True) + EPS) * g


def _params(sem):
    return pltpu.CompilerParams(dimension_semantics=sem, vmem_limit_bytes=VMEM_LIMIT)


def _in_proj_kernel(x_ref, g_ref, w_ref, wgk_hi_ref, wgk_lo_ref, bgk_ref,
                    q_ref, k_ref, v_ref, r_ref, gk_ref, u_ref):
    xb = _rms(x_ref[...], g_ref[...]).astype(BF16)
    kw = GLA_KEY_WIDTH
    q_ref[...] = _dot(xb, w_ref[:, 0:kw]) * (GLA_DK ** -0.5)
    k_ref[...] = _dot(xb, w_ref[:, kw:2 * kw])
    v_ref[...] = _dot(xb, w_ref[:, 2 * kw:2 * kw + GLA_WIDTH])
    r_ref[...] = _dot(xb, w_ref[:, 2 * kw + GLA_WIDTH:2 * kw + 2 * GLA_WIDTH])
    c0 = 2 * kw + 2 * GLA_WIDTH
    tail = _dot(xb, w_ref[:, c0:])
    u_ref[...] = tail[:, GLA_RANK:GLA_RANK + S5_WIDTH]
    a_low = tail[:, :128]
    a_hi, a_lo = _split2(a_low)
    z = (_dot(a_hi, wgk_hi_ref[...]) + _dot(a_hi, wgk_lo_ref[...]) + _dot(a_lo, wgk_hi_ref[...])
         + bgk_ref[...])
    gk_ref[...] = (jnp.minimum(z, 0.0) - jnp.log1p(jnp.exp(-jnp.abs(z)))) * (1.0 / GLA_GATE_NORM)


def _in_proj(x, g_mix, w_cat, wgk_hi, wgk_lo, b_gk, tm):
    n = x.shape[0]
    wcols = w_cat.shape[1]
    row = lambda w: pl.BlockSpec((tm, w), lambda i: (i, 0))
    full = lambda a: pl.BlockSpec(a.shape, lambda i: (0,) * a.ndim)
    widths = [GLA_KEY_WIDTH, GLA_KEY_WIDTH, GLA_WIDTH, GLA_WIDTH, GLA_KEY_WIDTH, S5_WIDTH]
    return pl.pallas_call(
        _in_proj_kernel,
        grid=(n // tm,),
        in_specs=[row(D_MODEL), full(g_mix),
                  pl.BlockSpec((D_MODEL, wcols), lambda i: (0, 0), pipeline_mode=pl.Buffered(1)),
                  full(wgk_hi), full(wgk_lo), full(b_gk)],
        out_specs=[row(w) for w in widths],
        out_shape=[jax.ShapeDtypeStruct((n, w), F32) for w in widths],
        compiler_params=_params(("parallel",)),
        name="in_proj",
    )(x, g_mix, w_cat, wgk_hi, wgk_lo, b_gk)


def _gla_chunk(q_ref, k_ref, v_ref, g_ref, o_ref, r0, C, consts, read_state, write_state):
    tril, piece_rows, piece_lanes = consts
    sub = min(C, GLA_SUB)
    nsub = C // sub
    g = g_ref[pl.ds(r0, C), :]
    g1 = g.astype(BF16)
    rem = g - g1.astype(F32)
    g2 = rem.astype(BF16)
    g3 = (rem - g2.astype(F32)).astype(BF16)
    b_all = _dot(tril, g1) + _dot(tril, g2) + _dot(tril, g3)
    for h in range(GLA_HEADS):
        ks = slice(h * GLA_DK, (h + 1) * GLA_DK)
        vs = slice(h * GLA_DV, (h + 1) * GLA_DV)
        q = q_ref[pl.ds(r0, C), ks]
        k = k_ref[pl.ds(r0, C), ks]
        b = b_all[:, ks]
        s_prev = read_state(h)
        vb = v_ref[pl.ds(r0, C), vs].astype(BF16)
        o_inter = _dot((q * jnp.exp(b)).astype(BF16), s_prev.astype(BF16))
        blocks = []
        for s in range(nsub):
            lo = s * sub
            bs, qs, ksub = b[lo:lo + sub], q[lo:lo + sub], k[lo:lo + sub]
            acc = o_inter[lo:lo + sub]
            if s > 0:
                anchor = b[lo - 1:lo]
                qd = (qs * jnp.exp(bs - anchor)).astype(BF16)
                kd = (k[:lo] * jnp.exp(anchor - b[:lo])).astype(BF16)
                sc = lax.dot_general(qd, kd, (((1,), (1,)), ((), ())), preferred_element_type=F32)
                acc = acc + _dot(sc.astype(BF16), vb[:lo])
            pieces = [jnp.zeros((8, 128), F32) for _ in range(sub // 8)]
            for jj in range(sub):
                for p in range(jj // 8, sub // 8):
                    r8 = slice(8 * p, 8 * p + 8)
                    diff = bs[r8] - bs[jj:jj + 1]
                    if 8 * p < jj:
                        diff = jnp.where(piece_rows + 8 * p >= jj, diff, MASKED_EXPONENT)
                    col = jnp.sum((qs[r8] * ksub[jj:jj + 1]) * jnp.exp(diff), axis=-1, keepdims=True)
                    pieces[p] = jnp.where(piece_lanes == jj, col, pieces[p])
            scores = jnp.concatenate(pieces, axis=0) if len(pieces) > 1 else pieces[0]
            acc = acc + _dot(scores[:, :sub].astype(BF16), vb[lo:lo + sub])
            blocks.append(acc)
        o_ref[pl.ds(r0, C), vs] = jnp.concatenate(blocks, axis=0) if nsub > 1 else blocks[0]
        b_last = b[C - 1:C]
        kdec = (k * jnp.exp(b_last - b)).astype(BF16)
        upd = lax.dot_general(kdec, vb, (((0,), (0,)), ((), ())), preferred_element_type=F32)
        dcol = jnp.broadcast_to(jnp.exp(b_last), (GLA_DK, GLA_DK)).T
        write_state(h, s_prev * jnp.concatenate([dcol, dcol], axis=1) + upd)


def _gla_consts(C):
    sub = min(C, GLA_SUB)
    ri = lax.broadcasted_iota(jnp.int32, (C, C), 0)
    ci = lax.broadcasted_iota(jnp.int32, (C, C), 1)
    tril = jnp.where(ri >= ci, 1.0, 0.0).astype(BF16)
    piece_rows = lax.broadcasted_iota(jnp.int32, (8, GLA_DK), 0)
    piece_lanes = lax.broadcasted_iota(jnp.int32, (8, 128), 1)
    return tril, piece_rows, piece_lanes


def _gla_long_kernel(q_ref, k_ref, v_ref, g_ref, s0_ref, o_ref, sout_ref, s_scr, *, chunk, n_inner):
    j = pl.program_id(1)
    consts = _gla_consts(chunk)

    @pl.when(j == 0)
    def _init():
        s_scr[...] = s0_ref[0]

    def write_state(h, s):
        s_scr[h] = s

    def chunk_body(c, carry):
        _gla_chunk(q_ref, k_ref, v_ref, g_ref, o_ref, pl.multiple_of(c * chunk, chunk), chunk, consts,
                   lambda h: s_scr[h], write_state)
        return carry

    lax.fori_loop(0, n_inner, chunk_body, 0)

    @pl.when(j == pl.num_programs(1) - 1)
    def _fin():
        sout_ref[0] = s_scr[...]


def _gla_short_kernel(q_ref, k_ref, v_ref, g_ref, s0_ref, o_ref, sout_ref, *, chunk, nb):
    consts = _gla_consts(chunk)

    def seq_body(n, carry):
        def write_state(h, s):
            sout_ref[n, h] = s

        _gla_chunk(q_ref, k_ref, v_ref, g_ref, o_ref, pl.multiple_of(n * chunk, chunk), chunk, consts,
                   lambda h: s0_ref[n, h], write_state)
        return carry

    lax.fori_loop(0, nb, seq_body, 0, unroll=2 if nb % 2 == 0 else 1)


def _gla(q, k, v, g, s0, *, seq_len, chunk):
    nseq = s0.shape[0]
    n = q.shape[0]
    out_shape = [jax.ShapeDtypeStruct((n, GLA_WIDTH), F32),
                 jax.ShapeDtypeStruct((nseq, GLA_HEADS, GLA_DK, GLA_DV), F32)]
    if seq_len == chunk:
        nb = min(nseq, GLA_SEQS_PER_STEP)
        rows = lambda w: pl.BlockSpec((nb * chunk, w), lambda s: (s, 0))
        st = pl.BlockSpec((nb, GLA_HEADS, GLA_DK, GLA_DV), lambda s: (s, 0, 0, 0))
        return pl.pallas_call(
            functools.partial(_gla_short_kernel, chunk=chunk, nb=nb),
            grid=(nseq // nb,),
            in_specs=[rows(GLA_KEY_WIDTH), rows(GLA_KEY_WIDTH), rows(GLA_WIDTH), rows(GLA_KEY_WIDTH), st],
            out_specs=[rows(GLA_WIDTH), st],
            out_shape=out_shape,
            compiler_params=_params(("parallel",)),
            name="gla_short",
        )(q, k, v, g, s0)
    rb = min(seq_len, 4 * chunk)
    nblk = seq_len // rb
    rows = lambda w: pl.BlockSpec((rb, w), lambda s, j: (s * nblk + j, 0))
    st = pl.BlockSpec((1, GLA_HEADS, GLA_DK, GLA_DV), lambda s, j: (s, 0, 0, 0))
    return pl.pallas_call(
        functools.partial(_gla_long_kernel, chunk=chunk, n_inner=rb // chunk),
        grid=(nseq, nblk),
        in_specs=[rows(GLA_KEY_WIDTH), rows(GLA_KEY_WIDTH), rows(GLA_WIDTH), rows(GLA_KEY_WIDTH), st],
        out_specs=[rows(GLA_WIDTH), st],
        out_shape=out_shape,
        scratch_shapes=[pltpu.VMEM((GLA_HEADS, GLA_DK, GLA_DV), F32)],
        compiler_params=_params(("parallel", "arbitrary")),
        name="gla_long",
    )(q, k, v, g, s0)


def _cmul(ar, ai, br, bi):
    return ar * br - ai * bi, ar * bi + ai * br


def _cpow(ar, ai, n):
    res = None
    while n:
        if n & 1:
            res = (ar, ai) if res is None else _cmul(res[0], res[1], ar, ai)
        n >>= 1
        if n:
            ar, ai = _cmul(ar, ai, ar, ai)
    return res


def _s5_segment_len(seq_len):
    s = -(-seq_len // 8)
    s = -(-s // 4) * 4
    return s if (s // 4) % 2 == 1 else s + 4


def _st_store(st_scr, rows, val):
    for t in range(S5_LT):
        st_scr[t, rows, :] = val[:, t * 128:(t + 1) * 128]


def _st_load(st_scr, rows):
    return jnp.concatenate([st_scr[t, rows, :] for t in range(S5_LT)], axis=1)


def _s5_long_kernel(u_ref, bb_ref, cc_ref, dsk_ref, are_ref, aim_ref, h0re_ref, h0im_ref,
                    y_ref, hre_ref, him_ref, st_scr, *, seq_len, seg):
    T, S, SL = seq_len, seg, S5_SL
    rc = min(T, 512)
    for c in range(T // rc):
        sl = slice(c * rc, (c + 1) * rc)
        _st_store(st_scr, sl, _dot(u_ref[sl, :].astype(BF16), bb_ref[0]))
    if 8 * S > T:
        _st_store(st_scr, slice(T, 8 * S), jnp.zeros((8 * S - T, 2 * SL), F32))
    a_re, a_im = are_ref[0], aim_ref[0]
    ar = jnp.broadcast_to(a_re, (8, SL))
    ai = jnp.broadcast_to(a_im, (8, SL))

    def step(i, h, store):
        strand = pl.ds(i, 8, stride=S)
        x = _st_load(st_scr, strand)
        mr, mi = _cmul(ar, ai, h[0], h[1])
        nr, ni = mr + x[:, :SL], mi + x[:, SL:]
        if store:
            _st_store(st_scr, strand, jnp.concatenate([nr, ni], axis=1))
        return nr, ni

    zero = jnp.zeros((8, SL), F32)
    fr, fi = lax.fori_loop(0, S, functools.partial(step, store=False), (zero, zero), unroll=S5_UNROLL)

    car_r, car_i = _s5_carries(fr, fi, a_re, a_im, S, h0re_ref[0, 0], h0im_ref[0, 0])

    lax.fori_loop(0, S, functools.partial(step, store=True), (car_r, car_i), unroll=S5_UNROLL)

    last = _st_load(st_scr, slice(T - 1, T))
    hre_ref[0, 0] = last[:, :SL]
    him_ref[0, 0] = last[:, SL:]
    for c in range(T // rc):
        sl = slice(c * rc, (c + 1) * rc)
        y_ref[sl, :] = _dot(_st_load(st_scr, sl).astype(BF16), cc_ref[0]) + dsk_ref[0] * u_ref[sl, :]


def _s5_long(u, mats, h0_re, h0_im, *, seq_len):
    bb, cc, dsk, a_re, a_im = mats
    nseq = h0_re.shape[0]
    seg = _s5_segment_len(seq_len)
    gb3 = lambda shape: pl.BlockSpec((1,) + shape, lambda s, g: (g, 0, 0))
    st = pl.BlockSpec((1, 1, 1, S5_SL), lambda s, g: (s, g, 0, 0))
    urow = pl.BlockSpec((seq_len, S5_UL), lambda s, g: (s, g))
    st_shape = jax.ShapeDtypeStruct((nseq, S5_NGB, 1, S5_SL), F32)
    return pl.pallas_call(
        functools.partial(_s5_long_kernel, seq_len=seq_len, seg=seg),
        grid=(nseq, S5_NGB),
        in_specs=[urow, gb3((S5_UL, 2 * S5_SL)), gb3((2 * S5_SL, S5_UL)), gb3((1, S5_UL)),
                  gb3((1, S5_SL)), gb3((1, S5_SL)), st, st],
        out_specs=[urow, st, st],
        out_shape=[jax.ShapeDtypeStruct(u.shape, F32), st_shape, st_shape],
        scratch_shapes=[pltpu.VMEM((S5_LT, 8 * seg, 128), F32)],
        compiler_params=_params(("parallel", "parallel")),
        name="s5_long",
    )(u, bb, cc, dsk, a_re, a_im, h0_re, h0_im)


def _s5_pipe_step(b_buf, s_buf, c_buf, u16, c16, ub_ref, uc_ref, bb_ref, cc_ref, dsk_ref, are_ref, aim_ref,
                  h0re_ref, h0im_ref, y_ref, hre_ref, him_ref, *, seq_len, seg):
    T, S, SL = seq_len, seg, S5_SL
    rc, sc = T // S5_PIPE_CHUNKS, S // S5_PIPE_CHUNKS
    if 8 * S > T:
        _st_store(b_buf, slice(T, 8 * S), jnp.zeros((8 * S - T, 2 * SL), F32))
    a_re, a_im = are_ref[0], aim_ref[0]
    ar = jnp.broadcast_to(a_re, (8, SL))
    ai = jnp.broadcast_to(a_im, (8, SL))

    def scan_chunk(k, h, store):
        for d in range(sc):
            strand = pl.ds(k * sc + d, 8, stride=S)
            x = _st_load(s_buf, strand)
            mr, mi = _cmul(ar, ai, h[0], h[1])
            h = (mr + x[:, :SL], mi + x[:, SL:])
            if store:
                _st_store(s_buf, strand, jnp.concatenate(h, axis=1))
        return h

    u16[...] = ub_ref[...].astype(BF16)
    for c in range(S5_PIPE_CHUNKS):
        rows = slice(c * rc, (c + 1) * rc)
        c16[rows, :] = _st_load(c_buf, rows).astype(BF16)

    def end_states(k, h):
        rows = pl.ds(pl.multiple_of(k * rc, rc), rc)
        _st_store(b_buf, rows, _dot(u16[rows, :], bb_ref[0]))
        return scan_chunk(k, h, False)

    zero = jnp.zeros((8, SL), F32)
    fr, fi = lax.fori_loop(0, S5_PIPE_CHUNKS, end_states, (zero, zero))
    car_r, car_i = _s5_carries(fr, fi, a_re, a_im, S, h0re_ref[0, 0], h0im_ref[0, 0])

    def states(k, h):
        rows = pl.ds(pl.multiple_of(k * rc, rc), rc)
        y_ref[rows, :] = _dot(c16[rows, :], cc_ref[0]) + dsk_ref[0] * uc_ref[rows, :]
        return scan_chunk(k, h, True)

    lax.fori_loop(0, S5_PIPE_CHUNKS, states, (car_r, car_i))
    last = _st_load(s_buf, slice(T - 1, T))
    hre_ref[0, 0] = last[:, :SL]
    him_ref[0, 0] = last[:, SL:]


def _s5_carries(fr, fi, a_re, a_im, seg, h0_re, h0_im):
    as_re, as_im = _cpow(a_re, a_im, seg)
    rows = lax.broadcasted_iota(jnp.int32, fr.shape, 0)
    cr, ci = h0_re, h0_im
    car_r, car_i = jnp.zeros_like(fr), jnp.zeros_like(fi)
    for r in range(8):
        car_r = jnp.where(rows == r, cr, car_r)
        car_i = jnp.where(rows == r, ci, car_i)
        if r < 7:
            mr, mi = _cmul(as_re, as_im, cr, ci)
            cr, ci = mr + fr[r:r + 1], mi + fi[r:r + 1]
    return car_r, car_i


def _s5_pipe_kernel(*refs, seq_len, seg):
    bufs, u16, c16 = refs[-5:-2], refs[-2], refs[-1]
    refs = refs[:-5]
    t = pl.program_id(0)

    @pl.when(t == 0)
    def _init():
        for buf in bufs:
            buf[...] = jnp.zeros(buf.shape, buf.dtype)

    for r in range(3):
        @pl.when(t % 3 == r)
        def _rotation(r=r):
            _s5_pipe_step(bufs[r], bufs[(r + 2) % 3], bufs[(r + 1) % 3], u16, c16, *refs, seq_len=seq_len, seg=seg)


def _s5_pipelined(u, mats, h0_re, h0_im, *, seq_len):
    bb, cc, dsk, a_re, a_im = mats
    nseq = h0_re.shape[0]
    n_items = nseq * S5_NGB
    seg = _s5_segment_len(seq_len)
    assert seq_len % (8 * S5_PIPE_CHUNKS) == 0 and seg % S5_PIPE_CHUNKS == 0

    def item(lag):
        return lambda t: jnp.clip(t - lag, 0, n_items - 1)

    def per_gb(shape, lag):
        return pl.BlockSpec((1,) + shape, lambda t: (item(lag)(t) % S5_NGB, 0, 0))

    def rows(lag):
        return pl.BlockSpec((seq_len, S5_UL), lambda t: (item(lag)(t) // S5_NGB, item(lag)(t) % S5_NGB))

    st = pl.BlockSpec((1, 1, 1, S5_SL), lambda t: (item(1)(t) // S5_NGB, item(1)(t) % S5_NGB, 0, 0))
    st_shape = jax.ShapeDtypeStruct((nseq, S5_NGB, 1, S5_SL), F32)
    buf = pltpu.VMEM((S5_LT, 8 * seg, 128), F32)
    return pl.pallas_call(
        functools.partial(_s5_pipe_kernel, seq_len=seq_len, seg=seg),
        grid=(n_items + 2,),
        in_specs=[rows(0), rows(2), per_gb((S5_UL, 2 * S5_SL), 0), per_gb((2 * S5_SL, S5_UL), 2),
                  per_gb((1, S5_UL), 2), per_gb((1, S5_SL), 1), per_gb((1, S5_SL), 1), st, st],
        out_specs=[rows(2), st, st],
        out_shape=[jax.ShapeDtypeStruct(u.shape, F32), st_shape, st_shape],
        scratch_shapes=[buf, buf, buf, pltpu.VMEM((seq_len, S5_UL), BF16), pltpu.VMEM((seq_len, 2 * S5_SL), BF16)],
        compiler_params=_params(("arbitrary",)),
        name="s5_pipelined",
    )(u, u, bb, cc, dsk, a_re, a_im, h0_re, h0_im)


def _s5_short_kernel(u_ref, bb_ref, cc_ref, dsk_ref, are_ref, aim_ref, h0re_ref, h0im_ref,
                     y_ref, hre_ref, him_ref, st_scr, *, seq_len, nseq):
    SL = S5_SL
    _st_store(st_scr, slice(None), _dot(u_ref[...].astype(BF16), bb_ref[0]))
    ar = jnp.broadcast_to(are_ref[0], (nseq, SL))
    ai = jnp.broadcast_to(aim_ref[0], (nseq, SL))
    hr, hi = h0re_ref[...], h0im_ref[...]
    for t in range(seq_len):
        step = pl.ds(t, nseq, stride=seq_len)
        x = _st_load(st_scr, step)
        mr, mi = _cmul(ar, ai, hr, hi)
        hr, hi = mr + x[:, :SL], mi + x[:, SL:]
        _st_store(st_scr, step, jnp.concatenate([hr, hi], axis=1))
    hre_ref[...] = hr
    him_ref[...] = hi
    y_ref[...] = _dot(_st_load(st_scr, slice(None)).astype(BF16), cc_ref[0]) + dsk_ref[0] * u_ref[...]


def _s5_short(u, mats, h0_re, h0_im, *, seq_len):
    bb, cc, dsk, a_re, a_im = mats
    nseq = h0_re.shape[0]
    n = nseq * seq_len
    gb3 = lambda shape: pl.BlockSpec((1,) + shape, lambda g: (g, 0, 0))
    st = pl.BlockSpec((nseq, S5_SL), lambda g: (0, g))
    urow = pl.BlockSpec((n, S5_UL), lambda g: (0, g))
    st_shape = jax.ShapeDtypeStruct(h0_re.shape, F32)
    return pl.pallas_call(
        functools.partial(_s5_short_kernel, seq_len=seq_len, nseq=nseq),
        grid=(S5_NGB,),
        in_specs=[urow, gb3((S5_UL, 2 * S5_SL)), gb3((2 * S5_SL, S5_UL)), gb3((1, S5_UL)),
                  gb3((1, S5_SL)), gb3((1, S5_SL)), st, st],
        out_specs=[urow, st, st],
        out_shape=[jax.ShapeDtypeStruct(u.shape, F32), st_shape, st_shape],
        scratch_shapes=[pltpu.VMEM((S5_LT, n, 128), F32)],
        compiler_params=_params(("parallel",)),
        name="s5_short",
    )(u, bb, cc, dsk, a_re, a_im, h0_re, h0_im)


def _s5_matrices(lam_re, lam_im, log_dt, b_re, b_im, c_re, c_im, d_skip):
    dt = jnp.exp(log_dt)[:, None]
    mag = jnp.exp(lam_re * dt)
    ab_re, ab_im = mag * jnp.cos(lam_im * dt), mag * jnp.sin(lam_im * dt)
    den = lam_re * lam_re + lam_im * lam_im
    f_re = ((ab_re - 1.0) * lam_re + ab_im * lam_im) / den
    f_im = (ab_im * lam_re - (ab_re - 1.0) * lam_im) / den
    bb_re = f_re[..., None] * b_re - f_im[..., None] * b_im
    bb_im = f_re[..., None] * b_im + f_im[..., None] * b_re
    eye = jnp.eye(S5_GB, dtype=F32)

    def in_mat(m):
        m = m.reshape(S5_NGB, S5_GB, S5_STATE, S5_GROUP)
        return jnp.einsum('bgpc,gh->bgchp', m, eye).reshape(S5_NGB, S5_UL, S5_SL)

    def out_mat(m):
        m = m.reshape(S5_NGB, S5_GB, S5_GROUP, S5_STATE)
        return jnp.einsum('bgcp,gh->bgphc', m, eye).reshape(S5_NGB, S5_SL, S5_UL)

    bb = jnp.concatenate([in_mat(bb_re), in_mat(bb_im)], axis=2).astype(BF16)
    cc = jnp.concatenate([out_mat(c_re), out_mat(-c_im)], axis=1).astype(BF16)
    dsk = d_skip.reshape(S5_NGB, 1, S5_UL)
    return bb, cc, dsk, ab_re.reshape(S5_NGB, 1, S5_SL), ab_im.reshape(S5_NGB, 1, S5_SL)


def _post_mix_kernel(xa_ref, oa_ref, ra_ref, ya_ref, xb_ref, ob_ref, rb_ref, yb_ref, *rest, n_first):
    i = pl.program_id(0)

    @pl.when(i < n_first)
    def _first():
        _post_mix_tile(xa_ref, oa_ref, ra_ref, ya_ref, *rest)

    @pl.when(i >= n_first)
    def _second():
        _post_mix_tile(xb_ref, ob_ref, rb_ref, yb_ref, *rest)


def _post_mix_tile(x_ref, o_ref, r_ref, y5_ref, ggla_ref, wglu_ref, bglu_ref, wout_ref, gffn_ref,
                   wr_ref, br_ref, h1_ref, hn_ref, lg_ref):
    o = o_ref[...]
    parts = []
    for h in range(GLA_HEADS):
        oh = o[:, h * GLA_DV:(h + 1) * GLA_DV]
        parts.append(oh * lax.rsqrt(jnp.mean(oh * oh, axis=-1, keepdims=True) + EPS))
    r = r_ref[...]
    o_gla = (jnp.concatenate(parts, axis=1) * ggla_ref[...]) * (r * jax.nn.sigmoid(r))
    y5 = y5_ref[...]
    z = y5 * (0.5 * (1.0 + jnp.tanh(math.sqrt(2.0 / math.pi) * (y5 + 0.044715 * (y5 * y5 * y5)))))
    o_s5 = z * jax.nn.sigmoid(_dot(z.astype(BF16), wglu_ref[...]) + bglu_ref[...])
    att = (_dot(o_gla.astype(BF16), wout_ref[0:GLA_WIDTH, :])
           + _dot(o_s5.astype(BF16), wout_ref[GLA_WIDTH:GLA_WIDTH + S5_WIDTH, :]))
    h1 = x_ref[...] + att
    h1_ref[...] = h1
    hn = _rms(h1, gffn_ref[...])
    hn_ref[...] = _pack_bf16_pairs(hn)
    hn_hi, hn_lo = _split2(hn)
    both = _dot(hn_hi, wr_ref[...])
    logits = (both[:, :ROUTER_LANES] + both[:, ROUTER_LANES:] + _dot(hn_lo, wr_ref[:, :ROUTER_LANES])
              + br_ref[...])
    lg_ref[...] = _route_tile(logits)


def _route_tile(logits):
    lt = logits.T[:ROUTE_ROWS]
    row = lax.broadcasted_iota(jnp.int32, lt.shape, 0)
    ninf = float('-inf')

    def first_max(vals):
        m = jnp.max(vals, axis=0, keepdims=True)
        return m, jnp.min(jnp.where(vals == m, row, ROUTE_ROWS), axis=0, keepdims=True)

    is_group = row < N_GROUPS
    gmax, gsel = first_max(jnp.where(is_group, lt, ninf))
    p_group = 1.0 / jnp.sum(jnp.where(is_group, jnp.exp(lt - gmax), 0.0), axis=0, keepdims=True)
    first = N_GROUPS + gsel * EXPERTS_PER_GROUP
    in_group = jnp.logical_and(row >= first, row < first + EXPERTS_PER_GROUP)
    cand = jnp.where(in_group, lt, ninf)
    m1, i1 = first_max(cand)
    m2, i2 = first_max(jnp.where(row == i1, ninf, cand))
    t = jnp.exp(m2 - m1)
    p1 = 1.0 / (1.0 + t)
    out_row = lax.broadcasted_iota(jnp.int32, (8, lt.shape[1]), 0)
    out = jnp.where(out_row == 0, (i1 - N_GROUPS).astype(F32), 0.0)
    out = jnp.where(out_row == 1, (i2 - N_GROUPS).astype(F32), out)
    out = jnp.where(out_row == 2, p_group * p1, out)
    return jnp.where(out_row == 3, p_group * (t * p1), out)


def _post_mix(rows_a, rows_b, g_gla, w_glu, b_glu, w_out, g_ffn, wr_pair, b_r, tm):
    na, nb = rows_a[0].shape[0], rows_b[0].shape[0]
    n_first = na // tm
    widths = [D_MODEL, GLA_WIDTH, GLA_WIDTH, S5_WIDTH]
    spec_a = [pl.BlockSpec((tm, w), lambda i: (jnp.minimum(i, n_first - 1), 0)) for w in widths]
    spec_b = [pl.BlockSpec((tm, w), lambda i: (jnp.maximum(i - n_first, 0), 0)) for w in widths]
    full = lambda a: pl.BlockSpec(a.shape, lambda i: (0,) * a.ndim)
    row = lambda w: pl.BlockSpec((tm, w), lambda i: (i, 0))
    weights = [g_gla, w_glu, b_glu, w_out, g_ffn, wr_pair, b_r]
    n = na + nb
    return pl.pallas_call(
        functools.partial(_post_mix_kernel, n_first=n_first),
        grid=(n // tm,),
        in_specs=spec_a + spec_b + [full(a) for a in weights],
        out_specs=[row(D_MODEL), row(D_MODEL // 2), pl.BlockSpec((8, tm), lambda i: (0, i))],
        out_shape=[jax.ShapeDtypeStruct((n, D_MODEL), F32), jax.ShapeDtypeStruct((n, D_MODEL // 2), jnp.uint32),
                   jax.ShapeDtypeStruct((8, n), F32)],
        compiler_params=_params(("parallel",)),
        name="post_mix",
    )(*rows_a, *rows_b, *weights)


def _gather_start(idx_ref, idx_base, idx_stride, src_hbm, dst, sem, n):
    for r in range(n):
        row = idx_ref[idx_base + r * idx_stride]
        pltpu.make_async_copy(src_hbm.at[pl.ds(row, 1)], dst.at[pl.ds(r, 1)], sem).start()


def _pack_bf16_pairs(x):
    half = x.shape[1] // 2
    bits = lax.bitcast_convert_type(x.astype(BF16).astype(F32), jnp.uint32)
    return bits[:, half:] | (bits[:, :half] >> 16)


def _unpack_bf16_pairs(p):
    lo = lax.bitcast_convert_type(p << 16, F32).astype(BF16)
    hi = lax.bitcast_convert_type(p & jnp.uint32(0xFFFF0000), F32).astype(BF16)
    return jnp.concatenate([lo, hi], axis=1)


def _gather_wait(src_hbm, dst, sem, n):
    pltpu.make_async_copy(src_hbm.at[pl.ds(0, n)], dst.at[pl.ds(0, n)], sem).wait()


def _dispatch_rows_kernel(dest_ref, fill_ref, x_ref, xs_hbm, buf, sem, zsem, *, tm):
    i = pl.program_id(0)
    slot = i % 2

    def row_copy(r, k, s):
        row = dest_ref[(i * tm + r) * TOP_K + k]
        return pltpu.make_async_copy(buf.at[s, pl.ds(r, 1)], xs_hbm.at[pl.ds(row, 1)], sem.at[s])

    def wait_slot(s):
        for _ in range(TOP_K):
            pltpu.make_async_copy(buf.at[s], xs_hbm.at[pl.ds(0, tm)], sem.at[s]).wait()

    @pl.when(i == 0)
    def _zero_fill():
        buf[1] = jnp.zeros(buf.shape[1:], buf.dtype)

        def fill(n, carry):
            @pl.when(fill_ref[n] >= 0)
            def _():
                pltpu.make_async_copy(buf.at[1, pl.ds(0, MOE_TM)], xs_hbm.at[pl.ds(fill_ref[n] * MOE_TM, MOE_TM)],
                                      zsem).start()
            return carry
        lax.fori_loop(0, 2 * N_EXPERTS, fill, 0)

        def drain(n, carry):
            @pl.when(fill_ref[n] >= 0)
            def _():
                pltpu.make_async_copy(buf.at[1, pl.ds(0, MOE_TM)], xs_hbm.at[pl.ds(0, MOE_TM)], zsem).wait()
            return carry
        lax.fori_loop(0, 2 * N_EXPERTS, drain, 0)

    @pl.when(i >= 2)
    def _reuse():
        wait_slot(slot)

    buf[slot] = x_ref[...]

    for r in range(tm):
        for k in range(TOP_K):
            row_copy(r, k, slot).start()

    @pl.when(i == pl.num_programs(0) - 1)
    def _finish():
        wait_slot(slot)

        @pl.when(i >= 1)
        def _():
            wait_slot(1 - slot)


def _dispatch_rows(dest, fill_blocks, hn, n_rows, tm):
    n = hn.shape[0]
    assert tm >= MOE_TM
    return pl.pallas_call(
        functools.partial(_dispatch_rows_kernel, tm=tm),
        grid_spec=pltpu.PrefetchScalarGridSpec(
            num_scalar_prefetch=2,
            grid=(n // tm,),
            in_specs=[pl.BlockSpec((tm, D_MODEL // 2), lambda i, d, lb: (i, 0))],
            out_specs=pl.BlockSpec(memory_space=pl.ANY),
            scratch_shapes=[pltpu.VMEM((2, tm, D_MODEL // 2), jnp.uint32), pltpu.SemaphoreType.DMA((2,)),
                            pltpu.SemaphoreType.DMA],
        ),
        out_shape=jax.ShapeDtypeStruct((n_rows, D_MODEL // 2), jnp.uint32),
        compiler_params=_params(("arbitrary",)),
        name="dispatch_rows",
    )(dest, fill_blocks, hn)


def _moe_kernel(bexp_ref, eord_ref, next_ref, nused_ref, x_ref, wg_hbm, wu_hbm, wd_hbm, o_ref,
                wg_st, wu_st, wd_st, wsem, wg_bf, wu_bf, wd_bf):
    b = pl.program_id(0)
    nu = nused_ref[0]
    e = bexp_ref[b]
    new_expert = jnp.logical_or(b == 0, bexp_ref[jnp.maximum(b - 1, 0)] != e)

    def weight_copies(expert, slot):
        return [pltpu.make_async_copy(src.at[expert], dst.at[slot], wsem.at[slot])
                for src, dst in ((wg_hbm, wg_st), (wu_hbm, wu_st), (wd_hbm, wd_st))]

    @pl.when(b == 0)
    def _prologue():
        for c in weight_copies(e, 0):
            c.start()

    @pl.when(jnp.logical_and(b < nu, new_expert))
    def _new_expert():
        slot = eord_ref[b] % 2
        for c in weight_copies(e, slot):
            c.wait()
        nxt = next_ref[b]

        @pl.when(nxt >= 0)
        def _prefetch():
            for c in weight_copies(nxt, 1 - slot):
                c.start()

        wg_bf[...] = wg_st[slot].astype(BF16)
        wu_bf[...] = wu_st[slot].astype(BF16)
        wd_bf[...] = wd_st[slot].astype(BF16)

    @pl.when(b < nu)
    def _run():
        x = _unpack_bf16_pairs(x_ref[...])
        gate = _dot(x, wg_bf[...])
        up = _dot(x, wu_bf[...])
        hid = ((gate * jax.nn.sigmoid(gate)) * up).astype(BF16)
        o_ref[...] = _dot(hid, wd_bf[...])

    @pl.when(b >= nu)
    def _skip():
        o_ref[...] = jnp.zeros(o_ref.shape, o_ref.dtype)


def _moe(block_exp, block_ord, block_next, n_used, xs, w_gate, w_up, w_down):
    nblk = xs.shape[0] // MOE_TM
    any_spec = pl.BlockSpec(memory_space=pl.ANY)
    return pl.pallas_call(
        _moe_kernel,
        grid_spec=pltpu.PrefetchScalarGridSpec(
            num_scalar_prefetch=4,
            grid=(nblk,),
            in_specs=[pl.BlockSpec((MOE_TM, D_MODEL // 2), lambda b, be, eo, nx, nu: (jnp.minimum(b, nu[0] - 1), 0)),
                      any_spec, any_spec, any_spec],
            out_specs=pl.BlockSpec((MOE_TM, D_MODEL), lambda b, *_: (b, 0)),
            scratch_shapes=[pltpu.VMEM((2, D_MODEL, EXPERT_HIDDEN), F32), pltpu.VMEM((2, D_MODEL, EXPERT_HIDDEN), F32),
                            pltpu.VMEM((2, EXPERT_HIDDEN, D_MODEL), F32), pltpu.SemaphoreType.DMA((2,)),
                            pltpu.VMEM((D_MODEL, EXPERT_HIDDEN), BF16), pltpu.VMEM((D_MODEL, EXPERT_HIDDEN), BF16),
                            pltpu.VMEM((EXPERT_HIDDEN, D_MODEL), BF16)],
        ),
        out_shape=jax.ShapeDtypeStruct((nblk * MOE_TM, D_MODEL), F32),
        compiler_params=_params(("arbitrary",)),
        name="moe",
    )(block_exp, block_ord, block_next, n_used, xs, w_gate, w_up, w_down)


def _final_kernel(dest_ref, h1_ref, wa_ref, wb_ref, g_ref, rows_hbm, y_ref, ybuf, sem, *, tm, tok_off):
    i = pl.program_id(0)

    def start(tile):
        slot = tile % 2
        base = (tok_off + tile * tm) * TOP_K
        for k in range(TOP_K):
            _gather_start(dest_ref, base + k, TOP_K, rows_hbm, ybuf.at[slot, k], sem.at[slot], tm)

    @pl.when(i == 0)
    def _first():
        start(i)

    @pl.when(i + 1 < pl.num_programs(0))
    def _next():
        start(i + 1)

    slot = i % 2
    for k in range(TOP_K):
        _gather_wait(rows_hbm, ybuf.at[slot, k], sem.at[slot], tm)
    moe = ybuf[slot, 0] * wa_ref[...] + ybuf[slot, 1] * wb_ref[...]
    y_ref[...] = _rms(h1_ref[...] + moe, g_ref[...])


def _final(dest, h1, wa, wb, g_final, rows, *, tm, n, row_off):
    off = row_off // tm
    row = lambda w: pl.BlockSpec((tm, w), lambda i, d: (i + off, 0))
    return pl.pallas_call(
        functools.partial(_final_kernel, tm=tm, tok_off=row_off),
        grid_spec=pltpu.PrefetchScalarGridSpec(
            num_scalar_prefetch=1,
            grid=(n // tm,),
            in_specs=[row(D_MODEL), row(1), row(1), pl.BlockSpec((1, D_MODEL), lambda i, d: (0, 0)),
                      pl.BlockSpec(memory_space=pl.ANY)],
            out_specs=pl.BlockSpec((tm, D_MODEL), lambda i, d: (i, 0)),
            scratch_shapes=[pltpu.VMEM((2, TOP_K, tm, D_MODEL), F32), pltpu.SemaphoreType.DMA((2,))],
        ),
        out_shape=jax.ShapeDtypeStruct((n, D_MODEL), F32),
        compiler_params=_params(("arbitrary",)),
        name="final",
    )(dest, h1, wa, wb, g_final, rows)


def _dispatch(eid):
    t = eid.shape[0]
    a = t * TOP_K
    assert a % MOE_TM == 0
    nblk = a // MOE_TM + N_EXPERTS
    flat = eid.reshape(-1)
    experts = jnp.arange(N_EXPERTS, dtype=jnp.int32)
    onehot = (flat[:, None] == experts[None, :]).astype(F32).reshape(a // MOE_TM, MOE_TM, N_EXPERTS)
    strict_lower = jnp.tril(jnp.ones((MOE_TM, MOE_TM), F32), -1)
    within = jnp.einsum('ij,bjk->bik', strict_lower, onehot)
    totals = jnp.sum(onehot, axis=1)
    before = jnp.cumsum(totals, axis=0) - totals
    counts = jnp.sum(totals, axis=0).astype(jnp.int32)
    padded = (counts + MOE_TM - 1) // MOE_TM * MOE_TM
    pends = jnp.cumsum(padded)
    offset = before + (pends - padded).astype(F32)[None, :]
    dest = jnp.sum((within + offset[:, None, :]) * onehot, axis=-1).reshape(-1)
    n_used = (pends[-1] // MOE_TM).astype(jnp.int32)
    blk = jnp.minimum(jnp.arange(nblk, dtype=jnp.int32), n_used - 1)
    block_exp = jnp.minimum(jnp.searchsorted(pends, blk * MOE_TM, side='right'), N_EXPERTS - 1).astype(jnp.int32)
    in_use = counts > 0
    ordinal = jnp.cumsum(in_use.astype(jnp.int32)) - 1
    later = lax.cummin(jnp.where(in_use, experts, N_EXPERTS), axis=0, reverse=True)
    nxt = jnp.concatenate([later[1:], jnp.full((1,), N_EXPERTS, jnp.int32)])
    nxt = jnp.where(nxt < N_EXPERTS, nxt, -1)
    last_block = jnp.where(in_use, pends // MOE_TM - 1, -1)
    unused = n_used + experts
    fill_blocks = jnp.concatenate([last_block, jnp.where(unused < nblk, unused, -1)]).astype(jnp.int32)
    return (dest.astype(jnp.int32), nblk * MOE_TM, block_exp, ordinal[block_exp], nxt[block_exp], fill_blocks,
            n_used.reshape(1))


def kernel(x_prompt, x_sample, state_gla, state_s5_re, state_s5_im, meta, g_mix, w_in, w_gk2, b_gk, g_gla,
           lam_re, lam_im, log_dt, s5_b_re, s5_b_im, s5_c_re, s5_c_im, d_skip, w_glu, b_glu, w_out, g_ffn,
           w_rg, b_rg, w_re, b_re, w_gate, w_up, w_down, g_final):
    bp, tp, _ = x_prompt.shape
    bs, ts, _ = x_sample.shape
    l = 0

    w_cat = w_in[l].astype(BF16)
    wgk_hi, wgk_lo = _split2(jnp.pad(w_gk2[l], ((0, 128 - GLA_RANK), (0, 0))))
    g_mix2, b_gk2 = g_mix[l][None], b_gk[l][None]
    mats = _s5_matrices(lam_re[l], lam_im[l], log_dt[l], s5_b_re[l], s5_b_im[l], s5_c_re[l], s5_c_im[l],
                        d_skip[l])
    w_router = jnp.concatenate([w_rg[l], jnp.moveaxis(w_re[l], 0, 1).reshape(D_MODEL, N_EXPERTS)], axis=1)
    w_router = jnp.pad(w_router, ((0, 0), (0, ROUTER_LANES - N_GROUPS - N_EXPERTS)))
    wr_pair = jnp.concatenate(_split2(w_router), axis=1)
    b_router = jnp.pad(jnp.concatenate([b_rg[l], b_re[l].reshape(-1)]),
                       (0, ROUTER_LANES - N_GROUPS - N_EXPERTS))[None]
    w_glu_b, w_out_b = w_glu[l].astype(BF16), w_out[l].astype(BF16)
    g_gla2, b_glu2, g_ffn2 = g_gla[l].reshape(1, GLA_WIDTH), b_glu[l][None], g_ffn[l][None]

    proj = functools.partial(_in_proj, g_mix=g_mix2, w_cat=w_cat, wgk_hi=wgk_hi, wgk_lo=wgk_lo, b_gk=b_gk2)

    qm, km, vm, _, gm, um = proj(meta, tm=N_META)
    zero_s = jnp.zeros((1, GLA_HEADS, GLA_DK, GLA_DV), F32)
    _, s_meta = _gla(qm, km, vm, gm, zero_s, seq_len=N_META, chunk=N_META)
    zero_h = jnp.zeros((1, S5_NGB, 1, S5_SL), F32)
    _, hm_re, hm_im = _s5_long(um, mats, zero_h, zero_h, seq_len=N_META)

    xp = x_prompt.reshape(bp * tp, D_MODEL)
    qp, kp, vp, rp, gp, up = proj(xp, tm=512)
    op, gla_p = _gla(qp, kp, vp, gp, jnp.broadcast_to(s_meta, (bp,) + s_meta.shape[1:]),
                     seq_len=tp, chunk=GLA_CHUNK)
    y5p, hp_re, hp_im = _s5_pipelined(up, mats, jnp.broadcast_to(hm_re, (bp,) + hm_re.shape[1:]),
                                      jnp.broadcast_to(hm_im, (bp,) + hm_im.shape[1:]), seq_len=tp)

    xs = x_sample.reshape(bs * ts, D_MODEL)
    qs, ks, vs, rs, gs, us = proj(xs, tm=512)
    chunk_s = GLA_CHUNK if ts % GLA_CHUNK == 0 else ts
    os_, gla_s = _gla(qs, ks, vs, gs, state_gla[l], seq_len=ts, chunk=chunk_s)
    y5s, hs_re, hs_im = _s5_short(us, mats, state_s5_re[l].reshape(bs, -1), state_s5_im[l].reshape(bs, -1),
                                  seq_len=ts)

    np_rows, ns_rows = bp * tp, bs * ts
    h1, hn, route = _post_mix((xp, op, rp, y5p), (xs, os_, rs, y5s), g_gla2, w_glu_b, b_glu2, w_out_b, g_ffn2,
                               wr_pair, b_router, tm=256)

    eid = route[:TOP_K].T.astype(jnp.int32)
    wts = route[TOP_K:2 * TOP_K].T
    dest, n_sorted, block_exp, block_ord, block_next, fill_blocks, n_used = _dispatch(eid)
    xs_rows = _dispatch_rows(dest, fill_blocks, hn, n_sorted, tm=256)
    out_rows = _moe(block_exp, block_ord, block_next, n_used, xs_rows, w_gate[l], w_up[l], w_down[l])
    fin = functools.partial(_final, dest, h1, wts[:, 0:1], wts[:, 1:2], g_final[None], out_rows, tm=256)
    y_prompt = fin(n=np_rows, row_off=0)
    y_sample = fin(n=ns_rows, row_off=np_rows)

    return (y_prompt.reshape(bp, tp, D_MODEL), y_sample.reshape(bs, ts, D_MODEL),
            gla_p[None], hp_re.reshape(1, bp, S5_GROUPS, S5_STATE), hp_im.reshape(1, bp, S5_GROUPS, S5_STATE),
            gla_s[None], hs_re.reshape(1, bs, S5_GROUPS, S5_STATE), hs_im.reshape(1, bs, S5_GROUPS, S5_STATE))
```

```python
import functools
import math

import jax
import jax.numpy as jnp
from jax import lax
from jax.experimental import pallas as pl
from jax.experimental.pallas import tpu as pltpu

F32 = jnp.float32
BF16 = jnp.bfloat16

D_MODEL = 2048
N_META = 16
GLA_HEADS = 4
GLA_DK = 128
GLA_DV = 256
GLA_KEY_WIDTH = GLA_HEADS * GLA_DK
GLA_WIDTH = GLA_HEADS * GLA_DV
GLA_RANK = 16
GLA_GATE_NORM = 16.0
GLA_CHUNK = 64
GLA_SUB = 16
GLA_SEQS_PER_STEP = 8
MASKED_EXPONENT = -1e30
S5_WIDTH = 1024
S5_GROUP = 16
S5_GROUPS = 64
S5_STATE = 64
S5_GB = 8
S5_NGB = S5_GROUPS // S5_GB
S5_UL = S5_GB * S5_GROUP
S5_SL = S5_GB * S5_STATE
S5_LT = 2 * S5_SL // 128
S5_UNROLL = 4
S5_PIPE_CHUNKS = 1
N_GROUPS = 4
EXPERTS_PER_GROUP = 8
N_EXPERTS = N_GROUPS * EXPERTS_PER_GROUP
EXPERT_HIDDEN = 512
TOP_K = 2
EPS = 1e-6
ROUTER_LANES = 128
ROUTE_ROWS = -(-(N_GROUPS + N_EXPERTS) // 8) * 8
MOE_TM = 256
MOE_WEIGHT_SLOTS = 3
VMEM_LIMIT = 56 * 1024 * 1024

_dot = functools.partial(jnp.dot, preferred_element_type=F32)


def _split2(x):
    hi = x.astype(BF16)
    lo = (x - hi.astype(F32)).astype(BF16)
    return hi, lo


def _rms(x, g):
    return x * lax.rsqrt(jnp.mean(x * x, axis=-1, keepdims=True) + EPS) * g


def _params(sem):
    return pltpu.CompilerParams(dimension_semantics=sem, vmem_limit_bytes=VMEM_LIMIT)


def _in_proj_kernel(x_ref, g_ref, w_ref, wgk_hi_ref, wgk_lo_ref, bgk_ref,
                    q_ref, k_ref, v_ref, r_ref, gk_ref, u_ref):
    xb = _rms(x_ref[...], g_ref[...]).astype(BF16)
    kw = GLA_KEY_WIDTH
    q_ref[...] = _dot(xb, w_ref[:, 0:kw]) * (GLA_DK ** -0.5)
    k_ref[...] = _dot(xb, w_ref[:, kw:2 * kw])
    v_ref[...] = _dot(xb, w_ref[:, 2 * kw:2 * kw + GLA_WIDTH])
    r_ref[...] = _dot(xb, w_ref[:, 2 * kw + GLA_WIDTH:2 * kw + 2 * GLA_WIDTH])
    c0 = 2 * kw + 2 * GLA_WIDTH
    tail = _dot(xb, w_ref[:, c0:])
    u_ref[...] = tail[:, GLA_RANK:GLA_RANK + S5_WIDTH]
    a_low = tail[:, :128]
    a_hi, a_lo = _split2(a_low)
    z = (_dot(a_hi, wgk_hi_ref[...]) + _dot(a_hi, wgk_lo_ref[...]) + _dot(a_lo, wgk_hi_ref[...])
         + bgk_ref[...])
    gk_ref[...] = (jnp.minimum(z, 0.0) - jnp.log1p(jnp.exp(-jnp.abs(z)))) * (1.0 / GLA_GATE_NORM)


def _in_proj(x, g_mix, w_cat, wgk_hi, wgk_lo, b_gk, tm):
    n = x.shape[0]
    wcols = w_cat.shape[1]
    row = lambda w: pl.BlockSpec((tm, w), lambda i: (i, 0))
    full = lambda a: pl.BlockSpec(a.shape, lambda i: (0,) * a.ndim)
    widths = [GLA_KEY_WIDTH, GLA_KEY_WIDTH, GLA_WIDTH, GLA_WIDTH, GLA_KEY_WIDTH, S5_WIDTH]
    return pl.pallas_call(
        _in_proj_kernel,
        grid=(n // tm,),
        in_specs=[row(D_MODEL), full(g_mix),
                  pl.BlockSpec((D_MODEL, wcols), lambda i: (0, 0), pipeline_mode=pl.Buffered(1)),
                  full(wgk_hi), full(wgk_lo), full(b_gk)],
        out_specs=[row(w) for w in widths],
        out_shape=[jax.ShapeDtypeStruct((n, w), F32) for w in widths],
        compiler_params=_params(("parallel",)),
        name="in_proj",
    )(x, g_mix, w_cat, wgk_hi, wgk_lo, b_gk)


def _gla_chunk(q_ref, k_ref, v_ref, g_ref, o_ref, r0, C, consts, read_state, write_state):
    tril, piece_rows, piece_lanes = consts
    sub = min(C, GLA_SUB)
    nsub = C // sub
    g = g_ref[pl.ds(r0, C), :]
    g1 = g.astype(BF16)
    rem = g - g1.astype(F32)
    g2 = rem.astype(BF16)
    g3 = (rem - g2.astype(F32)).astype(BF16)
    b_all = _dot(tril, g1) + _dot(tril, g2) + _dot(tril, g3)
    for h in range(GLA_HEADS):
        ks = slice(h * GLA_DK, (h + 1) * GLA_DK)
        vs = slice(h * GLA_DV, (h + 1) * GLA_DV)
        q = q_ref[pl.ds(r0, C), ks]
        k = k_ref[pl.ds(r0, C), ks]
        b = b_all[:, ks]
        s_prev = read_state(h)
        vb = v_ref[pl.ds(r0, C), vs].astype(BF16)
        o_inter = _dot((q * jnp.exp(b)).astype(BF16), s_prev.astype(BF16))
        blocks = []
        for s in range(nsub):
            lo = s * sub
            bs, qs, ksub = b[lo:lo + sub], q[lo:lo + sub], k[lo:lo + sub]
            acc = o_inter[lo:lo + sub]
            if s > 0:
                anchor = b[lo - 1:lo]
                qd = (qs * jnp.exp(bs - anchor)).astype(BF16)
                kd = (k[:lo] * jnp.exp(anchor - b[:lo])).astype(BF16)
                sc = lax.dot_general(qd, kd, (((1,), (1,)), ((), ())), preferred_element_type=F32)
                acc = acc + _dot(sc.astype(BF16), vb[:lo])
            pieces = [jnp.zeros((8, 128), F32) for _ in range(sub // 8)]
            for jj in range(sub):
                for p in range(jj // 8, sub // 8):
                    r8 = slice(8 * p, 8 * p + 8)
                    diff = bs[r8] - bs[jj:jj + 1]
                    if 8 * p < jj:
                        diff = jnp.where(piece_rows + 8 * p >= jj, diff, MASKED_EXPONENT)
                    col = jnp.sum((qs[r8] * ksub[jj:jj + 1]) * jnp.exp(diff), axis=-1, keepdims=True)
                    pieces[p] = jnp.where(piece_lanes == jj, col, pieces[p])
            scores = jnp.concatenate(pieces, axis=0) if len(pieces) > 1 else pieces[0]
            acc = acc + _dot(scores[:, :sub].astype(BF16), vb[lo:lo + sub])
            blocks.append(acc)
        o_ref[pl.ds(r0, C), vs] = jnp.concatenate(blocks, axis=0) if nsub > 1 else blocks[0]
        b_last = b[C - 1:C]
        kdec = (k * jnp.exp(b_last - b)).astype(BF16)
        upd = lax.dot_general(kdec, vb, (((0,), (0,)), ((), ())), preferred_element_type=F32)
        dcol = jnp.broadcast_to(jnp.exp(b_last), (GLA_DK, GLA_DK)).T
        write_state(h, s_prev * jnp.concatenate([dcol, dcol], axis=1) + upd)


def _gla_consts(C):
    sub = min(C, GLA_SUB)
    ri = lax.broadcasted_iota(jnp.int32, (C, C), 0)
    ci = lax.broadcasted_iota(jnp.int32, (C, C), 1)
    tril = jnp.where(ri >= ci, 1.0, 0.0).astype(BF16)
    piece_rows = lax.broadcasted_iota(jnp.int32, (8, GLA_DK), 0)
    piece_lanes = lax.broadcasted_iota(jnp.int32, (8, 128), 1)
    return tril, piece_rows, piece_lanes


def _gla_long_kernel(q_ref, k_ref, v_ref, g_ref, s0_ref, o_ref, sout_ref, s_scr, *, chunk, n_inner):
    j = pl.program_id(1)
    consts = _gla_consts(chunk)

    @pl.when(j == 0)
    def _init():
        s_scr[...] = s0_ref[0]

    def write_state(h, s):
        s_scr[h] = s

    def chunk_body(c, carry):
        _gla_chunk(q_ref, k_ref, v_ref, g_ref, o_ref, pl.multiple_of(c * chunk, chunk), chunk, consts,
                   lambda h: s_scr[h], write_state)
        return carry

    lax.fori_loop(0, n_inner, chunk_body, 0)

    @pl.when(j == pl.num_programs(1) - 1)
    def _fin():
        sout_ref[0] = s_scr[...]


def _gla_short_kernel(q_ref, k_ref, v_ref, g_ref, s0_ref, o_ref, sout_ref, *, chunk, nb):
    consts = _gla_consts(chunk)

    def seq_body(n, carry):
        def write_state(h, s):
            sout_ref[n, h] = s

        _gla_chunk(q_ref, k_ref, v_ref, g_ref, o_ref, pl.multiple_of(n * chunk, chunk), chunk, consts,
                   lambda h: s0_ref[n, h], write_state)
        return carry

    lax.fori_loop(0, nb, seq_body, 0, unroll=2 if nb % 2 == 0 else 1)


def _gla(q, k, v, g, s0, *, seq_len, chunk):
    nseq = s0.shape[0]
    n = q.shape[0]
    out_shape = [jax.ShapeDtypeStruct((n, GLA_WIDTH), F32),
                 jax.ShapeDtypeStruct((nseq, GLA_HEADS, GLA_DK, GLA_DV), F32)]
    if seq_len == chunk:
        nb = min(nseq, GLA_SEQS_PER_STEP)
        rows = lambda w: pl.BlockSpec((nb * chunk, w), lambda s: (s, 0))
        st = pl.BlockSpec((nb, GLA_HEADS, GLA_DK, GLA_DV), lambda s: (s, 0, 0, 0))
        return pl.pallas_call(
            functools.partial(_gla_short_kernel, chunk=chunk, nb=nb),
            grid=(nseq // nb,),
            in_specs=[rows(GLA_KEY_WIDTH), rows(GLA_KEY_WIDTH), rows(GLA_WIDTH), rows(GLA_KEY_WIDTH), st],
            out_specs=[rows(GLA_WIDTH), st],
            out_shape=out_shape,
            compiler_params=_params(("parallel",)),
            name="gla_short",
        )(q, k, v, g, s0)
    rb = min(seq_len, 4 * chunk)
    nblk = seq_len // rb
    rows = lambda w: pl.BlockSpec((rb, w), lambda s, j: (s * nblk + j, 0))
    st = pl.BlockSpec((1, GLA_HEADS, GLA_DK, GLA_DV), lambda s, j: (s, 0, 0, 0))
    return pl.pallas_call(
        functools.partial(_gla_long_kernel, chunk=chunk, n_inner=rb // chunk),
        grid=(nseq, nblk),
        in_specs=[rows(GLA_KEY_WIDTH), rows(GLA_KEY_WIDTH), rows(GLA_WIDTH), rows(GLA_KEY_WIDTH), st],
        out_specs=[rows(GLA_WIDTH), st],
        out_shape=out_shape,
        scratch_shapes=[pltpu.VMEM((GLA_HEADS, GLA_DK, GLA_DV), F32)],
        compiler_params=_params(("parallel", "arbitrary")),
        name="gla_long",
    )(q, k, v, g, s0)


def _cmul(ar, ai, br, bi):
    return ar * br - ai * bi, ar * bi + ai * br


def _cpow(ar, ai, n):
    res = None
    while n:
        if n & 1:
            res = (ar, ai) if res is None else _cmul(res[0], res[1], ar, ai)
        n >>= 1
        if n:
            ar, ai = _cmul(ar, ai, ar, ai)
    return res


def _s5_segment_len(seq_len):
    s = -(-seq_len // 8)
    s = -(-s // 4) * 4
    return s if (s // 4) % 2 == 1 else s + 4


def _st_store(st_scr, rows, val):
    for t in range(S5_LT):
        st_scr[t, rows, :] = val[:, t * 128:(t + 1) * 128]


def _st_load(st_scr, rows):
    return jnp.concatenate([st_scr[t, rows, :] for t in range(S5_LT)], axis=1)


def _s5_long_kernel(u_ref, bb_ref, cc_ref, dsk_ref, are_ref, aim_ref, h0re_ref, h0im_ref,
                    y_ref, hre_ref, him_ref, st_scr, *, seq_len, seg):
    T, S, SL = seq_len, seg, S5_SL
    rc = min(T, 512)
    for c in range(T // rc):
        sl = slice(c * rc, (c + 1) * rc)
        _st_store(st_scr, sl, _dot(u_ref[sl, :].astype(BF16), bb_ref[0]))
    if 8 * S > T:
        _st_store(st_scr, slice(T, 8 * S), jnp.zeros((8 * S - T, 2 * SL), F32))
    a_re, a_im = are_ref[0], aim_ref[0]
    ar = jnp.broadcast_to(a_re, (8, SL))
    ai = jnp.broadcast_to(a_im, (8, SL))

    def step(i, h, store):
        strand = pl.ds(i, 8, stride=S)
        x = _st_load(st_scr, strand)
        mr, mi = _cmul(ar, ai, h[0], h[1])
        nr, ni = mr + x[:, :SL], mi + x[:, SL:]
        if store:
            _st_store(st_scr, strand, jnp.concatenate([nr, ni], axis=1))
        return nr, ni

    zero = jnp.zeros((8, SL), F32)
    fr, fi = lax.fori_loop(0, S, functools.partial(step, store=False), (zero, zero), unroll=S5_UNROLL)

    car_r, car_i = _s5_carries(fr, fi, a_re, a_im, S, h0re_ref[0, 0], h0im_ref[0, 0])

    lax.fori_loop(0, S, functools.partial(step, store=True), (car_r, car_i), unroll=S5_UNROLL)

    last = _st_load(st_scr, slice(T - 1, T))
    hre_ref[0, 0] = last[:, :SL]
    him_ref[0, 0] = last[:, SL:]
    for c in range(T // rc):
        sl = slice(c * rc, (c + 1) * rc)
        y_ref[sl, :] = _dot(_st_load(st_scr, sl).astype(BF16), cc_ref[0]) + dsk_ref[0] * u_ref[sl, :]


def _s5_long(u, mats, h0_re, h0_im, *, seq_len):
    bb, cc, dsk, a_re, a_im = mats
    nseq = h0_re.shape[0]
    seg = _s5_segment_len(seq_len)
    gb3 = lambda shape: pl.BlockSpec((1,) + shape, lambda s, g: (g, 0, 0))
    st = pl.BlockSpec((1, 1, 1, S5_SL), lambda s, g: (s, g, 0, 0))
    urow = pl.BlockSpec((seq_len, S5_UL), lambda s, g: (s, g))
    st_shape = jax.ShapeDtypeStruct((nseq, S5_NGB, 1, S5_SL), F32)
    return pl.pallas_call(
        functools.partial(_s5_long_kernel, seq_len=seq_len, seg=seg),
        grid=(nseq, S5_NGB),
        in_specs=[urow, gb3((S5_UL, 2 * S5_SL)), gb3((2 * S5_SL, S5_UL)), gb3((1, S5_UL)),
                  gb3((1, S5_SL)), gb3((1, S5_SL)), st, st],
        out_specs=[urow, st, st],
        out_shape=[jax.ShapeDtypeStruct(u.shape, F32), st_shape, st_shape],
        scratch_shapes=[pltpu.VMEM((S5_LT, 8 * seg, 128), F32)],
        compiler_params=_params(("parallel", "parallel")),
        name="s5_long",
    )(u, bb, cc, dsk, a_re, a_im, h0_re, h0_im)


def _s5_pipe_step(b_buf, s_buf, c_buf, u16, c16, ub_ref, uc_ref, bb_ref, cc_ref, dsk_ref, are_ref, aim_ref,
                  h0re_ref, h0im_ref, y_ref, hre_ref, him_ref, *, seq_len, seg):
    T, S, SL = seq_len, seg, S5_SL
    rc, sc = T // S5_PIPE_CHUNKS, S // S5_PIPE_CHUNKS
    if 8 * S > T:
        _st_store(b_buf, slice(T, 8 * S), jnp.zeros((8 * S - T, 2 * SL), F32))
    a_re, a_im = are_ref[0], aim_ref[0]
    ar = jnp.broadcast_to(a_re, (8, SL))
    ai = jnp.broadcast_to(a_im, (8, SL))

    def scan_chunk(k, h, store):
        for d in range(sc):
            strand = pl.ds(k * sc + d, 8, stride=S)
            x = _st_load(s_buf, strand)
            mr, mi = _cmul(ar, ai, h[0], h[1])
            h = (mr + x[:, :SL], mi + x[:, SL:])
            if store:
                _st_store(s_buf, strand, jnp.concatenate(h, axis=1))
        return h

    u16[...] = ub_ref[...].astype(BF16)
    for c in range(S5_PIPE_CHUNKS):
        rows = slice(c * rc, (c + 1) * rc)
        c16[rows, :] = _st_load(c_buf, rows).astype(BF16)

    def end_states(k, h):
        rows = pl.ds(pl.multiple_of(k * rc, rc), rc)
        _st_store(b_buf, rows, _dot(u16[rows, :], bb_ref[0]))
        return scan_chunk(k, h, False)

    zero = jnp.zeros((8, SL), F32)
    fr, fi = lax.fori_loop(0, S5_PIPE_CHUNKS, end_states, (zero, zero))
    car_r, car_i = _s5_carries(fr, fi, a_re, a_im, S, h0re_ref[0, 0], h0im_ref[0, 0])

    def states(k, h):
        rows = pl.ds(pl.multiple_of(k * rc, rc), rc)
        y_ref[rows, :] = _dot(c16[rows, :], cc_ref[0]) + dsk_ref[0] * uc_ref[rows, :]
        return scan_chunk(k, h, True)

    lax.fori_loop(0, S5_PIPE_CHUNKS, states, (car_r, car_i))
    last = _st_load(s_buf, slice(T - 1, T))
    hre_ref[0, 0] = last[:, :SL]
    him_ref[0, 0] = last[:, SL:]


def _s5_carries(fr, fi, a_re, a_im, seg, h0_re, h0_im):
    as_re, as_im = _cpow(a_re, a_im, seg)
    rows = lax.broadcasted_iota(jnp.int32, fr.shape, 0)
    cr, ci = h0_re, h0_im
    car_r, car_i = jnp.zeros_like(fr), jnp.zeros_like(fi)
    for r in range(8):
        car_r = jnp.where(rows == r, cr, car_r)
        car_i = jnp.where(rows == r, ci, car_i)
        if r < 7:
            mr, mi = _cmul(as_re, as_im, cr, ci)
            cr, ci = mr + fr[r:r + 1], mi + fi[r:r + 1]
    return car_r, car_i


def _s5_pipe_kernel(*refs, seq_len, seg):
    bufs, u16, c16 = refs[-5:-2], refs[-2], refs[-1]
    refs = refs[:-5]
    t = pl.program_id(0)

    @pl.when(t == 0)
    def _init():
        for buf in bufs:
            buf[...] = jnp.zeros(buf.shape, buf.dtype)

    for r in range(3):
        @pl.when(t % 3 == r)
        def _rotation(r=r):
            _s5_pipe_step(bufs[r], bufs[(r + 2) % 3], bufs[(r + 1) % 3], u16, c16, *refs, seq_len=seq_len, seg=seg)


def _s5_pipelined(u, mats, h0_re, h0_im, *, seq_len):
    bb, cc, dsk, a_re, a_im = mats
    nseq = h0_re.shape[0]
    n_items = nseq * S5_NGB
    seg = _s5_segment_len(seq_len)
    assert seq_len % (8 * S5_PIPE_CHUNKS) == 0 and seg % S5_PIPE_CHUNKS == 0

    def item(lag):
        return lambda t: jnp.clip(t - lag, 0, n_items - 1)

    def per_gb(shape, lag):
        return pl.BlockSpec((1,) + shape, lambda t: (item(lag)(t) % S5_NGB, 0, 0))

    def rows(lag):
        return pl.BlockSpec((seq_len, S5_UL), lambda t: (item(lag)(t) // S5_NGB, item(lag)(t) % S5_NGB))

    st = pl.BlockSpec((1, 1, 1, S5_SL), lambda t: (item(1)(t) // S5_NGB, item(1)(t) % S5_NGB, 0, 0))
    st_shape = jax.ShapeDtypeStruct((nseq, S5_NGB, 1, S5_SL), F32)
    buf = pltpu.VMEM((S5_LT, 8 * seg, 128), F32)
    return pl.pallas_call(
        functools.partial(_s5_pipe_kernel, seq_len=seq_len, seg=seg),
        grid=(n_items + 2,),
        in_specs=[rows(0), rows(2), per_gb((S5_UL, 2 * S5_SL), 0), per_gb((2 * S5_SL, S5_UL), 2),
                  per_gb((1, S5_UL), 2), per_gb((1, S5_SL), 1), per_gb((1, S5_SL), 1), st, st],
        out_specs=[rows(2), st, st],
        out_shape=[jax.ShapeDtypeStruct(u.shape, F32), st_shape, st_shape],
        scratch_shapes=[buf, buf, buf, pltpu.VMEM((seq_len, S5_UL), BF16), pltpu.VMEM((seq_len, 2 * S5_SL), BF16)],
        compiler_params=_params(("arbitrary",)),
        name="s5_pipelined",
    )(u, u, bb, cc, dsk, a_re, a_im, h0_re, h0_im)


def _s5_short_kernel(u_ref, bb_ref, cc_ref, dsk_ref, are_ref, aim_ref, h0re_ref, h0im_ref,
                     y_ref, hre_ref, him_ref, st_scr, *, seq_len, nseq):
    SL = S5_SL
    _st_store(st_scr, slice(None), _dot(u_ref[...].astype(BF16), bb_ref[0]))
    ar = jnp.broadcast_to(are_ref[0], (nseq, SL))
    ai = jnp.broadcast_to(aim_ref[0], (nseq, SL))
    hr, hi = h0re_ref[...], h0im_ref[...]
    for t in range(seq_len):
        step = pl.ds(t, nseq, stride=seq_len)
        x = _st_load(st_scr, step)
        mr, mi = _cmul(ar, ai, hr, hi)
        hr, hi = mr + x[:, :SL], mi + x[:, SL:]
        _st_store(st_scr, step, jnp.concatenate([hr, hi], axis=1))
    hre_ref[...] = hr
    him_ref[...] = hi
    y_ref[...] = _dot(_st_load(st_scr, slice(None)).astype(BF16), cc_ref[0]) + dsk_ref[0] * u_ref[...]


def _s5_short(u, mats, h0_re, h0_im, *, seq_len):
    bb, cc, dsk, a_re, a_im = mats
    nseq = h0_re.shape[0]
    n = nseq * seq_len
    gb3 = lambda shape: pl.BlockSpec((1,) + shape, lambda g: (g, 0, 0))
    st = pl.BlockSpec((nseq, S5_SL), lambda g: (0, g))
    urow = pl.BlockSpec((n, S5_UL), lambda g: (0, g))
    st_shape = jax.ShapeDtypeStruct(h0_re.shape, F32)
    return pl.pallas_call(
        functools.partial(_s5_short_kernel, seq_len=seq_len, nseq=nseq),
        grid=(S5_NGB,),
        in_specs=[urow, gb3((S5_UL, 2 * S5_SL)), gb3((2 * S5_SL, S5_UL)), gb3((1, S5_UL)),
                  gb3((1, S5_SL)), gb3((1, S5_SL)), st, st],
        out_specs=[urow, st, st],
        out_shape=[jax.ShapeDtypeStruct(u.shape, F32), st_shape, st_shape],
        scratch_shapes=[pltpu.VMEM((S5_LT, n, 128), F32)],
        compiler_params=_params(("parallel",)),
        name="s5_short",
    )(u, bb, cc, dsk, a_re, a_im, h0_re, h0_im)


def _s5_matrices(lam_re, lam_im, log_dt, b_re, b_im, c_re, c_im, d_skip):
    dt = jnp.exp(log_dt)[:, None]
    mag = jnp.exp(lam_re * dt)
    ab_re, ab_im = mag * jnp.cos(lam_im * dt), mag * jnp.sin(lam_im * dt)
    den = lam_re * lam_re + lam_im * lam_im
    f_re = ((ab_re - 1.0) * lam_re + ab_im * lam_im) / den
    f_im = (ab_im * lam_re - (ab_re - 1.0) * lam_im) / den
    bb_re = f_re[..., None] * b_re - f_im[..., None] * b_im
    bb_im = f_re[..., None] * b_im + f_im[..., None] * b_re
    eye = jnp.eye(S5_GB, dtype=F32)

    def in_mat(m):
        m = m.reshape(S5_NGB, S5_GB, S5_STATE, S5_GROUP)
        return jnp.einsum('bgpc,gh->bgchp', m, eye).reshape(S5_NGB, S5_UL, S5_SL)

    def out_mat(m):
        m = m.reshape(S5_NGB, S5_GB, S5_GROUP, S5_STATE)
        return jnp.einsum('bgcp,gh->bgphc', m, eye).reshape(S5_NGB, S5_SL, S5_UL)

    bb = jnp.concatenate([in_mat(bb_re), in_mat(bb_im)], axis=2).astype(BF16)
    cc = jnp.concatenate([out_mat(c_re), out_mat(-c_im)], axis=1).astype(BF16)
    dsk = d_skip.reshape(S5_NGB, 1, S5_UL)
    return bb, cc, dsk, ab_re.reshape(S5_NGB, 1, S5_SL), ab_im.reshape(S5_NGB, 1, S5_SL)


def _post_mix_kernel(xa_ref, oa_ref, ra_ref, ya_ref, xb_ref, ob_ref, rb_ref, yb_ref, *rest, n_first):
    i = pl.program_id(0)

    @pl.when(i < n_first)
    def _first():
        _post_mix_tile(xa_ref, oa_ref, ra_ref, ya_ref, *rest)

    @pl.when(i >= n_first)
    def _second():
        _post_mix_tile(xb_ref, ob_ref, rb_ref, yb_ref, *rest)


def _post_mix_tile(x_ref, o_ref, r_ref, y5_ref, ggla_ref, wglu_ref, bglu_ref, wout_ref, gffn_ref,
                   wr_ref, br_ref, h1_ref, hn_ref, lg_ref):
    o = o_ref[...]
    parts = []
    for h in range(GLA_HEADS):
        oh = o[:, h * GLA_DV:(h + 1) * GLA_DV]
        parts.append(oh * lax.rsqrt(jnp.mean(oh * oh, axis=-1, keepdims=True) + EPS))
    r = r_ref[...]
    o_gla = (jnp.concatenate(parts, axis=1) * ggla_ref[...]) * (r * jax.nn.sigmoid(r))
    y5 = y5_ref[...]
    z = y5 * (0.5 * (1.0 + jnp.tanh(math.sqrt(2.0 / math.pi) * (y5 + 0.044715 * (y5 * y5 * y5)))))
    o_s5 = z * jax.nn.sigmoid(_dot(z.astype(BF16), wglu_ref[...]) + bglu_ref[...])
    att = (_dot(o_gla.astype(BF16), wout_ref[0:GLA_WIDTH, :])
           + _dot(o_s5.astype(BF16), wout_ref[GLA_WIDTH:GLA_WIDTH + S5_WIDTH, :]))
    h1 = x_ref[...] + att
    h1_ref[...] = h1
    hn = _rms(h1, gffn_ref[...])
    hn_ref[...] = _pack_bf16_pairs(hn)
    hn_hi, hn_lo = _split2(hn)
    both = _dot(hn_hi, wr_ref[...])
    logits = (both[:, :ROUTER_LANES] + both[:, ROUTER_LANES:] + _dot(hn_lo, wr_ref[:, :ROUTER_LANES])
              + br_ref[...])
    lg_ref[...] = _route_tile(logits)


def _route_tile(logits):
    lt = logits.T[:ROUTE_ROWS]
    row = lax.broadcasted_iota(jnp.int32, lt.shape, 0)
    ninf = float('-inf')

    def first_max(vals):
        m = jnp.max(vals, axis=0, keepdims=True)
        return m, jnp.min(jnp.where(vals == m, row, ROUTE_ROWS), axis=0, keepdims=True)

    is_group = row < N_GROUPS
    gmax, gsel = first_max(jnp.where(is_group, lt, ninf))
    p_group = 1.0 / jnp.sum(jnp.where(is_group, jnp.exp(lt - gmax), 0.0), axis=0, keepdims=True)
    first = N_GROUPS + gsel * EXPERTS_PER_GROUP
    in_group = jnp.logical_and(row >= first, row < first + EXPERTS_PER_GROUP)
    cand = jnp.where(in_group, lt, ninf)
    m1, i1 = first_max(cand)
    m2, i2 = first_max(jnp.where(row == i1, ninf, cand))
    t = jnp.exp(m2 - m1)
    p1 = 1.0 / (1.0 + t)
    out_row = lax.broadcasted_iota(jnp.int32, (8, lt.shape[1]), 0)
    out = jnp.where(out_row == 0, (i1 - N_GROUPS).astype(F32), 0.0)
    out = jnp.where(out_row == 1, (i2 - N_GROUPS).astype(F32), out)
    out = jnp.where(out_row == 2, p_group * p1, out)
    return jnp.where(out_row == 3, p_group * (t * p1), out)


def _post_mix(rows_a, rows_b, g_gla, w_glu, b_glu, w_out, g_ffn, wr_pair, b_r, tm):
    na, nb = rows_a[0].shape[0], rows_b[0].shape[0]
    n_first = na // tm
    widths = [D_MODEL, GLA_WIDTH, GLA_WIDTH, S5_WIDTH]
    spec_a = [pl.BlockSpec((tm, w), lambda i: (jnp.minimum(i, n_first - 1), 0)) for w in widths]
    spec_b = [pl.BlockSpec((tm, w), lambda i: (jnp.maximum(i - n_first, 0), 0)) for w in widths]
    full = lambda a: pl.BlockSpec(a.shape, lambda i: (0,) * a.ndim)
    row = lambda w: pl.BlockSpec((tm, w), lambda i: (i, 0))
    weights = [g_gla, w_glu, b_glu, w_out, g_ffn, wr_pair, b_r]
    n = na + nb
    return pl.pallas_call(
        functools.partial(_post_mix_kernel, n_first=n_first),
        grid=(n // tm,),
        in_specs=spec_a + spec_b + [full(a) for a in weights],
        out_specs=[row(D_MODEL), row(D_MODEL // 2), pl.BlockSpec((8, tm), lambda i: (0, i))],
        out_shape=[jax.ShapeDtypeStruct((n, D_MODEL), F32), jax.ShapeDtypeStruct((n, D_MODEL // 2), jnp.uint32),
                   jax.ShapeDtypeStruct((8, n), F32)],
        compiler_params=_params(("parallel",)),
        name="post_mix",
    )(*rows_a, *rows_b, *weights)


def _gather_start(idx_ref, idx_base, idx_stride, src_hbm, dst, sem, n):
    for r in range(n):
        row = idx_ref[idx_base + r * idx_stride]
        pltpu.make_async_copy(src_hbm.at[pl.ds(row, 1)], dst.at[pl.ds(r, 1)], sem).start()


def _pack_bf16_pairs(x):
    half = x.shape[1] // 2
    bits = lax.bitcast_convert_type(x.astype(BF16).astype(F32), jnp.uint32)
    return bits[:, half:] | (bits[:, :half] >> 16)


def _unpack_bf16_pairs(p):
    lo = lax.bitcast_convert_type(p << 16, F32).astype(BF16)
    hi = lax.bitcast_convert_type(p & jnp.uint32(0xFFFF0000), F32).astype(BF16)
    return jnp.concatenate([lo, hi], axis=1)


def _gather_wait(src_hbm, dst, sem, n):
    pltpu.make_async_copy(src_hbm.at[pl.ds(0, n)], dst.at[pl.ds(0, n)], sem).wait()


def _dispatch_rows_kernel(dest_ref, fill_ref, x_ref, xs_hbm, buf, sem, zsem, *, tm):
    i = pl.program_id(0)
    slot = i % 2

    def row_copy(r, k, s):
        row = dest_ref[(i * tm + r) * TOP_K + k]
        return pltpu.make_async_copy(buf.at[s, pl.ds(r, 1)], xs_hbm.at[pl.ds(row, 1)], sem.at[s])

    def wait_slot(s):
        for _ in range(TOP_K):
            pltpu.make_async_copy(buf.at[s], xs_hbm.at[pl.ds(0, tm)], sem.at[s]).wait()

    @pl.when(i == 0)
    def _zero_fill():
        buf[1] = jnp.zeros(buf.shape[1:], buf.dtype)

        def fill(n, carry):
            @pl.when(fill_ref[n] >= 0)
            def _():
                pltpu.make_async_copy(buf.at[1, pl.ds(0, MOE_TM)], xs_hbm.at[pl.ds(fill_ref[n] * MOE_TM, MOE_TM)],
                                      zsem).start()
            return carry
        lax.fori_loop(0, 2 * N_EXPERTS, fill, 0)

        def drain(n, carry):
            @pl.when(fill_ref[n] >= 0)
            def _():
                pltpu.make_async_copy(buf.at[1, pl.ds(0, MOE_TM)], xs_hbm.at[pl.ds(0, MOE_TM)], zsem).wait()
            return carry
        lax.fori_loop(0, 2 * N_EXPERTS, drain, 0)

    @pl.when(i >= 2)
    def _reuse():
        wait_slot(slot)

    buf[slot] = x_ref[...]

    for r in range(tm):
        for k in range(TOP_K):
            row_copy(r, k, slot).start()

    @pl.when(i == pl.num_programs(0) - 1)
    def _finish():
        wait_slot(slot)

        @pl.when(i >= 1)
        def _():
            wait_slot(1 - slot)


def _dispatch_rows(dest, fill_blocks, hn, n_rows, tm):
    n = hn.shape[0]
    assert tm >= MOE_TM
    return pl.pallas_call(
        functools.partial(_dispatch_rows_kernel, tm=tm),
        grid_spec=pltpu.PrefetchScalarGridSpec(
            num_scalar_prefetch=2,
            grid=(n // tm,),
            in_specs=[pl.BlockSpec((tm, D_MODEL // 2), lambda i, d, lb: (i, 0))],
            out_specs=pl.BlockSpec(memory_space=pl.ANY),
            scratch_shapes=[pltpu.VMEM((2, tm, D_MODEL // 2), jnp.uint32), pltpu.SemaphoreType.DMA((2,)),
                            pltpu.SemaphoreType.DMA],
        ),
        out_shape=jax.ShapeDtypeStruct((n_rows, D_MODEL // 2), jnp.uint32),
        compiler_params=_params(("arbitrary",)),
        name="dispatch_rows",
    )(dest, fill_blocks, hn)


def _moe_kernel(bexp_ref, eord_ref, next_ref, next2_ref, nused_ref, x_ref, wg_hbm, wu_hbm, wd_hbm, o_ref,
                wg_st, wu_st, wd_st, wsem, wg_bf, wu_bf, wd_bf):
    b = pl.program_id(0)
    nu = nused_ref[0]
    e = bexp_ref[b]
    new_expert = jnp.logical_or(b == 0, bexp_ref[jnp.maximum(b - 1, 0)] != e)

    def weight_copies(expert, slot):
        return [pltpu.make_async_copy(src.at[expert], dst.at[slot], wsem.at[slot])
                for src, dst in ((wg_hbm, wg_st), (wu_hbm, wu_st), (wd_hbm, wd_st))]

    def prefetch(expert, slot):
        @pl.when(expert >= 0)
        def _():
            for c in weight_copies(expert, slot):
                c.start()

    @pl.when(b == 0)
    def _prologue():
        for c in weight_copies(e, 0):
            c.start()
        prefetch(next_ref[0], 1)

    @pl.when(jnp.logical_and(b < nu, new_expert))
    def _new_expert():
        ordinal = eord_ref[b]
        slot = ordinal % MOE_WEIGHT_SLOTS
        for c in weight_copies(e, slot):
            c.wait()
        prefetch(next2_ref[b], (ordinal + 2) % MOE_WEIGHT_SLOTS)

        wg_bf[...] = wg_st[slot].astype(BF16)
        wu_bf[...] = wu_st[slot].astype(BF16)
        wd_bf[...] = wd_st[slot].astype(BF16)

    @pl.when(b < nu)
    def _run():
        x = _unpack_bf16_pairs(x_ref[...])
        gate = _dot(x, wg_bf[...])
        up = _dot(x, wu_bf[...])
        hid = ((gate * jax.nn.sigmoid(gate)) * up).astype(BF16)
        o_ref[...] = _dot(hid, wd_bf[...])

    @pl.when(b >= nu)
    def _skip():
        o_ref[...] = jnp.zeros(o_ref.shape, o_ref.dtype)


def _moe(block_exp, block_ord, block_next, block_next2, n_used, xs, w_gate, w_up, w_down):
    nblk = xs.shape[0] // MOE_TM
    any_spec = pl.BlockSpec(memory_space=pl.ANY)
    return pl.pallas_call(
        _moe_kernel,
        grid_spec=pltpu.PrefetchScalarGridSpec(
            num_scalar_prefetch=5,
            grid=(nblk,),
            in_specs=[pl.BlockSpec((MOE_TM, D_MODEL // 2),
                                   lambda b, be, eo, nx, nx2, nu: (jnp.minimum(b, nu[0] - 1), 0)),
                      any_spec, any_spec, any_spec],
            out_specs=pl.BlockSpec((MOE_TM, D_MODEL), lambda b, *_: (b, 0)),
            scratch_shapes=[pltpu.VMEM((MOE_WEIGHT_SLOTS, D_MODEL, EXPERT_HIDDEN), F32),
                            pltpu.VMEM((MOE_WEIGHT_SLOTS, D_MODEL, EXPERT_HIDDEN), F32),
                            pltpu.VMEM((MOE_WEIGHT_SLOTS, EXPERT_HIDDEN, D_MODEL), F32),
                            pltpu.SemaphoreType.DMA((MOE_WEIGHT_SLOTS,)),
                            pltpu.VMEM((D_MODEL, EXPERT_HIDDEN), BF16), pltpu.VMEM((D_MODEL, EXPERT_HIDDEN), BF16),
                            pltpu.VMEM((EXPERT_HIDDEN, D_MODEL), BF16)],
        ),
        out_shape=jax.ShapeDtypeStruct((nblk * MOE_TM, D_MODEL), F32),
        compiler_params=_params(("arbitrary",)),
        name="moe",
    )(block_exp, block_ord, block_next, block_next2, n_used, xs, w_gate, w_up, w_down)


def _final_kernel(dest_ref, h1_ref, wa_ref, wb_ref, g_ref, rows_hbm, y_ref, ybuf, sem, *, tm, tok_off):
    i = pl.program_id(0)

    def start(tile):
        slot = tile % 2
        base = (tok_off + tile * tm) * TOP_K
        for k in range(TOP_K):
            _gather_start(dest_ref, base + k, TOP_K, rows_hbm, ybuf.at[slot, k], sem.at[slot], tm)

    @pl.when(i == 0)
    def _first():
        start(i)

    @pl.when(i + 1 < pl.num_programs(0))
    def _next():
        start(i + 1)

    slot = i % 2
    for k in range(TOP_K):
        _gather_wait(rows_hbm, ybuf.at[slot, k], sem.at[slot], tm)
    moe = ybuf[slot, 0] * wa_ref[...] + ybuf[slot, 1] * wb_ref[...]
    y_ref[...] = _rms(h1_ref[...] + moe, g_ref[...])


def _final(dest, h1, wa, wb, g_final, rows, *, tm, n, row_off):
    off = row_off // tm
    row = lambda w: pl.BlockSpec((tm, w), lambda i, d: (i + off, 0))
    return pl.pallas_call(
        functools.partial(_final_kernel, tm=tm, tok_off=row_off),
        grid_spec=pltpu.PrefetchScalarGridSpec(
            num_scalar_prefetch=1,
            grid=(n // tm,),
            in_specs=[row(D_MODEL), row(1), row(1), pl.BlockSpec((1, D_MODEL), lambda i, d: (0, 0)),
                      pl.BlockSpec(memory_space=pl.ANY)],
            out_specs=pl.BlockSpec((tm, D_MODEL), lambda i, d: (i, 0)),
            scratch_shapes=[pltpu.VMEM((2, TOP_K, tm, D_MODEL), F32), pltpu.SemaphoreType.DMA((2,))],
        ),
        out_shape=jax.ShapeDtypeStruct((n, D_MODEL), F32),
        compiler_params=_params(("arbitrary",)),
        name="final",
    )(dest, h1, wa, wb, g_final, rows)


def _dispatch(eid):
    t = eid.shape[0]
    a = t * TOP_K
    assert a % MOE_TM == 0
    nblk = a // MOE_TM + N_EXPERTS
    flat = eid.reshape(-1)
    experts = jnp.arange(N_EXPERTS, dtype=jnp.int32)
    onehot = (flat[:, None] == experts[None, :]).astype(F32).reshape(a // MOE_TM, MOE_TM, N_EXPERTS)
    strict_lower = jnp.tril(jnp.ones((MOE_TM, MOE_TM), F32), -1)
    within = jnp.einsum('ij,bjk->bik', strict_lower, onehot)
    totals = jnp.sum(onehot, axis=1)
    before = jnp.cumsum(totals, axis=0) - totals
    counts = jnp.sum(totals, axis=0).astype(jnp.int32)
    padded = (counts + MOE_TM - 1) // MOE_TM * MOE_TM
    pends = jnp.cumsum(padded)
    offset = before + (pends - padded).astype(F32)[None, :]
    dest = jnp.sum((within + offset[:, None, :]) * onehot, axis=-1).reshape(-1)
    n_used = (pends[-1] // MOE_TM).astype(jnp.int32)
    blk = jnp.minimum(jnp.arange(nblk, dtype=jnp.int32), n_used - 1)
    block_exp = jnp.minimum(jnp.searchsorted(pends, blk * MOE_TM, side='right'), N_EXPERTS - 1).astype(jnp.int32)
    in_use = counts > 0
    ordinal = jnp.cumsum(in_use.astype(jnp.int32)) - 1
    later = lax.cummin(jnp.where(in_use, experts, N_EXPERTS), axis=0, reverse=True)
    nxt = jnp.concatenate([later[1:], jnp.full((1,), N_EXPERTS, jnp.int32)])
    nxt = jnp.where(nxt < N_EXPERTS, nxt, -1)
    nxt2 = jnp.where(nxt >= 0, nxt[jnp.maximum(nxt, 0)], -1)
    last_block = jnp.where(in_use, pends // MOE_TM - 1, -1)
    unused = n_used + experts
    fill_blocks = jnp.concatenate([last_block, jnp.where(unused < nblk, unused, -1)]).astype(jnp.int32)
    return (dest.astype(jnp.int32), nblk * MOE_TM, block_exp, ordinal[block_exp], nxt[block_exp], nxt2[block_exp],
            fill_blocks, n_used.reshape(1))


def kernel(x_prompt, x_sample, state_gla, state_s5_re, state_s5_im, meta, g_mix, w_in, w_gk2, b_gk, g_gla,
           lam_re, lam_im, log_dt, s5_b_re, s5_b_im, s5_c_re, s5_c_im, d_skip, w_glu, b_glu, w_out, g_ffn,
           w_rg, b_rg, w_re, b_re, w_gate, w_up, w_down, g_final):
    bp, tp, _ = x_prompt.shape
    bs, ts, _ = x_sample.shape
    l = 0

    w_cat = w_in[l].astype(BF16)
    wgk_hi, wgk_lo = _split2(jnp.pad(w_gk2[l], ((0, 128 - GLA_RANK), (0, 0))))
    g_mix2, b_gk2 = g_mix[l][None], b_gk[l][None]
    mats = _s5_matrices(lam_re[l], lam_im[l], log_dt[l], s5_b_re[l], s5_b_im[l], s5_c_re[l], s5_c_im[l],
                        d_skip[l])
    w_router = jnp.concatenate([w_rg[l], jnp.moveaxis(w_re[l], 0, 1).reshape(D_MODEL, N_EXPERTS)], axis=1)
    w_router = jnp.pad(w_router, ((0, 0), (0, ROUTER_LANES - N_GROUPS - N_EXPERTS)))
    wr_pair = jnp.concatenate(_split2(w_router), axis=1)
    b_router = jnp.pad(jnp.concatenate([b_rg[l], b_re[l].reshape(-1)]),
                       (0, ROUTER_LANES - N_GROUPS - N_EXPERTS))[None]
    w_glu_b, w_out_b = w_glu[l].astype(BF16), w_out[l].astype(BF16)
    g_gla2, b_glu2, g_ffn2 = g_gla[l].reshape(1, GLA_WIDTH), b_glu[l][None], g_ffn[l][None]

    proj = functools.partial(_in_proj, g_mix=g_mix2, w_cat=w_cat, wgk_hi=wgk_hi, wgk_lo=wgk_lo, b_gk=b_gk2)

    qm, km, vm, _, gm, um = proj(meta, tm=N_META)
    zero_s = jnp.zeros((1, GLA_HEADS, GLA_DK, GLA_DV), F32)
    _, s_meta = _gla(qm, km, vm, gm, zero_s, seq_len=N_META, chunk=N_META)
    zero_h = jnp.zeros((1, S5_NGB, 1, S5_SL), F32)
    _, hm_re, hm_im = _s5_long(um, mats, zero_h, zero_h, seq_len=N_META)

    xp = x_prompt.reshape(bp * tp, D_MODEL)
    qp, kp, vp, rp, gp, up = proj(xp, tm=512)
    op, gla_p = _gla(qp, kp, vp, gp, jnp.broadcast_to(s_meta, (bp,) + s_meta.shape[1:]),
                     seq_len=tp, chunk=GLA_CHUNK)
    y5p, hp_re, hp_im = _s5_pipelined(up, mats, jnp.broadcast_to(hm_re, (bp,) + hm_re.shape[1:]),
                                      jnp.broadcast_to(hm_im, (bp,) + hm_im.shape[1:]), seq_len=tp)

    xs = x_sample.reshape(bs * ts, D_MODEL)
    qs, ks, vs, rs, gs, us = proj(xs, tm=512)
    chunk_s = GLA_CHUNK if ts % GLA_CHUNK == 0 else ts
    os_, gla_s = _gla(qs, ks, vs, gs, state_gla[l], seq_len=ts, chunk=chunk_s)
    y5s, hs_re, hs_im = _s5_short(us, mats, state_s5_re[l].reshape(bs, -1), state_s5_im[l].reshape(bs, -1),
                                  seq_len=ts)

    np_rows, ns_rows = bp * tp, bs * ts
    h1, hn, route = _post_mix((xp, op, rp, y5p), (xs, os_, rs, y5s), g_gla2, w_glu_b, b_glu2, w_out_b, g_ffn2,
                               wr_pair, b_router, tm=256)

    eid = route[:TOP_K].T.astype(jnp.int32)
    wts = route[TOP_K:2 * TOP_K].T
    dest, n_sorted, block_exp, block_ord, block_next, block_next2, fill_blocks, n_used = _dispatch(eid)
    xs_rows = _dispatch_rows(dest, fill_blocks, hn, n_sorted, tm=256)
    out_rows = _moe(block_exp, block_ord, block_next, block_next2, n_used, xs_rows, w_gate[l], w_up[l], w_down[l])
    fin = functools.partial(_final, dest, h1, wts[:, 0:1], wts[:, 1:2], g_final[None], out_rows, tm=256)
    y_prompt = fin(n=np_rows, row_off=0)
    y_sample = fin(n=ns_rows, row_off=np_rows)

    return (y_prompt.reshape(bp, tp, D_MODEL), y_sample.reshape(bs, ts, D_MODEL),
            gla_p[None], hp_re.reshape(1, bp, S5_GROUPS, S5_STATE), hp_im.reshape(1, bp, S5_GROUPS, S5_STATE),
            gla_s[None], hs_re.reshape(1, bs, S5_GROUPS, S5_STATE), hs_im.reshape(1, bs, S5_GROUPS, S5_STATE))
```

```python
import functools
import math

import jax
import jax.numpy as jnp
from jax import lax
from jax.experimental import pallas as pl
from jax.experimental.pallas import tpu as pltpu

F32 = jnp.float32
BF16 = jnp.bfloat16

D_MODEL = 2048
N_META = 16
GLA_HEADS = 4
GLA_DK = 128
GLA_DV = 256
GLA_KEY_WIDTH = GLA_HEADS * GLA_DK
GLA_WIDTH = GLA_HEADS * GLA_DV
GLA_RANK = 16
GLA_GATE_NORM = 16.0
GLA_CHUNK = 64
GLA_SUB = 16
GLA_SEQS_PER_STEP = 8
MASKED_EXPONENT = -1e30
S5_WIDTH = 1024
S5_GROUP = 16
S5_GROUPS = 64
S5_STATE = 64
S5_GB = 8
S5_NGB = S5_GROUPS // S5_GB
S5_UL = S5_GB * S5_GROUP
S5_SL = S5_GB * S5_STATE
S5_LT = 2 * S5_SL // 128
S5_UNROLL = 4
S5_PIPE_CHUNKS = 1
N_GROUPS = 4
EXPERTS_PER_GROUP = 8
N_EXPERTS = N_GROUPS * EXPERTS_PER_GROUP
EXPERT_HIDDEN = 512
TOP_K = 2
EPS = 1e-6
ROUTER_LANES = 128
ROUTE_ROWS = -(-(N_GROUPS + N_EXPERTS) // 8) * 8
MOE_TM = 256
MOE_WEIGHT_SLOTS = 3
VMEM_LIMIT = 56 * 1024 * 1024

_dot = functools.partial(jnp.dot, preferred_element_type=F32)


def _split2(x):
    hi = x.astype(BF16)
    lo = (x - hi.astype(F32)).astype(BF16)
    return hi, lo


def _rms(x, g):
    return x * lax.rsqrt(jnp.mean(x * x, axis=-1, keepdims=True) + EPS) * g


def _params(sem):
    return pltpu.CompilerParams(dimension_semantics=sem, vmem_limit_bytes=VMEM_LIMIT)


def _in_proj_kernel(x_ref, g_ref, w_ref, wgk_hi_ref, wgk_lo_ref, bgk_ref,
                    q_ref, k_ref, v_ref, r_ref, gk_ref, u_ref):
    xb = _rms(x_ref[...], g_ref[...]).astype(BF16)
    kw = GLA_KEY_WIDTH
    q_ref[...] = _dot(xb, w_ref[:, 0:kw]) * (GLA_DK ** -0.5)
    k_ref[...] = _dot(xb, w_ref[:, kw:2 * kw])
    v_ref[...] = _dot(xb, w_ref[:, 2 * kw:2 * kw + GLA_WIDTH])
    r_ref[...] = _dot(xb, w_ref[:, 2 * kw + GLA_WIDTH:2 * kw + 2 * GLA_WIDTH])
    c0 = 2 * kw + 2 * GLA_WIDTH
    tail = _dot(xb, w_ref[:, c0:])
    u_ref[...] = tail[:, GLA_RANK:GLA_RANK + S5_WIDTH]
    a_low = tail[:, :128]
    a_hi, a_lo = _split2(a_low)
    z = (_dot(a_hi, wgk_hi_ref[...]) + _dot(a_hi, wgk_lo_ref[...]) + _dot(a_lo, wgk_hi_ref[...])
         + bgk_ref[...])
    gk_ref[...] = (jnp.minimum(z, 0.0) - jnp.log1p(jnp.exp(-jnp.abs(z)))) * (1.0 / GLA_GATE_NORM)


def _in_proj(x, g_mix, w_cat, wgk_hi, wgk_lo, b_gk, tm):
    n = x.shape[0]
    wcols = w_cat.shape[1]
    row = lambda w: pl.BlockSpec((tm, w), lambda i: (i, 0))
    full = lambda a: pl.BlockSpec(a.shape, lambda i: (0,) * a.ndim)
    widths = [GLA_KEY_WIDTH, GLA_KEY_WIDTH, GLA_WIDTH, GLA_WIDTH, GLA_KEY_WIDTH, S5_WIDTH]
    return pl.pallas_call(
        _in_proj_kernel,
        grid=(n // tm,),
        in_specs=[row(D_MODEL), full(g_mix),
                  pl.BlockSpec((D_MODEL, wcols), lambda i: (0, 0), pipeline_mode=pl.Buffered(1)),
                  full(wgk_hi), full(wgk_lo), full(b_gk)],
        out_specs=[row(w) for w in widths],
        out_shape=[jax.ShapeDtypeStruct((n, w), F32) for w in widths],
        compiler_params=_params(("parallel",)),
        name="in_proj",
    )(x, g_mix, w_cat, wgk_hi, wgk_lo, b_gk)


def _gla_chunk(q_ref, k_ref, v_ref, g_ref, o_ref, r0, C, consts, read_state, write_state):
    tril, piece_rows, piece_lanes = consts
    sub = min(C, GLA_SUB)
    nsub = C // sub
    g = g_ref[pl.ds(r0, C), :]
    g1 = g.astype(BF16)
    rem = g - g1.astype(F32)
    g2 = rem.astype(BF16)
    g3 = (rem - g2.astype(F32)).astype(BF16)
    b_all = _dot(tril, g1) + _dot(tril, g2) + _dot(tril, g3)
    for h in range(GLA_HEADS):
        ks = slice(h * GLA_DK, (h + 1) * GLA_DK)
        vs = slice(h * GLA_DV, (h + 1) * GLA_DV)
        q = q_ref[pl.ds(r0, C), ks]
        k = k_ref[pl.ds(r0, C), ks]
        b = b_all[:, ks]
        s_prev = read_state(h)
        vb = v_ref[pl.ds(r0, C), vs].astype(BF16)
        o_inter = _dot((q * jnp.exp(b)).astype(BF16), s_prev.astype(BF16))
        blocks = []
        for s in range(nsub):
            lo = s * sub
            bs, qs, ksub = b[lo:lo + sub], q[lo:lo + sub], k[lo:lo + sub]
            acc = o_inter[lo:lo + sub]
            if s > 0:
                anchor = b[lo - 1:lo]
                qd = (qs * jnp.exp(bs - anchor)).astype(BF16)
                kd = (k[:lo] * jnp.exp(anchor - b[:lo])).astype(BF16)
                sc = lax.dot_general(qd, kd, (((1,), (1,)), ((), ())), preferred_element_type=F32)
                acc = acc + _dot(sc.astype(BF16), vb[:lo])
            pieces = [jnp.zeros((8, 128), F32) for _ in range(sub // 8)]
            for jj in range(sub):
                for p in range(jj // 8, sub // 8):
                    r8 = slice(8 * p, 8 * p + 8)
                    diff = bs[r8] - bs[jj:jj + 1]
                    if 8 * p < jj:
                        diff = jnp.where(piece_rows + 8 * p >= jj, diff, MASKED_EXPONENT)
                    col = jnp.sum((qs[r8] * ksub[jj:jj + 1]) * jnp.exp(diff), axis=-1, keepdims=True)
                    pieces[p] = jnp.where(piece_lanes == jj, col, pieces[p])
            scores = jnp.concatenate(pieces, axis=0) if len(pieces) > 1 else pieces[0]
            acc = acc + _dot(scores[:, :sub].astype(BF16), vb[lo:lo + sub])
            blocks.append(acc)
        o_ref[pl.ds(r0, C), vs] = jnp.concatenate(blocks, axis=0) if nsub > 1 else blocks[0]
        b_last = b[C - 1:C]
        kdec = (k * jnp.exp(b_last - b)).astype(BF16)
        upd = lax.dot_general(kdec, vb, (((0,), (0,)), ((), ())), preferred_element_type=F32)
        dcol = jnp.broadcast_to(jnp.exp(b_last), (GLA_DK, GLA_DK)).T
        write_state(h, s_prev * jnp.concatenate([dcol, dcol], axis=1) + upd)


def _gla_consts(C):
    sub = min(C, GLA_SUB)
    ri = lax.broadcasted_iota(jnp.int32, (C, C), 0)
    ci = lax.broadcasted_iota(jnp.int32, (C, C), 1)
    tril = jnp.where(ri >= ci, 1.0, 0.0).astype(BF16)
    piece_rows = lax.broadcasted_iota(jnp.int32, (8, GLA_DK), 0)
    piece_lanes = lax.broadcasted_iota(jnp.int32, (8, 128), 1)
    return tril, piece_rows, piece_lanes


def _gla_long_kernel(q_ref, k_ref, v_ref, g_ref, s0_ref, o_ref, sout_ref, s_scr, *, chunk, n_inner):
    j = pl.program_id(1)
    consts = _gla_consts(chunk)

    @pl.when(j == 0)
    def _init():
        s_scr[...] = s0_ref[0]

    def write_state(h, s):
        s_scr[h] = s

    def chunk_body(c, carry):
        _gla_chunk(q_ref, k_ref, v_ref, g_ref, o_ref, pl.multiple_of(c * chunk, chunk), chunk, consts,
                   lambda h: s_scr[h], write_state)
        return carry

    lax.fori_loop(0, n_inner, chunk_body, 0)

    @pl.when(j == pl.num_programs(1) - 1)
    def _fin():
        sout_ref[0] = s_scr[...]


def _gla_short_kernel(q_ref, k_ref, v_ref, g_ref, s0_ref, o_ref, sout_ref, *, chunk, nb):
    consts = _gla_consts(chunk)

    def seq_body(n, carry):
        def write_state(h, s):
            sout_ref[n, h] = s

        _gla_chunk(q_ref, k_ref, v_ref, g_ref, o_ref, pl.multiple_of(n * chunk, chunk), chunk, consts,
                   lambda h: s0_ref[n, h], write_state)
        return carry

    lax.fori_loop(0, nb, seq_body, 0, unroll=2 if nb % 2 == 0 else 1)


def _gla(q, k, v, g, s0, *, seq_len, chunk):
    nseq = s0.shape[0]
    n = q.shape[0]
    out_shape = [jax.ShapeDtypeStruct((n, GLA_WIDTH), F32),
                 jax.ShapeDtypeStruct((nseq, GLA_HEADS, GLA_DK, GLA_DV), F32)]
    if seq_len == chunk:
        nb = min(nseq, GLA_SEQS_PER_STEP)
        rows = lambda w: pl.BlockSpec((nb * chunk, w), lambda s: (s, 0))
        st = pl.BlockSpec((nb, GLA_HEADS, GLA_DK, GLA_DV), lambda s: (s, 0, 0, 0))
        return pl.pallas_call(
            functools.partial(_gla_short_kernel, chunk=chunk, nb=nb),
            grid=(nseq // nb,),
            in_specs=[rows(GLA_KEY_WIDTH), rows(GLA_KEY_WIDTH), rows(GLA_WIDTH), rows(GLA_KEY_WIDTH), st],
            out_specs=[rows(GLA_WIDTH), st],
            out_shape=out_shape,
            compiler_params=_params(("parallel",)),
            name="gla_short",
        )(q, k, v, g, s0)
    rb = min(seq_len, 4 * chunk)
    nblk = seq_len // rb
    rows = lambda w: pl.BlockSpec((rb, w), lambda s, j: (s * nblk + j, 0))
    st = pl.BlockSpec((1, GLA_HEADS, GLA_DK, GLA_DV), lambda s, j: (s, 0, 0, 0))
    return pl.pallas_call(
        functools.partial(_gla_long_kernel, chunk=chunk, n_inner=rb // chunk),
        grid=(nseq, nblk),
        in_specs=[rows(GLA_KEY_WIDTH), rows(GLA_KEY_WIDTH), rows(GLA_WIDTH), rows(GLA_KEY_WIDTH), st],
        out_specs=[rows(GLA_WIDTH), st],
        out_shape=out_shape,
        scratch_shapes=[pltpu.VMEM((GLA_HEADS, GLA_DK, GLA_DV), F32)],
        compiler_params=_params(("parallel", "arbitrary")),
        name="gla_long",
    )(q, k, v, g, s0)


def _cmul(ar, ai, br, bi):
    return ar * br - ai * bi, ar * bi + ai * br


def _cpow(ar, ai, n):
    res = None
    while n:
        if n & 1:
            res = (ar, ai) if res is None else _cmul(res[0], res[1], ar, ai)
        n >>= 1
        if n:
            ar, ai = _cmul(ar, ai, ar, ai)
    return res


def _s5_segment_len(seq_len):
    s = -(-seq_len // 8)
    s = -(-s // 4) * 4
    return s if (s // 4) % 2 == 1 else s + 4


def _st_store(st_scr, rows, val):
    for t in range(S5_LT):
        st_scr[t, rows, :] = val[:, t * 128:(t + 1) * 128]


def _st_load(st_scr, rows):
    return jnp.concatenate([st_scr[t, rows, :] for t in range(S5_LT)], axis=1)


def _s5_long_kernel(u_ref, bb_ref, cc_ref, dsk_ref, are_ref, aim_ref, h0re_ref, h0im_ref,
                    y_ref, hre_ref, him_ref, st_scr, *, seq_len, seg):
    T, S, SL = seq_len, seg, S5_SL
    rc = min(T, 512)
    for c in range(T // rc):
        sl = slice(c * rc, (c + 1) * rc)
        _st_store(st_scr, sl, _dot(u_ref[sl, :].astype(BF16), bb_ref[0]))
    if 8 * S > T:
        _st_store(st_scr, slice(T, 8 * S), jnp.zeros((8 * S - T, 2 * SL), F32))
    a_re, a_im = are_ref[0], aim_ref[0]
    ar = jnp.broadcast_to(a_re, (8, SL))
    ai = jnp.broadcast_to(a_im, (8, SL))

    def step(i, h, store):
        strand = pl.ds(i, 8, stride=S)
        x = _st_load(st_scr, strand)
        mr, mi = _cmul(ar, ai, h[0], h[1])
        nr, ni = mr + x[:, :SL], mi + x[:, SL:]
        if store:
            _st_store(st_scr, strand, jnp.concatenate([nr, ni], axis=1))
        return nr, ni

    zero = jnp.zeros((8, SL), F32)
    fr, fi = lax.fori_loop(0, S, functools.partial(step, store=False), (zero, zero), unroll=S5_UNROLL)

    car_r, car_i = _s5_carries(fr, fi, a_re, a_im, S, h0re_ref[0, 0], h0im_ref[0, 0])

    lax.fori_loop(0, S, functools.partial(step, store=True), (car_r, car_i), unroll=S5_UNROLL)

    last = _st_load(st_scr, slice(T - 1, T))
    hre_ref[0, 0] = last[:, :SL]
    him_ref[0, 0] = last[:, SL:]
    for c in range(T // rc):
        sl = slice(c * rc, (c + 1) * rc)
        y_ref[sl, :] = _dot(_st_load(st_scr, sl).astype(BF16), cc_ref[0]) + dsk_ref[0] * u_ref[sl, :]


def _s5_long(u, mats, h0_re, h0_im, *, seq_len):
    bb, cc, dsk, a_re, a_im = mats
    nseq = h0_re.shape[0]
    seg = _s5_segment_len(seq_len)
    gb3 = lambda shape: pl.BlockSpec((1,) + shape, lambda s, g: (g, 0, 0))
    st = pl.BlockSpec((1, 1, 1, S5_SL), lambda s, g: (s, g, 0, 0))
    urow = pl.BlockSpec((seq_len, S5_UL), lambda s, g: (s, g))
    st_shape = jax.ShapeDtypeStruct((nseq, S5_NGB, 1, S5_SL), F32)
    return pl.pallas_call(
        functools.partial(_s5_long_kernel, seq_len=seq_len, seg=seg),
        grid=(nseq, S5_NGB),
        in_specs=[urow, gb3((S5_UL, 2 * S5_SL)), gb3((2 * S5_SL, S5_UL)), gb3((1, S5_UL)),
                  gb3((1, S5_SL)), gb3((1, S5_SL)), st, st],
        out_specs=[urow, st, st],
        out_shape=[jax.ShapeDtypeStruct(u.shape, F32), st_shape, st_shape],
        scratch_shapes=[pltpu.VMEM((S5_LT, 8 * seg, 128), F32)],
        compiler_params=_params(("parallel", "parallel")),
        name="s5_long",
    )(u, bb, cc, dsk, a_re, a_im, h0_re, h0_im)


def _s5_pipe_step(b_buf, s_buf, c_buf, u16, c16, ub_ref, uc_ref, bb_ref, cc_ref, dsk_ref, are_ref, aim_ref,
                  h0re_ref, h0im_ref, y_ref, hre_ref, him_ref, *, seq_len, seg):
    T, S, SL = seq_len, seg, S5_SL
    rc, sc = T // S5_PIPE_CHUNKS, S // S5_PIPE_CHUNKS
    if 8 * S > T:
        _st_store(b_buf, slice(T, 8 * S), jnp.zeros((8 * S - T, 2 * SL), F32))
    a_re, a_im = are_ref[0], aim_ref[0]
    ar = jnp.broadcast_to(a_re, (8, SL))
    ai = jnp.broadcast_to(a_im, (8, SL))

    def scan_chunk(k, h, store):
        for d in range(sc):
            strand = pl.ds(k * sc + d, 8, stride=S)
            x = _st_load(s_buf, strand)
            mr, mi = _cmul(ar, ai, h[0], h[1])
            h = (mr + x[:, :SL], mi + x[:, SL:])
            if store:
                _st_store(s_buf, strand, jnp.concatenate(h, axis=1))
        return h

    u16[...] = ub_ref[...].astype(BF16)
    for c in range(S5_PIPE_CHUNKS):
        rows = slice(c * rc, (c + 1) * rc)
        c16[rows, :] = _st_load(c_buf, rows).astype(BF16)

    def end_states(k, h):
        rows = pl.ds(pl.multiple_of(k * rc, rc), rc)
        _st_store(b_buf, rows, _dot(u16[rows, :], bb_ref[0]))
        return scan_chunk(k, h, False)

    zero = jnp.zeros((8, SL), F32)
    fr, fi = lax.fori_loop(0, S5_PIPE_CHUNKS, end_states, (zero, zero))
    car_r, car_i = _s5_carries(fr, fi, a_re, a_im, S, h0re_ref[0, 0], h0im_ref[0, 0])

    def states(k, h):
        rows = pl.ds(pl.multiple_of(k * rc, rc), rc)
        y_ref[rows, :] = _dot(c16[rows, :], cc_ref[0]) + dsk_ref[0] * uc_ref[rows, :]
        return scan_chunk(k, h, True)

    lax.fori_loop(0, S5_PIPE_CHUNKS, states, (car_r, car_i))
    last = _st_load(s_buf, slice(T - 1, T))
    hre_ref[0, 0] = last[:, :SL]
    him_ref[0, 0] = last[:, SL:]


def _s5_carries(fr, fi, a_re, a_im, seg, h0_re, h0_im):
    as_re, as_im = _cpow(a_re, a_im, seg)
    rows = lax.broadcasted_iota(jnp.int32, fr.shape, 0)
    cr, ci = h0_re, h0_im
    car_r, car_i = jnp.zeros_like(fr), jnp.zeros_like(fi)
    for r in range(8):
        car_r = jnp.where(rows == r, cr, car_r)
        car_i = jnp.where(rows == r, ci, car_i)
        if r < 7:
            mr, mi = _cmul(as_re, as_im, cr, ci)
            cr, ci = mr + fr[r:r + 1], mi + fi[r:r + 1]
    return car_r, car_i


def _s5_pipe_kernel(*refs, seq_len, seg):
    bufs, u16, c16 = refs[-5:-2], refs[-2], refs[-1]
    refs = refs[:-5]
    t = pl.program_id(0)

    @pl.when(t == 0)
    def _init():
        for buf in bufs:
            buf[...] = jnp.zeros(buf.shape, buf.dtype)

    for r in range(3):
        @pl.when(t % 3 == r)
        def _rotation(r=r):
            _s5_pipe_step(bufs[r], bufs[(r + 2) % 3], bufs[(r + 1) % 3], u16, c16, *refs, seq_len=seq_len, seg=seg)


def _s5_pipelined(u, mats, h0_re, h0_im, *, seq_len):
    bb, cc, dsk, a_re, a_im = mats
    nseq = h0_re.shape[0]
    n_items = nseq * S5_NGB
    seg = _s5_segment_len(seq_len)
    assert seq_len % (8 * S5_PIPE_CHUNKS) == 0 and seg % S5_PIPE_CHUNKS == 0

    def item(lag):
        return lambda t: jnp.clip(t - lag, 0, n_items - 1)

    def per_gb(shape, lag):
        return pl.BlockSpec((1,) + shape, lambda t: (item(lag)(t) % S5_NGB, 0, 0))

    def rows(lag):
        return pl.BlockSpec((seq_len, S5_UL), lambda t: (item(lag)(t) // S5_NGB, item(lag)(t) % S5_NGB))

    st = pl.BlockSpec((1, 1, 1, S5_SL), lambda t: (item(1)(t) // S5_NGB, item(1)(t) % S5_NGB, 0, 0))
    st_shape = jax.ShapeDtypeStruct((nseq, S5_NGB, 1, S5_SL), F32)
    buf = pltpu.VMEM((S5_LT, 8 * seg, 128), F32)
    return pl.pallas_call(
        functools.partial(_s5_pipe_kernel, seq_len=seq_len, seg=seg),
        grid=(n_items + 2,),
        in_specs=[rows(0), rows(2), per_gb((S5_UL, 2 * S5_SL), 0), per_gb((2 * S5_SL, S5_UL), 2),
                  per_gb((1, S5_UL), 2), per_gb((1, S5_SL), 1), per_gb((1, S5_SL), 1), st, st],
        out_specs=[rows(2), st, st],
        out_shape=[jax.ShapeDtypeStruct(u.shape, F32), st_shape, st_shape],
        scratch_shapes=[buf, buf, buf, pltpu.VMEM((seq_len, S5_UL), BF16), pltpu.VMEM((seq_len, 2 * S5_SL), BF16)],
        compiler_params=_params(("arbitrary",)),
        name="s5_pipelined",
    )(u, u, bb, cc, dsk, a_re, a_im, h0_re, h0_im)


def _s5_short_kernel(u_ref, bb_ref, cc_ref, dsk_ref, are_ref, aim_ref, h0re_ref, h0im_ref,
                     y_ref, hre_ref, him_ref, st_scr, *, seq_len, nseq):
    SL = S5_SL
    _st_store(st_scr, slice(None), _dot(u_ref[...].astype(BF16), bb_ref[0]))
    ar = jnp.broadcast_to(are_ref[0], (nseq, SL))
    ai = jnp.broadcast_to(aim_ref[0], (nseq, SL))
    hr, hi = h0re_ref[...], h0im_ref[...]
    for t in range(seq_len):
        step = pl.ds(t, nseq, stride=seq_len)
        x = _st_load(st_scr, step)
        mr, mi = _cmul(ar, ai, hr, hi)
        hr, hi = mr + x[:, :SL], mi + x[:, SL:]
        _st_store(st_scr, step, jnp.concatenate([hr, hi], axis=1))
    hre_ref[...] = hr
    him_ref[...] = hi
    y_ref[...] = _dot(_st_load(st_scr, slice(None)).astype(BF16), cc_ref[0]) + dsk_ref[0] * u_ref[...]


def _s5_short(u, mats, h0_re, h0_im, *, seq_len):
    bb, cc, dsk, a_re, a_im = mats
    nseq = h0_re.shape[0]
    n = nseq * seq_len
    gb3 = lambda shape: pl.BlockSpec((1,) + shape, lambda g: (g, 0, 0))
    st = pl.BlockSpec((nseq, S5_SL), lambda g: (0, g))
    urow = pl.BlockSpec((n, S5_UL), lambda g: (0, g))
    st_shape = jax.ShapeDtypeStruct(h0_re.shape, F32)
    return pl.pallas_call(
        functools.partial(_s5_short_kernel, seq_len=seq_len, nseq=nseq),
        grid=(S5_NGB,),
        in_specs=[urow, gb3((S5_UL, 2 * S5_SL)), gb3((2 * S5_SL, S5_UL)), gb3((1, S5_UL)),
                  gb3((1, S5_SL)), gb3((1, S5_SL)), st, st],
        out_specs=[urow, st, st],
        out_shape=[jax.ShapeDtypeStruct(u.shape, F32), st_shape, st_shape],
        scratch_shapes=[pltpu.VMEM((S5_LT, n, 128), F32)],
        compiler_params=_params(("parallel",)),
        name="s5_short",
    )(u, bb, cc, dsk, a_re, a_im, h0_re, h0_im)


def _s5_matrices(lam_re, lam_im, log_dt, b_re, b_im, c_re, c_im, d_skip):
    dt = jnp.exp(log_dt)[:, None]
    mag = jnp.exp(lam_re * dt)
    ab_re, ab_im = mag * jnp.cos(lam_im * dt), mag * jnp.sin(lam_im * dt)
    den = lam_re * lam_re + lam_im * lam_im
    f_re = ((ab_re - 1.0) * lam_re + ab_im * lam_im) / den
    f_im = (ab_im * lam_re - (ab_re - 1.0) * lam_im) / den
    bb_re = f_re[..., None] * b_re - f_im[..., None] * b_im
    bb_im = f_re[..., None] * b_im + f_im[..., None] * b_re
    eye = jnp.eye(S5_GB, dtype=F32)

    def in_mat(m):
        m = m.reshape(S5_NGB, S5_GB, S5_STATE, S5_GROUP)
        return jnp.einsum('bgpc,gh->bgchp', m, eye).reshape(S5_NGB, S5_UL, S5_SL)

    def out_mat(m):
        m = m.reshape(S5_NGB, S5_GB, S5_GROUP, S5_STATE)
        return jnp.einsum('bgcp,gh->bgphc', m, eye).reshape(S5_NGB, S5_SL, S5_UL)

    bb = jnp.concatenate([in_mat(bb_re), in_mat(bb_im)], axis=2).astype(BF16)
    cc = jnp.concatenate([out_mat(c_re), out_mat(-c_im)], axis=1).astype(BF16)
    dsk = d_skip.reshape(S5_NGB, 1, S5_UL)
    return bb, cc, dsk, ab_re.reshape(S5_NGB, 1, S5_SL), ab_im.reshape(S5_NGB, 1, S5_SL)


def _post_mix_tile(x_ref, o_ref, r_ref, y5_ref, ggla_ref, wglu_ref, bglu_ref, wout_ref, gffn_ref,
                   wr_ref, br_ref, h1_ref, hn_ref, lg_ref):
    o = o_ref[...]
    parts = []
    for h in range(GLA_HEADS):
        oh = o[:, h * GLA_DV:(h + 1) * GLA_DV]
        parts.append(oh * lax.rsqrt(jnp.mean(oh * oh, axis=-1, keepdims=True) + EPS))
    r = r_ref[...]
    o_gla = (jnp.concatenate(parts, axis=1) * ggla_ref[...]) * (r * jax.nn.sigmoid(r))
    y5 = y5_ref[...]
    z = y5 * (0.5 * (1.0 + jnp.tanh(math.sqrt(2.0 / math.pi) * (y5 + 0.044715 * (y5 * y5 * y5)))))
    o_s5 = z * jax.nn.sigmoid(_dot(z.astype(BF16), wglu_ref[...]) + bglu_ref[...])
    att = (_dot(o_gla.astype(BF16), wout_ref[0:GLA_WIDTH, :])
           + _dot(o_s5.astype(BF16), wout_ref[GLA_WIDTH:GLA_WIDTH + S5_WIDTH, :]))
    h1 = x_ref[...] + att
    h1_ref[...] = h1
    hn = _rms(h1, gffn_ref[...])
    hn_ref[...] = _pack_bf16_pairs(hn)
    hn_hi, hn_lo = _split2(hn)
    both = _dot(hn_hi, wr_ref[...])
    logits = (both[:, :ROUTER_LANES] + both[:, ROUTER_LANES:] + _dot(hn_lo, wr_ref[:, :ROUTER_LANES])
              + br_ref[...])
    lg_ref[...] = _route_tile(logits)


def _route_tile(logits):
    lt = logits.T[:ROUTE_ROWS]
    row = lax.broadcasted_iota(jnp.int32, lt.shape, 0)
    ninf = float('-inf')

    def first_max(vals):
        m = jnp.max(vals, axis=0, keepdims=True)
        return m, jnp.min(jnp.where(vals == m, row, ROUTE_ROWS), axis=0, keepdims=True)

    is_group = row < N_GROUPS
    gmax, gsel = first_max(jnp.where(is_group, lt, ninf))
    p_group = 1.0 / jnp.sum(jnp.where(is_group, jnp.exp(lt - gmax), 0.0), axis=0, keepdims=True)
    first = N_GROUPS + gsel * EXPERTS_PER_GROUP
    in_group = jnp.logical_and(row >= first, row < first + EXPERTS_PER_GROUP)
    cand = jnp.where(in_group, lt, ninf)
    m1, i1 = first_max(cand)
    m2, i2 = first_max(jnp.where(row == i1, ninf, cand))
    t = jnp.exp(m2 - m1)
    p1 = 1.0 / (1.0 + t)
    out_row = lax.broadcasted_iota(jnp.int32, (8, lt.shape[1]), 0)
    out = jnp.where(out_row == 0, (i1 - N_GROUPS).astype(F32), 0.0)
    out = jnp.where(out_row == 1, (i2 - N_GROUPS).astype(F32), out)
    out = jnp.where(out_row == 2, p_group * p1, out)
    return jnp.where(out_row == 3, p_group * (t * p1), out)


def _post_mix(x, o, r, y5, g_gla, w_glu, b_glu, w_out, g_ffn, wr_pair, b_r, tm):
    n = x.shape[0]
    row = lambda w: pl.BlockSpec((tm, w), lambda i: (i, 0))
    full = lambda a: pl.BlockSpec(a.shape, lambda i: (0,) * a.ndim, pipeline_mode=pl.Buffered(1))
    weights = [g_gla, w_glu, b_glu, w_out, g_ffn, wr_pair, b_r]
    return pl.pallas_call(
        _post_mix_tile,
        grid=(n // tm,),
        in_specs=[row(D_MODEL), row(GLA_WIDTH), row(GLA_WIDTH), row(S5_WIDTH)] + [full(a) for a in weights],
        out_specs=[row(D_MODEL), row(D_MODEL // 2), pl.BlockSpec((8, tm), lambda i: (0, i))],
        out_shape=[jax.ShapeDtypeStruct((n, D_MODEL), F32), jax.ShapeDtypeStruct((n, D_MODEL // 2), jnp.uint32),
                   jax.ShapeDtypeStruct((8, n), F32)],
        compiler_params=_params(("parallel",)),
        name="post_mix",
    )(x, o, r, y5, *weights)


def _gather_start(idx_ref, idx_base, idx_stride, src_hbm, dst, sem, n):
    for r in range(n):
        row = idx_ref[idx_base + r * idx_stride]
        pltpu.make_async_copy(src_hbm.at[pl.ds(row, 1)], dst.at[pl.ds(r, 1)], sem).start()


def _pack_bf16_pairs(x):
    half = x.shape[1] // 2
    bits = lax.bitcast_convert_type(x.astype(BF16).astype(F32), jnp.uint32)
    return bits[:, half:] | (bits[:, :half] >> 16)


def _unpack_bf16_pairs(p):
    lo = lax.bitcast_convert_type(p << 16, F32).astype(BF16)
    hi = lax.bitcast_convert_type(p & jnp.uint32(0xFFFF0000), F32).astype(BF16)
    return jnp.concatenate([lo, hi], axis=1)


def _gather_wait(src_hbm, dst, sem, n):
    pltpu.make_async_copy(src_hbm.at[pl.ds(0, n)], dst.at[pl.ds(0, n)], sem).wait()


def _dispatch_rows_kernel(dest_ref, fill_ref, xa_ref, xb_ref, xs_hbm, buf, sem, zsem, *, tm, n_first):
    i = pl.program_id(0)
    slot = i % 2

    def row_copy(r, k, s):
        row = dest_ref[(i * tm + r) * TOP_K + k]
        return pltpu.make_async_copy(buf.at[s, pl.ds(r, 1)], xs_hbm.at[pl.ds(row, 1)], sem.at[s])

    def wait_slot(s):
        for _ in range(TOP_K):
            pltpu.make_async_copy(buf.at[s], xs_hbm.at[pl.ds(0, tm)], sem.at[s]).wait()

    @pl.when(i == 0)
    def _zero_fill():
        buf[1] = jnp.zeros(buf.shape[1:], buf.dtype)

        def fill(n, carry):
            @pl.when(fill_ref[n] >= 0)
            def _():
                pltpu.make_async_copy(buf.at[1, pl.ds(0, MOE_TM)], xs_hbm.at[pl.ds(fill_ref[n] * MOE_TM, MOE_TM)],
                                      zsem).start()
            return carry
        lax.fori_loop(0, 2 * N_EXPERTS, fill, 0)

        def drain(n, carry):
            @pl.when(fill_ref[n] >= 0)
            def _():
                pltpu.make_async_copy(buf.at[1, pl.ds(0, MOE_TM)], xs_hbm.at[pl.ds(0, MOE_TM)], zsem).wait()
            return carry
        lax.fori_loop(0, 2 * N_EXPERTS, drain, 0)

    @pl.when(i >= 2)
    def _reuse():
        wait_slot(slot)

    @pl.when(i < n_first)
    def _from_a():
        buf[slot] = xa_ref[...]

    @pl.when(i >= n_first)
    def _from_b():
        buf[slot] = xb_ref[...]

    for r in range(tm):
        for k in range(TOP_K):
            row_copy(r, k, slot).start()

    @pl.when(i == pl.num_programs(0) - 1)
    def _finish():
        wait_slot(slot)

        @pl.when(i >= 1)
        def _():
            wait_slot(1 - slot)


def _dispatch_rows(dest, fill_blocks, hn_a, hn_b, n_rows, tm):
    n_first = hn_a.shape[0] // tm
    n = hn_a.shape[0] + hn_b.shape[0]
    assert tm >= MOE_TM
    return pl.pallas_call(
        functools.partial(_dispatch_rows_kernel, tm=tm, n_first=n_first),
        grid_spec=pltpu.PrefetchScalarGridSpec(
            num_scalar_prefetch=2,
            grid=(n // tm,),
            in_specs=[pl.BlockSpec((tm, D_MODEL // 2), lambda i, d, lb: (jnp.minimum(i, n_first - 1), 0)),
                      pl.BlockSpec((tm, D_MODEL // 2), lambda i, d, lb: (jnp.maximum(i - n_first, 0), 0))],
            out_specs=pl.BlockSpec(memory_space=pl.ANY),
            scratch_shapes=[pltpu.VMEM((2, tm, D_MODEL // 2), jnp.uint32), pltpu.SemaphoreType.DMA((2,)),
                            pltpu.SemaphoreType.DMA],
        ),
        out_shape=jax.ShapeDtypeStruct((n_rows, D_MODEL // 2), jnp.uint32),
        compiler_params=_params(("arbitrary",)),
        name="dispatch_rows",
    )(dest, fill_blocks, hn_a, hn_b)


def _moe_kernel(bexp_ref, eord_ref, next_ref, next2_ref, nused_ref, x_ref, wg_hbm, wu_hbm, wd_hbm, o_ref,
                wg_st, wu_st, wd_st, wsem, wg_bf, wu_bf, wd_bf):
    b = pl.program_id(0)
    nu = nused_ref[0]
    e = bexp_ref[b]
    new_expert = jnp.logical_or(b == 0, bexp_ref[jnp.maximum(b - 1, 0)] != e)

    def weight_copies(expert, slot):
        return [pltpu.make_async_copy(src.at[expert], dst.at[slot], wsem.at[slot])
                for src, dst in ((wg_hbm, wg_st), (wu_hbm, wu_st), (wd_hbm, wd_st))]

    def prefetch(expert, slot):
        @pl.when(expert >= 0)
        def _():
            for c in weight_copies(expert, slot):
                c.start()

    @pl.when(b == 0)
    def _prologue():
        for c in weight_copies(e, 0):
            c.start()
        prefetch(next_ref[0], 1)

    @pl.when(jnp.logical_and(b < nu, new_expert))
    def _new_expert():
        ordinal = eord_ref[b]
        slot = ordinal % MOE_WEIGHT_SLOTS
        for c in weight_copies(e, slot):
            c.wait()
        prefetch(next2_ref[b], (ordinal + 2) % MOE_WEIGHT_SLOTS)

        wg_bf[...] = wg_st[slot].astype(BF16)
        wu_bf[...] = wu_st[slot].astype(BF16)
        wd_bf[...] = wd_st[slot].astype(BF16)

    @pl.when(b < nu)
    def _run():
        x = _unpack_bf16_pairs(x_ref[...])
        gate = _dot(x, wg_bf[...])
        up = _dot(x, wu_bf[...])
        hid = ((gate * jax.nn.sigmoid(gate)) * up).astype(BF16)
        o_ref[...] = _pack_bf16_pairs(_dot(hid, wd_bf[...]))

    @pl.when(b >= nu)
    def _skip():
        o_ref[...] = jnp.zeros(o_ref.shape, o_ref.dtype)


def _moe(block_exp, block_ord, block_next, block_next2, n_used, xs, w_gate, w_up, w_down):
    nblk = xs.shape[0] // MOE_TM
    any_spec = pl.BlockSpec(memory_space=pl.ANY)
    return pl.pallas_call(
        _moe_kernel,
        grid_spec=pltpu.PrefetchScalarGridSpec(
            num_scalar_prefetch=5,
            grid=(nblk,),
            in_specs=[pl.BlockSpec((MOE_TM, D_MODEL // 2),
                                   lambda b, be, eo, nx, nx2, nu: (jnp.minimum(b, nu[0] - 1), 0)),
                      any_spec, any_spec, any_spec],
            out_specs=pl.BlockSpec((MOE_TM, D_MODEL // 2), lambda b, *_: (b, 0)),
            scratch_shapes=[pltpu.VMEM((MOE_WEIGHT_SLOTS, D_MODEL, EXPERT_HIDDEN), F32),
                            pltpu.VMEM((MOE_WEIGHT_SLOTS, D_MODEL, EXPERT_HIDDEN), F32),
                            pltpu.VMEM((MOE_WEIGHT_SLOTS, EXPERT_HIDDEN, D_MODEL), F32),
                            pltpu.SemaphoreType.DMA((MOE_WEIGHT_SLOTS,)),
                            pltpu.VMEM((D_MODEL, EXPERT_HIDDEN), BF16), pltpu.VMEM((D_MODEL, EXPERT_HIDDEN), BF16),
                            pltpu.VMEM((EXPERT_HIDDEN, D_MODEL), BF16)],
        ),
        out_shape=jax.ShapeDtypeStruct((nblk * MOE_TM, D_MODEL // 2), jnp.uint32),
        compiler_params=_params(("arbitrary",)),
        name="moe",
    )(block_exp, block_ord, block_next, block_next2, n_used, xs, w_gate, w_up, w_down)


def _final_kernel(dest_ref, h1_ref, wa_ref, wb_ref, g_ref, rows_hbm, y_ref, ybuf, sem, *, tm, tok_off):
    i = pl.program_id(0)

    def start(tile):
        slot = tile % 2
        base = (tok_off + tile * tm) * TOP_K
        for k in range(TOP_K):
            _gather_start(dest_ref, base + k, TOP_K, rows_hbm, ybuf.at[slot, k], sem.at[slot], tm)

    @pl.when(i == 0)
    def _first():
        start(i)

    @pl.when(i + 1 < pl.num_programs(0))
    def _next():
        start(i + 1)

    slot = i % 2
    for k in range(TOP_K):
        _gather_wait(rows_hbm, ybuf.at[slot, k], sem.at[slot], tm)
    ya = _unpack_bf16_pairs(ybuf[slot, 0]).astype(F32)
    yb = _unpack_bf16_pairs(ybuf[slot, 1]).astype(F32)
    moe = ya * wa_ref[...] + yb * wb_ref[...]
    y_ref[...] = _rms(h1_ref[...] + moe, g_ref[...])


def _final(dest, wa, wb, g_final, rows, h1, *, tm, tok_off):
    n = h1.shape[0]
    off = tok_off // tm
    row = lambda w: pl.BlockSpec((tm, w), lambda i, d: (i + off, 0))
    return pl.pallas_call(
        functools.partial(_final_kernel, tm=tm, tok_off=tok_off),
        grid_spec=pltpu.PrefetchScalarGridSpec(
            num_scalar_prefetch=1,
            grid=(n // tm,),
            in_specs=[pl.BlockSpec((tm, D_MODEL), lambda i, d: (i, 0)), row(1), row(1),
                      pl.BlockSpec((1, D_MODEL), lambda i, d: (0, 0)),
                      pl.BlockSpec(memory_space=pl.ANY)],
            out_specs=pl.BlockSpec((tm, D_MODEL), lambda i, d: (i, 0)),
            scratch_shapes=[pltpu.VMEM((2, TOP_K, tm, D_MODEL // 2), jnp.uint32), pltpu.SemaphoreType.DMA((2,))],
        ),
        out_shape=jax.ShapeDtypeStruct((n, D_MODEL), F32),
        compiler_params=_params(("arbitrary",)),
        name="final",
    )(dest, h1, wa, wb, g_final, rows)


def _dispatch(eid):
    t = eid.shape[0]
    a = t * TOP_K
    assert a % MOE_TM == 0
    nblk = a // MOE_TM + N_EXPERTS
    flat = eid.reshape(-1)
    experts = jnp.arange(N_EXPERTS, dtype=jnp.int32)
    onehot = (flat[:, None] == experts[None, :]).astype(F32).reshape(a // MOE_TM, MOE_TM, N_EXPERTS)
    strict_lower = jnp.tril(jnp.ones((MOE_TM, MOE_TM), F32), -1)
    within = jnp.einsum('ij,bjk->bik', strict_lower, onehot)
    totals = jnp.sum(onehot, axis=1)
    before = jnp.cumsum(totals, axis=0) - totals
    counts = jnp.sum(totals, axis=0).astype(jnp.int32)
    padded = (counts + MOE_TM - 1) // MOE_TM * MOE_TM
    pends = jnp.cumsum(padded)
    offset = before + (pends - padded).astype(F32)[None, :]
    dest = jnp.sum((within + offset[:, None, :]) * onehot, axis=-1).reshape(-1)
    n_used = (pends[-1] // MOE_TM).astype(jnp.int32)
    blk = jnp.minimum(jnp.arange(nblk, dtype=jnp.int32), n_used - 1)
    block_exp = jnp.minimum(jnp.searchsorted(pends, blk * MOE_TM, side='right'), N_EXPERTS - 1).astype(jnp.int32)
    in_use = counts > 0
    ordinal = jnp.cumsum(in_use.astype(jnp.int32)) - 1
    later = lax.cummin(jnp.where(in_use, experts, N_EXPERTS), axis=0, reverse=True)
    nxt = jnp.concatenate([later[1:], jnp.full((1,), N_EXPERTS, jnp.int32)])
    nxt = jnp.where(nxt < N_EXPERTS, nxt, -1)
    nxt2 = jnp.where(nxt >= 0, nxt[jnp.maximum(nxt, 0)], -1)
    last_block = jnp.where(in_use, pends // MOE_TM - 1, -1)
    unused = n_used + experts
    fill_blocks = jnp.concatenate([last_block, jnp.where(unused < nblk, unused, -1)]).astype(jnp.int32)
    return (dest.astype(jnp.int32), nblk * MOE_TM, block_exp, ordinal[block_exp], nxt[block_exp], nxt2[block_exp],
            fill_blocks, n_used.reshape(1))


def kernel(x_prompt, x_sample, state_gla, state_s5_re, state_s5_im, meta, g_mix, w_in, w_gk2, b_gk, g_gla,
           lam_re, lam_im, log_dt, s5_b_re, s5_b_im, s5_c_re, s5_c_im, d_skip, w_glu, b_glu, w_out, g_ffn,
           w_rg, b_rg, w_re, b_re, w_gate, w_up, w_down, g_final):
    bp, tp, _ = x_prompt.shape
    bs, ts, _ = x_sample.shape
    l = 0

    w_cat = w_in[l].astype(BF16)
    wgk_hi, wgk_lo = _split2(jnp.pad(w_gk2[l], ((0, 128 - GLA_RANK), (0, 0))))
    g_mix2, b_gk2 = g_mix[l][None], b_gk[l][None]
    mats = _s5_matrices(lam_re[l], lam_im[l], log_dt[l], s5_b_re[l], s5_b_im[l], s5_c_re[l], s5_c_im[l],
                        d_skip[l])
    w_router = jnp.concatenate([w_rg[l], jnp.moveaxis(w_re[l], 0, 1).reshape(D_MODEL, N_EXPERTS)], axis=1)
    w_router = jnp.pad(w_router, ((0, 0), (0, ROUTER_LANES - N_GROUPS - N_EXPERTS)))
    wr_pair = jnp.concatenate(_split2(w_router), axis=1)
    b_router = jnp.pad(jnp.concatenate([b_rg[l], b_re[l].reshape(-1)]),
                       (0, ROUTER_LANES - N_GROUPS - N_EXPERTS))[None]
    w_glu_b, w_out_b = w_glu[l].astype(BF16), w_out[l].astype(BF16)
    g_gla2, b_glu2, g_ffn2 = g_gla[l].reshape(1, GLA_WIDTH), b_glu[l][None], g_ffn[l][None]

    proj = functools.partial(_in_proj, g_mix=g_mix2, w_cat=w_cat, wgk_hi=wgk_hi, wgk_lo=wgk_lo, b_gk=b_gk2)

    qm, km, vm, _, gm, um = proj(meta, tm=N_META)
    zero_s = jnp.zeros((1, GLA_HEADS, GLA_DK, GLA_DV), F32)
    _, s_meta = _gla(qm, km, vm, gm, zero_s, seq_len=N_META, chunk=N_META)
    zero_h = jnp.zeros((1, S5_NGB, 1, S5_SL), F32)
    _, hm_re, hm_im = _s5_long(um, mats, zero_h, zero_h, seq_len=N_META)

    xp = x_prompt.reshape(bp * tp, D_MODEL)
    qp, kp, vp, rp, gp, up = proj(xp, tm=512)
    op, gla_p = _gla(qp, kp, vp, gp, jnp.broadcast_to(s_meta, (bp,) + s_meta.shape[1:]),
                     seq_len=tp, chunk=GLA_CHUNK)
    y5p, hp_re, hp_im = _s5_pipelined(up, mats, jnp.broadcast_to(hm_re, (bp,) + hm_re.shape[1:]),
                                      jnp.broadcast_to(hm_im, (bp,) + hm_im.shape[1:]), seq_len=tp)

    xs = x_sample.reshape(bs * ts, D_MODEL)
    qs, ks, vs, rs, gs, us = proj(xs, tm=512)
    chunk_s = GLA_CHUNK if ts % GLA_CHUNK == 0 else ts
    os_, gla_s = _gla(qs, ks, vs, gs, state_gla[l], seq_len=ts, chunk=chunk_s)
    y5s, hs_re, hs_im = _s5_short(us, mats, state_s5_re[l].reshape(bs, -1), state_s5_im[l].reshape(bs, -1),
                                  seq_len=ts)

    post = functools.partial(_post_mix, g_gla=g_gla2, w_glu=w_glu_b, b_glu=b_glu2, w_out=w_out_b, g_ffn=g_ffn2,
                             wr_pair=wr_pair, b_r=b_router, tm=512)
    h1p, hnp, route_p = post(xp, op, rp, y5p)
    h1s, hns, route_s = post(xs, os_, rs, y5s)
    route = jnp.concatenate([route_p, route_s], axis=1)

    eid = route[:TOP_K].T.astype(jnp.int32)
    wts = route[TOP_K:2 * TOP_K].T
    dest, n_sorted, block_exp, block_ord, block_next, block_next2, fill_blocks, n_used = _dispatch(eid)
    xs_rows = _dispatch_rows(dest, fill_blocks, hnp, hns, n_sorted, tm=256)
    out_rows = _moe(block_exp, block_ord, block_next, block_next2, n_used, xs_rows, w_gate[l], w_up[l], w_down[l])
    fin = functools.partial(_final, dest, wts[:, 0:1], wts[:, 1:2], g_final[None], out_rows, tm=256)
    y_prompt = fin(h1p, tok_off=0)
    y_sample = fin(h1s, tok_off=bp * tp)

    return (y_prompt.reshape(bp, tp, D_MODEL), y_sample.reshape(bs, ts, D_MODEL),
            gla_p[None], hp_re.reshape(1, bp, S5_GROUPS, S5_STATE), hp_im.reshape(1, bp, S5_GROUPS, S5_STATE),
            gla_s[None], hs_re.reshape(1, bs, S5_GROUPS, S5_STATE), hs_im.reshape(1, bs, S5_GROUPS, S5_STATE))
```

```python
import functools
import math

import jax
import jax.numpy as jnp
from jax import lax
from jax.experimental import pallas as pl
from jax.experimental.pallas import tpu as pltpu

F32 = jnp.float32
BF16 = jnp.bfloat16

D_MODEL = 2048
N_META = 16
GLA_HEADS = 4
GLA_DK = 128
GLA_DV = 256
GLA_KEY_WIDTH = GLA_HEADS * GLA_DK
GLA_WIDTH = GLA_HEADS * GLA_DV
GLA_RANK = 16
GLA_GATE_NORM = 16.0
GLA_CHUNK = 64
GLA_SUB = 16
GLA_SEQS_PER_STEP = 8
MASKED_EXPONENT = -1e30
S5_WIDTH = 1024
S5_GROUP = 16
S5_GROUPS = 64
S5_STATE = 64
S5_GB = 8
S5_NGB = S5_GROUPS // S5_GB
S5_UL = S5_GB * S5_GROUP
S5_SL = S5_GB * S5_STATE
S5_LT = 2 * S5_SL // 128
S5_UNROLL = 4
N_GROUPS = 4
EXPERTS_PER_GROUP = 8
N_EXPERTS = N_GROUPS * EXPERTS_PER_GROUP
EXPERT_HIDDEN = 512
TOP_K = 2
EPS = 1e-6
ROUTER_LANES = 128
ROUTE_ROWS = -(-(N_GROUPS + N_EXPERTS) // 8) * 8
MOE_TM = 256
MOE_WEIGHT_SLOTS = 3
VMEM_LIMIT = 56 * 1024 * 1024

_dot = functools.partial(jnp.dot, preferred_element_type=F32)


def _split2(x):
    hi = x.astype(BF16)
    lo = (x - hi.astype(F32)).astype(BF16)
    return hi, lo


def _rms(x, g):
    return x * lax.rsqrt(jnp.mean(x * x, axis=-1, keepdims=True) + EPS) * g


def _params(sem):
    return pltpu.CompilerParams(dimension_semantics=sem, vmem_limit_bytes=VMEM_LIMIT)


def _in_proj_kernel(x_ref, g_ref, w_ref, wgk_hi_ref, wgk_lo_ref, bgk_ref,
                    q_ref, k_ref, v_ref, r_ref, gk_ref, u_ref):
    xb = _rms(x_ref[...], g_ref[...]).astype(BF16)
    kw = GLA_KEY_WIDTH
    q_ref[...] = _dot(xb, w_ref[:, 0:kw]) * (GLA_DK ** -0.5)
    k_ref[...] = _dot(xb, w_ref[:, kw:2 * kw])
    v_ref[...] = _dot(xb, w_ref[:, 2 * kw:2 * kw + GLA_WIDTH])
    r_ref[...] = _dot(xb, w_ref[:, 2 * kw + GLA_WIDTH:2 * kw + 2 * GLA_WIDTH])
    c0 = 2 * kw + 2 * GLA_WIDTH
    tail = _dot(xb, w_ref[:, c0:])
    u_ref[...] = tail[:, GLA_RANK:GLA_RANK + S5_WIDTH]
    a_low = tail[:, :128]
    a_hi, a_lo = _split2(a_low)
    z = (_dot(a_hi, wgk_hi_ref[...]) + _dot(a_hi, wgk_lo_ref[...]) + _dot(a_lo, wgk_hi_ref[...])
         + bgk_ref[...])
    gk_ref[...] = (jnp.minimum(z, 0.0) - jnp.log1p(jnp.exp(-jnp.abs(z)))) * (1.0 / GLA_GATE_NORM)


def _in_proj(x, g_mix, w_cat, wgk_hi, wgk_lo, b_gk, tm):
    n = x.shape[0]
    wcols = w_cat.shape[1]
    row = lambda w: pl.BlockSpec((tm, w), lambda i: (i, 0))
    full = lambda a: pl.BlockSpec(a.shape, lambda i: (0,) * a.ndim)
    widths = [GLA_KEY_WIDTH, GLA_KEY_WIDTH, GLA_WIDTH, GLA_WIDTH, GLA_KEY_WIDTH, S5_WIDTH]
    return pl.pallas_call(
        _in_proj_kernel,
        grid=(n // tm,),
        in_specs=[row(D_MODEL), full(g_mix),
                  pl.BlockSpec((D_MODEL, wcols), lambda i: (0, 0), pipeline_mode=pl.Buffered(1)),
                  full(wgk_hi), full(wgk_lo), full(b_gk)],
        out_specs=[row(w) for w in widths],
        out_shape=[jax.ShapeDtypeStruct((n, w), F32) for w in widths],
        compiler_params=_params(("parallel",)),
        name="in_proj",
    )(x, g_mix, w_cat, wgk_hi, wgk_lo, b_gk)


def _gla_chunk(q_ref, k_ref, v_ref, g_ref, o_ref, r0, C, consts, read_state, write_state):
    tril, piece_rows, piece_lanes = consts
    sub = min(C, GLA_SUB)
    nsub = C // sub
    g = g_ref[pl.ds(r0, C), :]
    g1 = g.astype(BF16)
    rem = g - g1.astype(F32)
    g2 = rem.astype(BF16)
    g3 = (rem - g2.astype(F32)).astype(BF16)
    b_all = _dot(tril, g1) + _dot(tril, g2) + _dot(tril, g3)
    for h in range(GLA_HEADS):
        ks = slice(h * GLA_DK, (h + 1) * GLA_DK)
        vs = slice(h * GLA_DV, (h + 1) * GLA_DV)
        q = q_ref[pl.ds(r0, C), ks]
        k = k_ref[pl.ds(r0, C), ks]
        b = b_all[:, ks]
        s_prev = read_state(h)
        vb = v_ref[pl.ds(r0, C), vs].astype(BF16)
        o_inter = _dot((q * jnp.exp(b)).astype(BF16), s_prev.astype(BF16))
        blocks = []
        for s in range(nsub):
            lo = s * sub
            bs, qs, ksub = b[lo:lo + sub], q[lo:lo + sub], k[lo:lo + sub]
            acc = o_inter[lo:lo + sub]
            if s > 0:
                anchor = b[lo - 1:lo]
                qd = (qs * jnp.exp(bs - anchor)).astype(BF16)
                kd = (k[:lo] * jnp.exp(anchor - b[:lo])).astype(BF16)
                sc = lax.dot_general(qd, kd, (((1,), (1,)), ((), ())), preferred_element_type=F32)
                acc = acc + _dot(sc.astype(BF16), vb[:lo])
            pieces = [jnp.zeros((8, 128), F32) for _ in range(sub // 8)]
            for jj in range(sub):
                for p in range(jj // 8, sub // 8):
                    r8 = slice(8 * p, 8 * p + 8)
                    diff = bs[r8] - bs[jj:jj + 1]
                    if 8 * p < jj:
                        diff = jnp.where(piece_rows + 8 * p >= jj, diff, MASKED_EXPONENT)
                    col = jnp.sum((qs[r8] * ksub[jj:jj + 1]) * jnp.exp(diff), axis=-1, keepdims=True)
                    pieces[p] = jnp.where(piece_lanes == jj, col, pieces[p])
            scores = jnp.concatenate(pieces, axis=0) if len(pieces) > 1 else pieces[0]
            acc = acc + _dot(scores[:, :sub].astype(BF16), vb[lo:lo + sub])
            blocks.append(acc)
        o_ref[pl.ds(r0, C), vs] = jnp.concatenate(blocks, axis=0) if nsub > 1 else blocks[0]
        b_last = b[C - 1:C]
        kdec = (k * jnp.exp(b_last - b)).astype(BF16)
        upd = lax.dot_general(kdec, vb, (((0,), (0,)), ((), ())), preferred_element_type=F32)
        dcol = jnp.broadcast_to(jnp.exp(b_last), (GLA_DK, GLA_DK)).T
        write_state(h, s_prev * jnp.concatenate([dcol, dcol], axis=1) + upd)


def _gla_consts(C):
    sub = min(C, GLA_SUB)
    ri = lax.broadcasted_iota(jnp.int32, (C, C), 0)
    ci = lax.broadcasted_iota(jnp.int32, (C, C), 1)
    tril = jnp.where(ri >= ci, 1.0, 0.0).astype(BF16)
    piece_rows = lax.broadcasted_iota(jnp.int32, (8, GLA_DK), 0)
    piece_lanes = lax.broadcasted_iota(jnp.int32, (8, 128), 1)
    return tril, piece_rows, piece_lanes


def _gla_long_kernel(q_ref, k_ref, v_ref, g_ref, s0_ref, o_ref, sout_ref, s_scr, *, chunk, n_inner):
    j = pl.program_id(1)
    consts = _gla_consts(chunk)

    @pl.when(j == 0)
    def _init():
        s_scr[...] = s0_ref[0]

    def write_state(h, s):
        s_scr[h] = s

    def chunk_body(c, carry):
        _gla_chunk(q_ref, k_ref, v_ref, g_ref, o_ref, pl.multiple_of(c * chunk, chunk), chunk, consts,
                   lambda h: s_scr[h], write_state)
        return carry

    lax.fori_loop(0, n_inner, chunk_body, 0)

    @pl.when(j == pl.num_programs(1) - 1)
    def _fin():
        sout_ref[0] = s_scr[...]


def _gla_short_kernel(q_ref, k_ref, v_ref, g_ref, s0_ref, o_ref, sout_ref, *, chunk, nb):
    consts = _gla_consts(chunk)

    def seq_body(n, carry):
        def write_state(h, s):
            sout_ref[n, h] = s

        _gla_chunk(q_ref, k_ref, v_ref, g_ref, o_ref, pl.multiple_of(n * chunk, chunk), chunk, consts,
                   lambda h: s0_ref[n, h], write_state)
        return carry

    lax.fori_loop(0, nb, seq_body, 0, unroll=2 if nb % 2 == 0 else 1)


def _gla(q, k, v, g, s0, *, seq_len, chunk):
    nseq = s0.shape[0]
    n = q.shape[0]
    out_shape = [jax.ShapeDtypeStruct((n, GLA_WIDTH), F32),
                 jax.ShapeDtypeStruct((nseq, GLA_HEADS, GLA_DK, GLA_DV), F32)]
    if seq_len == chunk:
        nb = min(nseq, GLA_SEQS_PER_STEP)
        rows = lambda w: pl.BlockSpec((nb * chunk, w), lambda s: (s, 0))
        st = pl.BlockSpec((nb, GLA_HEADS, GLA_DK, GLA_DV), lambda s: (s, 0, 0, 0))
        return pl.pallas_call(
            functools.partial(_gla_short_kernel, chunk=chunk, nb=nb),
            grid=(nseq // nb,),
            in_specs=[rows(GLA_KEY_WIDTH), rows(GLA_KEY_WIDTH), rows(GLA_WIDTH), rows(GLA_KEY_WIDTH), st],
            out_specs=[rows(GLA_WIDTH), st],
            out_shape=out_shape,
            compiler_params=_params(("parallel",)),
            name="gla_short",
        )(q, k, v, g, s0)
    rb = min(seq_len, 4 * chunk)
    nblk = seq_len // rb
    rows = lambda w: pl.BlockSpec((rb, w), lambda s, j: (s * nblk + j, 0))
    st = pl.BlockSpec((1, GLA_HEADS, GLA_DK, GLA_DV), lambda s, j: (s, 0, 0, 0))
    return pl.pallas_call(
        functools.partial(_gla_long_kernel, chunk=chunk, n_inner=rb // chunk),
        grid=(nseq, nblk),
        in_specs=[rows(GLA_KEY_WIDTH), rows(GLA_KEY_WIDTH), rows(GLA_WIDTH), rows(GLA_KEY_WIDTH), st],
        out_specs=[rows(GLA_WIDTH), st],
        out_shape=out_shape,
        scratch_shapes=[pltpu.VMEM((GLA_HEADS, GLA_DK, GLA_DV), F32)],
        compiler_params=_params(("parallel", "arbitrary")),
        name="gla_long",
    )(q, k, v, g, s0)


def _cmul(ar, ai, br, bi):
    return ar * br - ai * bi, ar * bi + ai * br


def _cpow(ar, ai, n):
    res = None
    while n:
        if n & 1:
            res = (ar, ai) if res is None else _cmul(res[0], res[1], ar, ai)
        n >>= 1
        if n:
            ar, ai = _cmul(ar, ai, ar, ai)
    return res


def _s5_segment_len(seq_len):
    s = -(-seq_len // 8)
    s = -(-s // 4) * 4
    return s if (s // 4) % 2 == 1 else s + 4


def _st_store(st_scr, rows, val):
    for t in range(S5_LT):
        st_scr[t, rows, :] = val[:, t * 128:(t + 1) * 128]


def _st_load(st_scr, rows):
    return jnp.concatenate([st_scr[t, rows, :] for t in range(S5_LT)], axis=1)


def _s5_long_kernel(u_ref, bb_ref, cc_ref, dsk_ref, are_ref, aim_ref, h0re_ref, h0im_ref,
                    y_ref, hre_ref, him_ref, st_scr, *, seq_len, seg):
    T, S, SL = seq_len, seg, S5_SL
    rc = min(T, 512)
    for c in range(T // rc):
        sl = slice(c * rc, (c + 1) * rc)
        _st_store(st_scr, sl, _dot(u_ref[sl, :].astype(BF16), bb_ref[0]))
    if 8 * S > T:
        _st_store(st_scr, slice(T, 8 * S), jnp.zeros((8 * S - T, 2 * SL), F32))
    a_re, a_im = are_ref[0], aim_ref[0]
    ar = jnp.broadcast_to(a_re, (8, SL))
    ai = jnp.broadcast_to(a_im, (8, SL))

    def step(i, h, store):
        strand = pl.ds(i, 8, stride=S)
        x = _st_load(st_scr, strand)
        mr, mi = _cmul(ar, ai, h[0], h[1])
        nr, ni = mr + x[:, :SL], mi + x[:, SL:]
        if store:
            _st_store(st_scr, strand, jnp.concatenate([nr, ni], axis=1))
        return nr, ni

    zero = jnp.zeros((8, SL), F32)
    fr, fi = lax.fori_loop(0, S, functools.partial(step, store=False), (zero, zero), unroll=S5_UNROLL)

    car_r, car_i = _s5_carries(fr, fi, a_re, a_im, S, h0re_ref[0, 0], h0im_ref[0, 0])

    lax.fori_loop(0, S, functools.partial(step, store=True), (car_r, car_i), unroll=S5_UNROLL)

    last = _st_load(st_scr, slice(T - 1, T))
    hre_ref[0, 0] = last[:, :SL]
    him_ref[0, 0] = last[:, SL:]
    for c in range(T // rc):
        sl = slice(c * rc, (c + 1) * rc)
        y_ref[sl, :] = _dot(_st_load(st_scr, sl).astype(BF16), cc_ref[0]) + dsk_ref[0] * u_ref[sl, :]


def _s5_long(u, mats, h0_re, h0_im, *, seq_len):
    bb, cc, dsk, a_re, a_im = mats
    nseq = h0_re.shape[0]
    seg = _s5_segment_len(seq_len)
    gb3 = lambda shape: pl.BlockSpec((1,) + shape, lambda s, g: (g, 0, 0))
    st = pl.BlockSpec((1, 1, 1, S5_SL), lambda s, g: (s, g, 0, 0))
    urow = pl.BlockSpec((seq_len, S5_UL), lambda s, g: (s, g))
    st_shape = jax.ShapeDtypeStruct((nseq, S5_NGB, 1, S5_SL), F32)
    return pl.pallas_call(
        functools.partial(_s5_long_kernel, seq_len=seq_len, seg=seg),
        grid=(nseq, S5_NGB),
        in_specs=[urow, gb3((S5_UL, 2 * S5_SL)), gb3((2 * S5_SL, S5_UL)), gb3((1, S5_UL)),
                  gb3((1, S5_SL)), gb3((1, S5_SL)), st, st],
        out_specs=[urow, st, st],
        out_shape=[jax.ShapeDtypeStruct(u.shape, F32), st_shape, st_shape],
        scratch_shapes=[pltpu.VMEM((S5_LT, 8 * seg, 128), F32)],
        compiler_params=_params(("parallel", "parallel")),
        name="s5_long",
    )(u, bb, cc, dsk, a_re, a_im, h0_re, h0_im)


def _s5_pipe_step(b_buf, s_buf, c_buf, ub_ref, uc_ref, bb_ref, cc_ref, dsk_ref, are_ref, aim_ref,
                  h0re_ref, h0im_ref, y_ref, hre_ref, him_ref, *, seq_len, seg):
    T, S, SL = seq_len, seg, S5_SL
    if 8 * S > T:
        _st_store(b_buf, slice(T, 8 * S), jnp.zeros((8 * S - T, 2 * SL), F32))
    a_re, a_im = are_ref[0], aim_ref[0]
    ar = jnp.broadcast_to(a_re, (8, SL))
    ai = jnp.broadcast_to(a_im, (8, SL))

    def scan(h, store):
        for i in range(S):
            strand = pl.ds(i, 8, stride=S)
            x = _st_load(s_buf, strand)
            mr, mi = _cmul(ar, ai, h[0], h[1])
            h = (mr + x[:, :SL], mi + x[:, SL:])
            if store:
                _st_store(s_buf, strand, jnp.concatenate(h, axis=1))
        return h

    _st_store(b_buf, slice(0, T), _dot(ub_ref[...].astype(BF16), bb_ref[0]))
    zero = jnp.zeros((8, SL), F32)
    fr, fi = scan((zero, zero), False)
    car_r, car_i = _s5_carries(fr, fi, a_re, a_im, S, h0re_ref[0, 0], h0im_ref[0, 0])
    y_ref[...] = _dot(_st_load(c_buf, slice(0, T)).astype(BF16), cc_ref[0]) + dsk_ref[0] * uc_ref[...]
    scan((car_r, car_i), True)
    last = _st_load(s_buf, slice(T - 1, T))
    hre_ref[0, 0] = last[:, :SL]
    him_ref[0, 0] = last[:, SL:]


def _s5_carries(fr, fi, a_re, a_im, seg, h0_re, h0_im):
    as_re, as_im = _cpow(a_re, a_im, seg)
    rows = lax.broadcasted_iota(jnp.int32, fr.shape, 0)
    cr, ci = h0_re, h0_im
    car_r, car_i = jnp.zeros_like(fr), jnp.zeros_like(fi)
    for r in range(8):
        car_r = jnp.where(rows == r, cr, car_r)
        car_i = jnp.where(rows == r, ci, car_i)
        if r < 7:
            mr, mi = _cmul(as_re, as_im, cr, ci)
            cr, ci = mr + fr[r:r + 1], mi + fi[r:r + 1]
    return car_r, car_i


def _s5_pipe_kernel(*refs, seq_len, seg):
    bufs, refs = refs[-3:], refs[:-3]
    t = pl.program_id(0)

    @pl.when(t == 0)
    def _init():
        for buf in bufs:
            buf[...] = jnp.zeros(buf.shape, buf.dtype)

    for r in range(3):
        @pl.when(t % 3 == r)
        def _rotation(r=r):
            _s5_pipe_step(bufs[r], bufs[(r + 2) % 3], bufs[(r + 1) % 3], *refs, seq_len=seq_len, seg=seg)


def _s5_pipelined(u, mats, h0_re, h0_im, *, seq_len):
    bb, cc, dsk, a_re, a_im = mats
    nseq = h0_re.shape[0]
    n_items = nseq * S5_NGB
    seg = _s5_segment_len(seq_len)

    def item(lag):
        return lambda t: jnp.clip(t - lag, 0, n_items - 1)

    def per_gb(shape, lag):
        return pl.BlockSpec((1,) + shape, lambda t: (item(lag)(t) % S5_NGB, 0, 0))

    def rows(lag):
        return pl.BlockSpec((seq_len, S5_UL), lambda t: (item(lag)(t) // S5_NGB, item(lag)(t) % S5_NGB))

    st = pl.BlockSpec((1, 1, 1, S5_SL), lambda t: (item(1)(t) // S5_NGB, item(1)(t) % S5_NGB, 0, 0))
    st_shape = jax.ShapeDtypeStruct((nseq, S5_NGB, 1, S5_SL), F32)
    buf = pltpu.VMEM((S5_LT, 8 * seg, 128), F32)
    return pl.pallas_call(
        functools.partial(_s5_pipe_kernel, seq_len=seq_len, seg=seg),
        grid=(n_items + 2,),
        in_specs=[rows(0), rows(2), per_gb((S5_UL, 2 * S5_SL), 0), per_gb((2 * S5_SL, S5_UL), 2),
                  per_gb((1, S5_UL), 2), per_gb((1, S5_SL), 1), per_gb((1, S5_SL), 1), st, st],
        out_specs=[rows(2), st, st],
        out_shape=[jax.ShapeDtypeStruct(u.shape, F32), st_shape, st_shape],
        scratch_shapes=[buf, buf, buf],
        compiler_params=_params(("arbitrary",)),
        name="s5_pipelined",
    )(u, u, bb, cc, dsk, a_re, a_im, h0_re, h0_im)


def _s5_short_kernel(u_ref, bb_ref, cc_ref, dsk_ref, are_ref, aim_ref, h0re_ref, h0im_ref,
                     y_ref, hre_ref, him_ref, st_scr, *, seq_len, nseq):
    SL = S5_SL
    _st_store(st_scr, slice(None), _dot(u_ref[...].astype(BF16), bb_ref[0]))
    ar = jnp.broadcast_to(are_ref[0], (nseq, SL))
    ai = jnp.broadcast_to(aim_ref[0], (nseq, SL))
    hr, hi = h0re_ref[...], h0im_ref[...]
    for t in range(seq_len):
        step = pl.ds(t, nseq, stride=seq_len)
        x = _st_load(st_scr, step)
        mr, mi = _cmul(ar, ai, hr, hi)
        hr, hi = mr + x[:, :SL], mi + x[:, SL:]
        _st_store(st_scr, step, jnp.concatenate([hr, hi], axis=1))
    hre_ref[...] = hr
    him_ref[...] = hi
    y_ref[...] = _dot(_st_load(st_scr, slice(None)).astype(BF16), cc_ref[0]) + dsk_ref[0] * u_ref[...]


def _s5_short(u, mats, h0_re, h0_im, *, seq_len):
    bb, cc, dsk, a_re, a_im = mats
    nseq = h0_re.shape[0]
    n = nseq * seq_len
    gb3 = lambda shape: pl.BlockSpec((1,) + shape, lambda g: (g, 0, 0))
    st = pl.BlockSpec((nseq, S5_SL), lambda g: (0, g))
    urow = pl.BlockSpec((n, S5_UL), lambda g: (0, g))
    st_shape = jax.ShapeDtypeStruct(h0_re.shape, F32)
    return pl.pallas_call(
        functools.partial(_s5_short_kernel, seq_len=seq_len, nseq=nseq),
        grid=(S5_NGB,),
        in_specs=[urow, gb3((S5_UL, 2 * S5_SL)), gb3((2 * S5_SL, S5_UL)), gb3((1, S5_UL)),
                  gb3((1, S5_SL)), gb3((1, S5_SL)), st, st],
        out_specs=[urow, st, st],
        out_shape=[jax.ShapeDtypeStruct(u.shape, F32), st_shape, st_shape],
        scratch_shapes=[pltpu.VMEM((S5_LT, n, 128), F32)],
        compiler_params=_params(("parallel",)),
        name="s5_short",
    )(u, bb, cc, dsk, a_re, a_im, h0_re, h0_im)


def _s5_matrices(lam_re, lam_im, log_dt, b_re, b_im, c_re, c_im, d_skip):
    dt = jnp.exp(log_dt)[:, None]
    mag = jnp.exp(lam_re * dt)
    ab_re, ab_im = mag * jnp.cos(lam_im * dt), mag * jnp.sin(lam_im * dt)
    den = lam_re * lam_re + lam_im * lam_im
    f_re = ((ab_re - 1.0) * lam_re + ab_im * lam_im) / den
    f_im = (ab_im * lam_re - (ab_re - 1.0) * lam_im) / den
    bb_re = f_re[..., None] * b_re - f_im[..., None] * b_im
    bb_im = f_re[..., None] * b_im + f_im[..., None] * b_re
    eye = jnp.eye(S5_GB, dtype=F32)

    def in_mat(m):
        m = m.reshape(S5_NGB, S5_GB, S5_STATE, S5_GROUP)
        return jnp.einsum('bgpc,gh->bgchp', m, eye).reshape(S5_NGB, S5_UL, S5_SL)

    def out_mat(m):
        m = m.reshape(S5_NGB, S5_GB, S5_GROUP, S5_STATE)
        return jnp.einsum('bgcp,gh->bgphc', m, eye).reshape(S5_NGB, S5_SL, S5_UL)

    bb = jnp.concatenate([in_mat(bb_re), in_mat(bb_im)], axis=2).astype(BF16)
    cc = jnp.concatenate([out_mat(c_re), out_mat(-c_im)], axis=1).astype(BF16)
    dsk = d_skip.reshape(S5_NGB, 1, S5_UL)
    return bb, cc, dsk, ab_re.reshape(S5_NGB, 1, S5_SL), ab_im.reshape(S5_NGB, 1, S5_SL)


def _post_mix_tile(x_ref, o_ref, r_ref, y5_ref, ggla_ref, wglu_ref, bglu_ref, wout_ref, gffn_ref,
                   wr_ref, br_ref, h1_ref, hn_ref, lg_ref):
    o = o_ref[...]
    parts = []
    for h in range(GLA_HEADS):
        oh = o[:, h * GLA_DV:(h + 1) * GLA_DV]
        parts.append(oh * lax.rsqrt(jnp.mean(oh * oh, axis=-1, keepdims=True) + EPS))
    r = r_ref[...]
    o_gla = (jnp.concatenate(parts, axis=1) * ggla_ref[...]) * (r * jax.nn.sigmoid(r))
    y5 = y5_ref[...]
    z = y5 * (0.5 * (1.0 + jnp.tanh(math.sqrt(2.0 / math.pi) * (y5 + 0.044715 * (y5 * y5 * y5)))))
    o_s5 = z * jax.nn.sigmoid(_dot(z.astype(BF16), wglu_ref[...]) + bglu_ref[...])
    att = (_dot(o_gla.astype(BF16), wout_ref[0:GLA_WIDTH, :])
           + _dot(o_s5.astype(BF16), wout_ref[GLA_WIDTH:GLA_WIDTH + S5_WIDTH, :]))
    h1 = x_ref[...] + att
    h1_ref[...] = h1
    hn = _rms(h1, gffn_ref[...])
    hn_ref[...] = _pack_bf16_pairs(hn)
    hn_hi, hn_lo = _split2(hn)
    both = _dot(hn_hi, wr_ref[...])
    logits = (both[:, :ROUTER_LANES] + both[:, ROUTER_LANES:] + _dot(hn_lo, wr_ref[:, :ROUTER_LANES])
              + br_ref[...])
    lg_ref[...] = _route_tile(logits)


def _route_tile(logits):
    lt = logits.T[:ROUTE_ROWS]
    row = lax.broadcasted_iota(jnp.int32, lt.shape, 0)
    ninf = float('-inf')

    def first_max(vals):
        m = jnp.max(vals, axis=0, keepdims=True)
        return m, jnp.min(jnp.where(vals == m, row, ROUTE_ROWS), axis=0, keepdims=True)

    is_group = row < N_GROUPS
    gmax, gsel = first_max(jnp.where(is_group, lt, ninf))
    p_group = 1.0 / jnp.sum(jnp.where(is_group, jnp.exp(lt - gmax), 0.0), axis=0, keepdims=True)
    first = N_GROUPS + gsel * EXPERTS_PER_GROUP
    in_group = jnp.logical_and(row >= first, row < first + EXPERTS_PER_GROUP)
    cand = jnp.where(in_group, lt, ninf)
    m1, i1 = first_max(cand)
    m2, i2 = first_max(jnp.where(row == i1, ninf, cand))
    t = jnp.exp(m2 - m1)
    p1 = 1.0 / (1.0 + t)
    out_row = lax.broadcasted_iota(jnp.int32, (8, lt.shape[1]), 0)
    out = jnp.where(out_row == 0, (i1 - N_GROUPS).astype(F32), 0.0)
    out = jnp.where(out_row == 1, (i2 - N_GROUPS).astype(F32), out)
    out = jnp.where(out_row == 2, p_group * p1, out)
    return jnp.where(out_row == 3, p_group * (t * p1), out)


def _post_mix(x, o, r, y5, g_gla, w_glu, b_glu, w_out, g_ffn, wr_pair, b_r, tm):
    n = x.shape[0]
    row = lambda w: pl.BlockSpec((tm, w), lambda i: (i, 0))
    full = lambda a: pl.BlockSpec(a.shape, lambda i: (0,) * a.ndim, pipeline_mode=pl.Buffered(1))
    weights = [g_gla, w_glu, b_glu, w_out, g_ffn, wr_pair, b_r]
    return pl.pallas_call(
        _post_mix_tile,
        grid=(n // tm,),
        in_specs=[row(D_MODEL), row(GLA_WIDTH), row(GLA_WIDTH), row(S5_WIDTH)] + [full(a) for a in weights],
        out_specs=[row(D_MODEL), row(D_MODEL // 2), pl.BlockSpec((8, tm), lambda i: (0, i))],
        out_shape=[jax.ShapeDtypeStruct((n, D_MODEL), F32), jax.ShapeDtypeStruct((n, D_MODEL // 2), jnp.uint32),
                   jax.ShapeDtypeStruct((8, n), F32)],
        compiler_params=_params(("parallel",)),
        name="post_mix",
    )(x, o, r, y5, *weights)


def _gather_start(idx_ref, idx_base, idx_stride, src_hbm, dst, sem, n):
    for r in range(n):
        row = idx_ref[idx_base + r * idx_stride]
        pltpu.make_async_copy(src_hbm.at[pl.ds(row, 1)], dst.at[pl.ds(r, 1)], sem).start(priority=r % 2)


def _pack_bf16_pairs(x):
    half = x.shape[1] // 2
    bits = lax.bitcast_convert_type(x.astype(BF16).astype(F32), jnp.uint32)
    return bits[:, half:] | (bits[:, :half] >> 16)


def _unpack_bf16_pairs(p):
    lo = lax.bitcast_convert_type(p << 16, F32).astype(BF16)
    hi = lax.bitcast_convert_type(p & jnp.uint32(0xFFFF0000), F32).astype(BF16)
    return jnp.concatenate([lo, hi], axis=1)


def _gather_wait(src_hbm, dst, sem, n):
    pltpu.make_async_copy(src_hbm.at[pl.ds(0, n)], dst.at[pl.ds(0, n)], sem).wait()


def _dispatch_rows_kernel(dest_ref, fill_ref, xa_ref, xb_ref, xs_hbm, buf, sem, zsem, *, tm, n_first):
    i = pl.program_id(0)
    slot = i % 2

    def row_copy(r, k, s):
        row = dest_ref[(i * tm + r) * TOP_K + k]
        return pltpu.make_async_copy(buf.at[s, pl.ds(r, 1)], xs_hbm.at[pl.ds(row, 1)], sem.at[s])

    def wait_slot(s):
        for _ in range(TOP_K):
            pltpu.make_async_copy(buf.at[s], xs_hbm.at[pl.ds(0, tm)], sem.at[s]).wait()

    @pl.when(i == 0)
    def _zero_fill():
        buf[1] = jnp.zeros(buf.shape[1:], buf.dtype)

        def fill(n, carry):
            @pl.when(fill_ref[n] >= 0)
            def _():
                pltpu.make_async_copy(buf.at[1, pl.ds(0, MOE_TM)], xs_hbm.at[pl.ds(fill_ref[n] * MOE_TM, MOE_TM)],
                                      zsem).start()
            return carry
        lax.fori_loop(0, 2 * N_EXPERTS, fill, 0)

        def drain(n, carry):
            @pl.when(fill_ref[n] >= 0)
            def _():
                pltpu.make_async_copy(buf.at[1, pl.ds(0, MOE_TM)], xs_hbm.at[pl.ds(0, MOE_TM)], zsem).wait()
            return carry
        lax.fori_loop(0, 2 * N_EXPERTS, drain, 0)

    @pl.when(i >= 2)
    def _reuse():
        wait_slot(slot)

    @pl.when(i < n_first)
    def _from_a():
        buf[slot] = xa_ref[...]

    @pl.when(i >= n_first)
    def _from_b():
        buf[slot] = xb_ref[...]

    for r in range(tm):
        for k in range(TOP_K):
            row_copy(r, k, slot).start(priority=k)

    @pl.when(i == pl.num_programs(0) - 1)
    def _finish():
        wait_slot(slot)

        @pl.when(i >= 1)
        def _():
            wait_slot(1 - slot)


def _dispatch_rows(dest, fill_blocks, hn_a, hn_b, n_rows, tm):
    n_first = hn_a.shape[0] // tm
    n = hn_a.shape[0] + hn_b.shape[0]
    assert tm >= MOE_TM
    return pl.pallas_call(
        functools.partial(_dispatch_rows_kernel, tm=tm, n_first=n_first),
        grid_spec=pltpu.PrefetchScalarGridSpec(
            num_scalar_prefetch=2,
            grid=(n // tm,),
            in_specs=[pl.BlockSpec((tm, D_MODEL // 2), lambda i, d, lb: (jnp.minimum(i, n_first - 1), 0)),
                      pl.BlockSpec((tm, D_MODEL // 2), lambda i, d, lb: (jnp.maximum(i - n_first, 0), 0))],
            out_specs=pl.BlockSpec(memory_space=pl.ANY),
            scratch_shapes=[pltpu.VMEM((2, tm, D_MODEL // 2), jnp.uint32), pltpu.SemaphoreType.DMA((2,)),
                            pltpu.SemaphoreType.DMA],
        ),
        out_shape=jax.ShapeDtypeStruct((n_rows, D_MODEL // 2), jnp.uint32),
        compiler_params=_params(("arbitrary",)),
        name="dispatch_rows",
    )(dest, fill_blocks, hn_a, hn_b)


def _moe_kernel(bexp_ref, eord_ref, next_ref, next2_ref, nused_ref, x_ref, wg_hbm, wu_hbm, wd_hbm, o_ref,
                wg_st, wu_st, wd_st, wsem, wg_bf, wu_bf, wd_bf):
    b = pl.program_id(0)
    nu = nused_ref[0]
    e = bexp_ref[b]
    new_expert = jnp.logical_or(b == 0, bexp_ref[jnp.maximum(b - 1, 0)] != e)

    def weight_copies(expert, slot):
        return [pltpu.make_async_copy(src.at[expert], dst.at[slot], wsem.at[slot])
                for src, dst in ((wg_hbm, wg_st), (wu_hbm, wu_st), (wd_hbm, wd_st))]

    def prefetch(expert, slot):
        @pl.when(expert >= 0)
        def _():
            for c in weight_copies(expert, slot):
                c.start()

    @pl.when(b == 0)
    def _prologue():
        for c in weight_copies(e, 0):
            c.start()
        prefetch(next_ref[0], 1)

    @pl.when(jnp.logical_and(b < nu, new_expert))
    def _new_expert():
        ordinal = eord_ref[b]
        slot = ordinal % MOE_WEIGHT_SLOTS
        for c in weight_copies(e, slot):
            c.wait()
        prefetch(next2_ref[b], (ordinal + 2) % MOE_WEIGHT_SLOTS)

        wg_bf[...] = wg_st[slot].astype(BF16)
        wu_bf[...] = wu_st[slot].astype(BF16)
        wd_bf[...] = wd_st[slot].astype(BF16)

    @pl.when(b < nu)
    def _run():
        x = _unpack_bf16_pairs(x_ref[...])
        gate = _dot(x, wg_bf[...])
        up = _dot(x, wu_bf[...])
        hid = ((gate * jax.nn.sigmoid(gate)) * up).astype(BF16)
        o_ref[...] = _pack_bf16_pairs(_dot(hid, wd_bf[...]))

    @pl.when(b >= nu)
    def _skip():
        o_ref[...] = jnp.zeros(o_ref.shape, o_ref.dtype)


def _moe(block_exp, block_ord, block_next, block_next2, n_used, xs, w_gate, w_up, w_down):
    nblk = xs.shape[0] // MOE_TM
    any_spec = pl.BlockSpec(memory_space=pl.ANY)
    return pl.pallas_call(
        _moe_kernel,
        grid_spec=pltpu.PrefetchScalarGridSpec(
            num_scalar_prefetch=5,
            grid=(nblk,),
            in_specs=[pl.BlockSpec((MOE_TM, D_MODEL // 2),
                                   lambda b, be, eo, nx, nx2, nu: (jnp.minimum(b, nu[0] - 1), 0)),
                      any_spec, any_spec, any_spec],
            out_specs=pl.BlockSpec((MOE_TM, D_MODEL // 2), lambda b, *_: (b, 0)),
            scratch_shapes=[pltpu.VMEM((MOE_WEIGHT_SLOTS, D_MODEL, EXPERT_HIDDEN), F32),
                            pltpu.VMEM((MOE_WEIGHT_SLOTS, D_MODEL, EXPERT_HIDDEN), F32),
                            pltpu.VMEM((MOE_WEIGHT_SLOTS, EXPERT_HIDDEN, D_MODEL), F32),
                            pltpu.SemaphoreType.DMA((MOE_WEIGHT_SLOTS,)),
                            pltpu.VMEM((D_MODEL, EXPERT_HIDDEN), BF16), pltpu.VMEM((D_MODEL, EXPERT_HIDDEN), BF16),
                            pltpu.VMEM((EXPERT_HIDDEN, D_MODEL), BF16)],
        ),
        out_shape=jax.ShapeDtypeStruct((nblk * MOE_TM, D_MODEL // 2), jnp.uint32),
        compiler_params=_params(("arbitrary",)),
        name="moe",
    )(block_exp, block_ord, block_next, block_next2, n_used, xs, w_gate, w_up, w_down)


def _final_kernel(dest_ref, h1_ref, wa_ref, wb_ref, g_ref, rows_hbm, y_ref, ybuf, sem, *, tm, tok_off):
    i = pl.program_id(0)

    def start(tile):
        slot = tile % 2
        base = (tok_off + tile * tm) * TOP_K
        for k in range(TOP_K):
            _gather_start(dest_ref, base + k, TOP_K, rows_hbm, ybuf.at[slot, k], sem.at[slot], tm)

    @pl.when(i == 0)
    def _first():
        start(i)

    @pl.when(i + 1 < pl.num_programs(0))
    def _next():
        start(i + 1)

    slot = i % 2
    for k in range(TOP_K):
        _gather_wait(rows_hbm, ybuf.at[slot, k], sem.at[slot], tm)
    ya = _unpack_bf16_pairs(ybuf[slot, 0]).astype(F32)
    yb = _unpack_bf16_pairs(ybuf[slot, 1]).astype(F32)
    moe = ya * wa_ref[...] + yb * wb_ref[...]
    y_ref[...] = _rms(h1_ref[...] + moe, g_ref[...])


def _final(dest, wa, wb, g_final, rows, h1, *, tm, tok_off):
    n = h1.shape[0]
    off = tok_off // tm
    row = lambda w: pl.BlockSpec((tm, w), lambda i, d: (i + off, 0))
    return pl.pallas_call(
        functools.partial(_final_kernel, tm=tm, tok_off=tok_off),
        grid_spec=pltpu.PrefetchScalarGridSpec(
            num_scalar_prefetch=1,
            grid=(n // tm,),
            in_specs=[pl.BlockSpec((tm, D_MODEL), lambda i, d: (i, 0)), row(1), row(1),
                      pl.BlockSpec((1, D_MODEL), lambda i, d: (0, 0)),
                      pl.BlockSpec(memory_space=pl.ANY)],
            out_specs=pl.BlockSpec((tm, D_MODEL), lambda i, d: (i, 0)),
            scratch_shapes=[pltpu.VMEM((2, TOP_K, tm, D_MODEL // 2), jnp.uint32), pltpu.SemaphoreType.DMA((2,))],
        ),
        out_shape=jax.ShapeDtypeStruct((n, D_MODEL), F32),
        compiler_params=_params(("arbitrary",)),
        name="final",
    )(dest, h1, wa, wb, g_final, rows)


def _dispatch(eid):
    t = eid.shape[0]
    a = t * TOP_K
    assert a % MOE_TM == 0
    nblk = a // MOE_TM + N_EXPERTS
    flat = eid.reshape(-1)
    experts = jnp.arange(N_EXPERTS, dtype=jnp.int32)
    onehot = (flat[:, None] == experts[None, :]).astype(F32).reshape(a // MOE_TM, MOE_TM, N_EXPERTS)
    strict_lower = jnp.tril(jnp.ones((MOE_TM, MOE_TM), F32), -1)
    within = jnp.einsum('ij,bjk->bik', strict_lower, onehot)
    totals = jnp.sum(onehot, axis=1)
    before = jnp.cumsum(totals, axis=0) - totals
    counts = jnp.sum(totals, axis=0).astype(jnp.int32)
    padded = (counts + MOE_TM - 1) // MOE_TM * MOE_TM
    pends = jnp.cumsum(padded)
    offset = before + (pends - padded).astype(F32)[None, :]
    dest = jnp.sum((within + offset[:, None, :]) * onehot, axis=-1).reshape(-1)
    n_used = (pends[-1] // MOE_TM).astype(jnp.int32)
    blk = jnp.minimum(jnp.arange(nblk, dtype=jnp.int32), n_used - 1)
    block_exp = jnp.minimum(jnp.searchsorted(pends, blk * MOE_TM, side='right'), N_EXPERTS - 1).astype(jnp.int32)
    in_use = counts > 0
    ordinal = jnp.cumsum(in_use.astype(jnp.int32)) - 1
    later = lax.cummin(jnp.where(in_use, experts, N_EXPERTS), axis=0, reverse=True)
    nxt = jnp.concatenate([later[1:], jnp.full((1,), N_EXPERTS, jnp.int32)])
    nxt = jnp.where(nxt < N_EXPERTS, nxt, -1)
    nxt2 = jnp.where(nxt >= 0, nxt[jnp.maximum(nxt, 0)], -1)
    last_block = jnp.where(in_use, pends // MOE_TM - 1, -1)
    unused = n_used + experts
    fill_blocks = jnp.concatenate([last_block, jnp.where(unused < nblk, unused, -1)]).astype(jnp.int32)
    return (dest.astype(jnp.int32), nblk * MOE_TM, block_exp, ordinal[block_exp], nxt[block_exp], nxt2[block_exp],
            fill_blocks, n_used.reshape(1))


def kernel(x_prompt, x_sample, state_gla, state_s5_re, state_s5_im, meta, g_mix, w_in, w_gk2, b_gk, g_gla,
           lam_re, lam_im, log_dt, s5_b_re, s5_b_im, s5_c_re, s5_c_im, d_skip, w_glu, b_glu, w_out, g_ffn,
           w_rg, b_rg, w_re, b_re, w_gate, w_up, w_down, g_final):
    bp, tp, _ = x_prompt.shape
    bs, ts, _ = x_sample.shape
    l = 0

    w_cat = w_in[l].astype(BF16)
    wgk_hi, wgk_lo = _split2(jnp.pad(w_gk2[l], ((0, 128 - GLA_RANK), (0, 0))))
    g_mix2, b_gk2 = g_mix[l][None], b_gk[l][None]
    mats = _s5_matrices(lam_re[l], lam_im[l], log_dt[l], s5_b_re[l], s5_b_im[l], s5_c_re[l], s5_c_im[l],
                        d_skip[l])
    w_router = jnp.concatenate([w_rg[l], jnp.moveaxis(w_re[l], 0, 1).reshape(D_MODEL, N_EXPERTS)], axis=1)
    w_router = jnp.pad(w_router, ((0, 0), (0, ROUTER_LANES - N_GROUPS - N_EXPERTS)))
    wr_pair = jnp.concatenate(_split2(w_router), axis=1)
    b_router = jnp.pad(jnp.concatenate([b_rg[l], b_re[l].reshape(-1)]),
                       (0, ROUTER_LANES - N_GROUPS - N_EXPERTS))[None]
    w_glu_b, w_out_b = w_glu[l].astype(BF16), w_out[l].astype(BF16)
    g_gla2, b_glu2, g_ffn2 = g_gla[l].reshape(1, GLA_WIDTH), b_glu[l][None], g_ffn[l][None]

    proj = functools.partial(_in_proj, g_mix=g_mix2, w_cat=w_cat, wgk_hi=wgk_hi, wgk_lo=wgk_lo, b_gk=b_gk2)

    qm, km, vm, _, gm, um = proj(meta, tm=N_META)
    zero_s = jnp.zeros((1, GLA_HEADS, GLA_DK, GLA_DV), F32)
    _, s_meta = _gla(qm, km, vm, gm, zero_s, seq_len=N_META, chunk=N_META)
    zero_h = jnp.zeros((1, S5_NGB, 1, S5_SL), F32)
    _, hm_re, hm_im = _s5_long(um, mats, zero_h, zero_h, seq_len=N_META)

    xp = x_prompt.reshape(bp * tp, D_MODEL)
    qp, kp, vp, rp, gp, up = proj(xp, tm=512)
    op, gla_p = _gla(qp, kp, vp, gp, jnp.broadcast_to(s_meta, (bp,) + s_meta.shape[1:]),
                     seq_len=tp, chunk=GLA_CHUNK)
    y5p, hp_re, hp_im = _s5_pipelined(up, mats, jnp.broadcast_to(hm_re, (bp,) + hm_re.shape[1:]),
                                      jnp.broadcast_to(hm_im, (bp,) + hm_im.shape[1:]), seq_len=tp)

    xs = x_sample.reshape(bs * ts, D_MODEL)
    qs, ks, vs, rs, gs, us = proj(xs, tm=512)
    chunk_s = GLA_CHUNK if ts % GLA_CHUNK == 0 else ts
    os_, gla_s = _gla(qs, ks, vs, gs, state_gla[l], seq_len=ts, chunk=chunk_s)
    y5s, hs_re, hs_im = _s5_short(us, mats, state_s5_re[l].reshape(bs, -1), state_s5_im[l].reshape(bs, -1),
                                  seq_len=ts)

    post = functools.partial(_post_mix, g_gla=g_gla2, w_glu=w_glu_b, b_glu=b_glu2, w_out=w_out_b, g_ffn=g_ffn2,
                             wr_pair=wr_pair, b_r=b_router, tm=512)
    h1p, hnp, route_p = post(xp, op, rp, y5p)
    h1s, hns, route_s = post(xs, os_, rs, y5s)
    route = jnp.concatenate([route_p, route_s], axis=1)

    eid = route[:TOP_K].T.astype(jnp.int32)
    wts = route[TOP_K:2 * TOP_K].T
    dest, n_sorted, block_exp, block_ord, block_next, block_next2, fill_blocks, n_used = _dispatch(eid)
    xs_rows = _dispatch_rows(dest, fill_blocks, hnp, hns, n_sorted, tm=256)
    out_rows = _moe(block_exp, block_ord, block_next, block_next2, n_used, xs_rows, w_gate[l], w_up[l], w_down[l])
    fin = functools.partial(_final, dest, wts[:, 0:1], wts[:, 1:2], g_final[None], out_rows, tm=256)
    y_prompt = fin(h1p, tok_off=0)
    y_sample = fin(h1s, tok_off=bp * tp)

    return (y_prompt.reshape(bp, tp, D_MODEL), y_sample.reshape(bs, ts, D_MODEL),
            gla_p[None], hp_re.reshape(1, bp, S5_GROUPS, S5_STATE), hp_im.reshape(1, bp, S5_GROUPS, S5_STATE),
            gla_s[None], hs_re.reshape(1, bs, S5_GROUPS, S5_STATE), hs_im.reshape(1, bs, S5_GROUPS, S5_STATE))
```
